```python
import math
import jax, jax.numpy as jnp
from jax import lax
import numpy as np

D_MODEL = 1024
BATCH = 2
SEQ = 8192
DEPTH = 2

DA_HEADS = 4
DA_QK = 64
DA_V = 2 * DA_QK
DA_WIDTH = DA_HEADS * DA_V
ROPE_DIM = DA_QK // 4
ROPE_THETA = 500000.0
Q_BLOCK = 128
SC_WIDTH = 256
SC_CONV = 3
S5_WIDTH = 256
S5_GROUP = 16
S5_GROUPS = S5_WIDTH // S5_GROUP
S5_STATE = 64
M2_HEADS = 4
M2_HEAD_DIM = 64
M2_WIDTH = M2_HEADS * M2_HEAD_DIM
M2_GROUPS = 2
M2_STATE = 128
M2_CONV = 4
M2_CHUNK = 128
M2_XBC = M2_WIDTH + 2 * M2_GROUPS * M2_STATE
D_MIX = DA_WIDTH + SC_WIDTH + S5_WIDTH + M2_WIDTH
OFF_Q = 0
OFF_K = OFF_Q + DA_WIDTH
OFF_V = OFF_K + DA_WIDTH
OFF_SC = OFF_V + DA_WIDTH
OFF_S5 = OFF_SC + 3 * SC_WIDTH
OFF_M2Z = OFF_S5 + S5_WIDTH
OFF_M2XBC = OFF_M2Z + M2_WIDTH
OFF_M2DT = OFF_M2XBC + M2_XBC
D_PROJ = OFF_M2DT + M2_HEADS
D_FF = 2816
FFN_CONV = 3
EPS = 1e-6

kernel_name = "hybrid_parallel_heads_diffattn_shortconv_s5_ssd"


def rmsnorm(x, g):
    xf = x.astype(jnp.float32)
    y = xf * lax.rsqrt(jnp.mean(xf * xf, axis=-1, keepdims=True) + EPS)
    return (y * g.astype(jnp.float32)).astype(x.dtype)


def causal_dwconv(x, w):
    k_w, ch = w.shape
    return lax.conv_general_dilated(
        x, w[:, None, :].astype(x.dtype), window_strides=(1,), padding=[(k_w - 1, 0)],
        dimension_numbers=('NWC', 'WIO', 'NWC'), feature_group_count=ch)


def rope_tables(seq):
    inv = 1.0 / (ROPE_THETA ** (jnp.arange(0, ROPE_DIM, 2, dtype=jnp.float32) / ROPE_DIM))
    ang = jnp.arange(seq, dtype=jnp.float32)[:, None] * inv[None, :]
    return jnp.cos(ang), jnp.sin(ang)


def partial_rope(x, cos, sin):
    xr = x[..., :ROPE_DIM].astype(jnp.float32)
    half = ROPE_DIM // 2
    x1, x2 = xr[..., :half], xr[..., half:]
    c = cos[:, None, None, :]
    s = sin[:, None, None, :]
    rot = jnp.concatenate([x1 * c - x2 * s, x1 * s + x2 * c], axis=-1).astype(x.dtype)
    return jnp.concatenate([rot, x[..., ROPE_DIM:]], axis=-1)


def diff_attention(q, k, v, lam, lambda_init, sub_g):
    bsz, seq, nh, _, d = q.shape
    nb = seq // Q_BLOCK
    scale = d ** -0.5
    qb = q.reshape(bsz, nb, Q_BLOCK, nh, 2, d).transpose(1, 0, 2, 3, 4, 5)
    kpos = jnp.arange(seq)

    def block(args):
        qi, i = args
        s = jnp.einsum('bqhcd,bkhcd->bhcqk', qi, k).astype(jnp.float32) * scale
        qpos = i * Q_BLOCK + jnp.arange(Q_BLOCK)
        s = jnp.where(kpos[None, :] <= qpos[:, None], s, -jnp.inf)
        p = jax.nn.softmax(s, axis=-1)
        a = p[:, :, 0] - lam * p[:, :, 1]
        return jnp.einsum('bhqk,bkhe->bqhe', a.astype(v.dtype), v)

    o = lax.map(block, (qb, jnp.arange(nb)))
    o = o.transpose(1, 0, 2, 3, 4).reshape(bsz, seq, nh, 2 * d)
    o = rmsnorm(o, sub_g) * (1.0 - lambda_init)
    return o.reshape(bsz, seq, nh * 2 * d)


def s5_mixer(u, lam_re, lam_im, log_dt, b_re, b_im, c_re, c_im, d_skip, w_glu, b_glu):
    bsz, seq, _ = u.shape
    f32 = jnp.float32
    uf = u.astype(f32).reshape(bsz, seq, S5_GROUPS, S5_GROUP)
    lam = lax.complex(lam_re.astype(f32), lam_im.astype(f32))
    dt = jnp.exp(log_dt.astype(f32))[:, None]
    lam_bar = jnp.exp(lam * dt)
    b = lax.complex(b_re.astype(f32), b_im.astype(f32))
    b_bar = ((lam_bar - 1.0) / lam)[..., None] * b
    bu = jnp.einsum('bsgh,gph->bsgp', uf.astype(jnp.complex64), b_bar)
    a = jnp.broadcast_to(lam_bar, bu.shape)

    def comb(e1, e2):
        a1, x1 = e1
        a2, x2 = e2
        return a2 * a1, a2 * x1 + x2

    _, xs = lax.associative_scan(comb, (a, bu), axis=1)
    c = lax.complex(c_re.astype(f32), c_im.astype(f32))
    y = jnp.einsum('bsgp,ghp->bsgh', xs, c).real \
        + d_skip.astype(f32).reshape(S5_GROUPS, S5_GROUP) * uf
    y = y.reshape(bsz, seq, S5_WIDTH)
    g = jax.nn.gelu(y)
    out = g * jax.nn.sigmoid(g @ w_glu.astype(f32) + b_glu.astype(f32))
    return out.astype(u.dtype)


def segsum(a):
    cs = jnp.cumsum(a, axis=-1)
    seg = cs[..., :, None] - cs[..., None, :]
    n = a.shape[-1]
    return jnp.where(jnp.tril(jnp.ones((n, n), dtype=bool)), seg, -jnp.inf)


def ssd_chunked(x, a, b, c):
    bsz, seq, nh, hp = x.shape
    ns = b.shape[-1]
    nc = seq // M2_CHUNK
    x = x.reshape(bsz, nc, M2_CHUNK, nh, hp)
    b = b.reshape(bsz, nc, M2_CHUNK, nh, ns)
    c = c.reshape(bsz, nc, M2_CHUNK, nh, ns)
    a = a.reshape(bsz, nc, M2_CHUNK, nh).transpose(0, 3, 1, 2)
    a_cs = jnp.cumsum(a, axis=-1)
    decay = jnp.exp(segsum(a))
    scores = jnp.einsum('bclhn,bcshn->bhcls', c, b) * decay
    y_diag = jnp.einsum('bhcls,bcshp->bclhp', scores, x)
    decay_states = jnp.exp(a_cs[..., -1:] - a_cs)
    states = jnp.einsum('bclhn,bhcl,bclhp->bchpn', b, decay_states, x)
    chunk_decay = jnp.exp(a_cs[..., -1])

    def step(h, inp):
        st, dec = inp
        return dec[..., None, None] * h + st, h

    h0 = jnp.zeros((bsz, nh, hp, ns), jnp.float32)
    _, h_in = lax.scan(step, h0, (states.transpose(1, 0, 2, 3, 4), chunk_decay.transpose(2, 0, 1)))
    h_in = h_in.transpose(1, 0, 2, 3, 4)
    y_off = jnp.einsum('bclhn,bchpn,bhcl->bclhp', c, h_in, jnp.exp(a_cs))
    return (y_diag + y_off).reshape(bsz, seq, nh, hp)


def mamba2_mixer(z, xbc, dt_raw, conv_w, conv_b, dt_bias, a_log, d_skip, norm_g):
    bsz, seq, _ = z.shape
    f32 = jnp.float32
    xbc = jax.nn.silu((causal_dwconv(xbc, conv_w) + conv_b.astype(xbc.dtype)).astype(f32))
    xs = xbc[..., :M2_WIDTH].reshape(bsz, seq, M2_HEADS, M2_HEAD_DIM)
    bm = xbc[..., M2_WIDTH:M2_WIDTH + M2_GROUPS * M2_STATE].reshape(bsz, seq, M2_GROUPS, M2_STATE)
    cm = xbc[..., M2_WIDTH + M2_GROUPS * M2_STATE:].reshape(bsz, seq, M2_GROUPS, M2_STATE)
    rep = M2_HEADS // M2_GROUPS
    bm = jnp.repeat(bm, rep, axis=2)
    cm = jnp.repeat(cm, rep, axis=2)
    dt = jax.nn.softplus(dt_raw.astype(f32) + dt_bias.astype(f32))
    a = -jnp.exp(a_log.astype(f32))
    y = ssd_chunked(xs * dt[..., None], a * dt, bm, cm)
    y = y + d_skip.astype(f32)[:, None] * xs
    y = y.reshape(bsz, seq, M2_WIDTH)
    y = rmsnorm(y * jax.nn.silu(z.astype(f32)), norm_g)
    return y.astype(z.dtype)


def setup_inputs(seed: int = 0) -> dict:
    key = jax.random.key(seed)
    ks = iter(jax.random.split(key, 48))
    f32 = jnp.float32

    def nrm(shape, scale):
        return jax.random.normal(next(ks), shape, f32) * scale

    L = DEPTH
    n = jnp.arange(S5_STATE, dtype=f32)
    m2_dt = jnp.exp(jax.random.uniform(next(ks), (L, M2_HEADS), f32, math.log(1e-3), math.log(1e-1)))
    return {
        "x": nrm((BATCH, SEQ, D_MODEL), 1.0),
        "ln1_g": 1.0 + nrm((L, D_MODEL), 0.02),
        "w_in": nrm((L, D_MODEL, D_PROJ), D_MODEL ** -0.5),
        "qk_norm_g": 1.0 + nrm((L, 2, DA_QK), 0.02),
        "da_lambda": nrm((L, 4, DA_QK), 0.1),
        "subln_g": 1.0 + nrm((L, DA_V), 0.02),
        "sc_conv_w": nrm((L, SC_CONV, SC_WIDTH), SC_CONV ** -0.5),
        "s5_lam_re": -0.5 + nrm((L, S5_GROUPS, S5_STATE), 0.01),
        "s5_lam_im": math.pi * n + nrm((L, S5_GROUPS, S5_STATE), 0.01),
        "s5_log_dt": jax.random.uniform(next(ks), (L, S5_GROUPS), f32, math.log(1e-3), math.log(1e-1)),
        "s5_b_re": nrm((L, S5_GROUPS, S5_STATE, S5_GROUP), (2 * S5_GROUP) ** -0.5),
        "s5_b_im": nrm((L, S5_GROUPS, S5_STATE, S5_GROUP), (2 * S5_GROUP) ** -0.5),
        "s5_c_re": nrm((L, S5_GROUPS, S5_GROUP, S5_STATE), (2 * S5_STATE) ** -0.5),
        "s5_c_im": nrm((L, S5_GROUPS, S5_GROUP, S5_STATE), (2 * S5_STATE) ** -0.5),
        "s5_d": nrm((L, S5_WIDTH), 1.0),
        "s5_w_glu": nrm((L, S5_WIDTH, S5_WIDTH), S5_WIDTH ** -0.5),
        "s5_b_glu": nrm((L, S5_WIDTH), 0.01),
        "m2_conv_w": nrm((L, M2_CONV, M2_XBC), M2_CONV ** -0.5),
        "m2_conv_b": nrm((L, M2_XBC), 0.01),
        "m2_dt_bias": m2_dt + jnp.log(-jnp.expm1(-m2_dt)),
        "m2_a_log": jnp.log(jax.random.uniform(next(ks), (L, M2_HEADS), f32, 1.0, 16.0)),
        "m2_d": 1.0 + nrm((L, M2_HEADS), 0.1),
        "m2_norm_g": 1.0 + nrm((L, M2_WIDTH), 0.02),
        "w_out": nrm((L, D_MIX, D_MODEL), D_MIX ** -0.5),
        "ln2_g": 1.0 + nrm((L, D_MODEL), 0.02),
        "ffn_w_gate": nrm((L, D_MODEL, D_FF), D_MODEL ** -0.5),
        "ffn_w_up": nrm((L, D_MODEL, D_FF), D_MODEL ** -0.5),
        "ffn_conv_w": nrm((L, FFN_CONV, D_FF), FFN_CONV ** -0.5),
        "ffn_w_down": nrm((L, D_FF, D_MODEL), D_FF ** -0.5),
    }


def reference(x, ln1_g, w_in, qk_norm_g, da_lambda, subln_g, sc_conv_w,
              s5_lam_re, s5_lam_im, s5_log_dt, s5_b_re, s5_b_im, s5_c_re, s5_c_im,
              s5_d, s5_w_glu, s5_b_glu, m2_conv_w, m2_conv_b, m2_dt_bias, m2_a_log,
              m2_d, m2_norm_g, w_out, ln2_g, ffn_w_gate, ffn_w_up, ffn_conv_w, ffn_w_down):
    bsz, seq, _ = x.shape
    cos, sin = rope_tables(seq)
    for l in range(DEPTH):
        h = rmsnorm(x, ln1_g[l])
        proj = h @ w_in[l]

        q = proj[..., OFF_Q:OFF_K].reshape(bsz, seq, DA_HEADS, 2, DA_QK)
        k = proj[..., OFF_K:OFF_V].reshape(bsz, seq, DA_HEADS, 2, DA_QK)
        v = proj[..., OFF_V:OFF_SC].reshape(bsz, seq, DA_HEADS, DA_V)
        q = partial_rope(rmsnorm(q, qk_norm_g[l, 0]), cos, sin)
        k = partial_rope(rmsnorm(k, qk_norm_g[l, 1]), cos, sin)
        lam_p = da_lambda[l].astype(jnp.float32)
        lambda_init = 0.8 - 0.6 * math.exp(-0.3 * l)
        lam = jnp.exp(jnp.sum(lam_p[0] * lam_p[1])) - jnp.exp(jnp.sum(lam_p[2] * lam_p[3])) + lambda_init
        o_a = diff_attention(q, k, v, lam, lambda_init, subln_g[l])

        sc_b = proj[..., OFF_SC:OFF_SC + SC_WIDTH]
        sc_c = proj[..., OFF_SC + SC_WIDTH:OFF_SC + 2 * SC_WIDTH]
        sc_h = proj[..., OFF_SC + 2 * SC_WIDTH:OFF_S5]
        o_b = sc_b * causal_dwconv(sc_c * sc_h, sc_conv_w[l])

        o_c = s5_mixer(proj[..., OFF_S5:OFF_M2Z], s5_lam_re[l], s5_lam_im[l], s5_log_dt[l],
                       s5_b_re[l], s5_b_im[l], s5_c_re[l], s5_c_im[l], s5_d[l],
                       s5_w_glu[l], s5_b_glu[l])

        o_d = mamba2_mixer(proj[..., OFF_M2Z:OFF_M2XBC], proj[..., OFF_M2XBC:OFF_M2DT],
                           proj[..., OFF_M2DT:D_PROJ], m2_conv_w[l], m2_conv_b[l],
                           m2_dt_bias[l], m2_a_log[l], m2_d[l], m2_norm_g[l])

        mixed = jnp.concatenate([o_a.astype(x.dtype), o_b.astype(x.dtype),
                                 o_c.astype(x.dtype), o_d.astype(x.dtype)], axis=-1)
        x = x + mixed @ w_out[l]

        h2 = rmsnorm(x, ln2_g[l])
        g = causal_dwconv(h2 @ ffn_w_gate[l], ffn_conv_w[l])
        x = x + (jax.nn.silu(g) * (h2 @ ffn_w_up[l])) @ ffn_w_down[l]
    return x
```

```python
import functools
import math

import jax
import jax.numpy as jnp
from jax import lax
from jax.experimental import pallas as pl
from jax.experimental.pallas import tpu as pltpu

F32 = jnp.float32
BF16 = jnp.bfloat16

D_MODEL = 1024
DA_HEADS = 4
DA_QK = 64
DA_V = 2 * DA_QK
DA_WIDTH = DA_HEADS * DA_V
ROPE_DIM = DA_QK // 4
ROPE_THETA = 500000.0
SC_WIDTH = 256
S5_WIDTH = 256
S5_GROUP = 16
S5_GROUPS = S5_WIDTH // S5_GROUP
S5_STATE = 64
M2_HEADS = 4
M2_HEAD_DIM = 64
M2_WIDTH = M2_HEADS * M2_HEAD_DIM
M2_GROUPS = 2
M2_STATE = 128
M2_XBC = M2_WIDTH + 2 * M2_GROUPS * M2_STATE
D_MIX = DA_WIDTH + SC_WIDTH + S5_WIDTH + M2_WIDTH
OFF_SC = 3 * DA_WIDTH
D_PROJ = OFF_SC + 3 * SC_WIDTH + S5_WIDTH + M2_WIDTH + M2_XBC + M2_HEADS
D_FF = 2816
EPS = 1e-6

LANES = 128
HIST = 8
D_PROJ_PAD = D_PROJ - M2_HEADS + LANES
D_REST = D_PROJ_PAD - OFF_SC
S5_CHUNK = 32
VMEM_LIMIT = 56 * 1024 * 1024

R_SC = 0
R_S5 = R_SC + 3 * SC_WIDTH
R_Z = R_S5 + S5_WIDTH
R_XBC = R_Z + M2_WIDTH
R_DT = R_XBC + M2_XBC


def _dot(a, b):
    return jnp.dot(a, b, preferred_element_type=F32)


def _dot_nt(a, b):
    return lax.dot_general(a, b, (((1,), (1,)), ((), ())), preferred_element_type=F32)


def _dot_tn(a, b):
    return lax.dot_general(a, b, (((0,), (0,)), ((), ())), preferred_element_type=F32)


def _split_bf16(x, parts):
    out = []
    r = x
    for _ in range(parts):
        p = r.astype(BF16)
        out.append(p)
        r = r - p.astype(F32)
    return out


def _silu(x):
    return x * (1.0 / (1.0 + jnp.exp(-x)))


def _const_spec(shape):
    nd = len(shape)
    return pl.BlockSpec(shape, lambda *_: (0,) * nd, pipeline_mode=pl.Buffered(1))


def _inproj_kernel(x_ref, g1_ref, w_ref, qkg_ref, cos_ref, sin_ref, gm_ref,
                   q_ref, k_ref, v_ref, rest_ref):
    tm = x_ref.shape[0]
    x = x_ref[...]
    ms = jnp.mean(x * x, axis=-1, keepdims=True)
    hn = (x * lax.rsqrt(ms + EPS) * g1_ref[...]).astype(BF16)
    cosv = cos_ref[...]
    sinv = sin_ref[...]
    gm = gm_ref[...]
    lane = lax.broadcasted_iota(jnp.int32, (tm, LANES), 1)
    pair_up = (lane % DA_QK) < (ROPE_DIM // 2)

    def qk_cols(col0, gvec, scale):
        y = _dot(hn, w_ref[:, col0:col0 + LANES])
        hi, lo = _split_bf16(y * y, 2)
        ss = _dot(hi, gm) + _dot(lo, gm)
        yn = y * lax.rsqrt(ss * (1.0 / DA_QK) + EPS) * gvec
        partner = jnp.where(pair_up,
                            pltpu.roll(yn, LANES - ROPE_DIM // 2, 1),
                            pltpu.roll(yn, ROPE_DIM // 2, 1))
        return ((yn * cosv + partner * sinv) * scale).astype(BF16)

    for h in range(DA_HEADS):
        sl = slice(h * LANES, (h + 1) * LANES)
        q_ref[:, sl] = qk_cols(h * LANES, qkg_ref[0:1, :], DA_QK ** -0.5)
        k_ref[:, sl] = qk_cols(DA_WIDTH + h * LANES, qkg_ref[1:2, :], 1.0)
    v_ref[...] = _dot(hn, w_ref[:, 2 * DA_WIDTH:3 * DA_WIDTH]).astype(BF16)
    rest_ref[...] = _dot(hn, w_ref[:, OFF_SC:D_PROJ_PAD])


def _inproj(x2, g1, w, qkg, cos_t, sin_t, gm, seq, tm):
    t = x2.shape[0]
    n_seq_tiles = seq // tm
    row = lambda i: (i, 0)
    tab = lambda i: (i % n_seq_tiles, 0)
    return pl.pallas_call(
        _inproj_kernel,
        grid=(t // tm,),
        in_specs=[
            pl.BlockSpec((tm, D_MODEL), row),
            _const_spec((1, D_MODEL)),
            _const_spec((D_MODEL, D_PROJ_PAD)),
            _const_spec((2, LANES)),
            pl.BlockSpec((tm, LANES), tab),
            pl.BlockSpec((tm, LANES), tab),
            _const_spec((LANES, LANES)),
        ],
        out_specs=[
            pl.BlockSpec((tm, DA_WIDTH), row),
            pl.BlockSpec((tm, DA_WIDTH), row),
            pl.BlockSpec((tm, DA_WIDTH), row),
            pl.BlockSpec((tm, D_REST), row),
        ],
        out_shape=[
            jax.ShapeDtypeStruct((t, DA_WIDTH), BF16),
            jax.ShapeDtypeStruct((t, DA_WIDTH), BF16),
            jax.ShapeDtypeStruct((t, DA_WIDTH), BF16),
            jax.ShapeDtypeStruct((t, D_REST), F32),
        ],
        compiler_params=pltpu.CompilerParams(
            dimension_semantics=("arbitrary",), vmem_limit_bytes=VMEM_LIMIT),
        name="inproj",
    )(x2, g1, w, qkg, cos_t, sin_t, gm)


def _attn_kernel(q_ref, k_ref, v_ref, lam_ref, sg_ref, o_ref, m_sc, l_sc, acc_sc,
                 *, blk, lambda_init):
    qi = pl.program_id(2)
    q = q_ref[...]
    qs = (q[:, :DA_QK], q[:, DA_QK:])
    m_sc[...] = jnp.full(m_sc.shape, -1e30, F32)
    l_sc[...] = jnp.zeros(l_sc.shape, F32)
    acc_sc[...] = jnp.zeros(acc_sc.shape, F32)

    def update(kb, vb, masked):
        for c in range(2):
            s = _dot_nt(qs[c], kb[:, c * DA_QK:(c + 1) * DA_QK])
            if masked:
                r = lax.broadcasted_iota(jnp.int32, s.shape, 0)
                col = lax.broadcasted_iota(jnp.int32, s.shape, 1)
                s = jnp.where(col <= r, s, -jnp.inf)
            m_prev = m_sc[c]
            m_new = jnp.maximum(m_prev, jnp.max(s, axis=1, keepdims=True))
            alpha = jnp.exp(m_prev - m_new)
            p = jnp.exp(s - m_new)
            l_sc[c] = alpha * l_sc[c] + jnp.sum(p, axis=1, keepdims=True)
            acc_sc[c] = alpha * acc_sc[c] + _dot(p.astype(BF16), vb)
            m_sc[c] = m_new

    def body(j, carry):
        off = pl.multiple_of(j * blk, blk)
        update(k_ref[pl.ds(off, blk), :], v_ref[pl.ds(off, blk), :], False)
        return carry

    lax.fori_loop(0, qi, body, 0)
    off = pl.multiple_of(qi * blk, blk)
    update(k_ref[pl.ds(off, blk), :], v_ref[pl.ds(off, blk), :], True)

    lp = lam_ref[...]
    lam = (jnp.exp(jnp.sum(lp[0:1] * lp[1:2], axis=1, keepdims=True))
           - jnp.exp(jnp.sum(lp[2:3] * lp[3:4], axis=1, keepdims=True)) + lambda_init)
    o = acc_sc[0] / l_sc[0] - lam * (acc_sc[1] / l_sc[1])
    ms = jnp.mean(o * o, axis=-1, keepdims=True)
    o = o * lax.rsqrt(ms + EPS) * sg_ref[...] * (1.0 - lambda_init)
    o_ref[...] = o.astype(o_ref.dtype)


def _attention(q, k, v, lam_p, sub_g, lambda_init, blk):
    bsz, seq, _ = q.shape
    kern = functools.partial(_attn_kernel, blk=blk, lambda_init=lambda_init)
    return pl.pallas_call(
        kern,
        grid=(bsz, DA_HEADS, seq // blk),
        in_specs=[
            pl.BlockSpec((None, blk, DA_V), lambda b, h, i: (b, i, h)),
            pl.BlockSpec((None, seq, DA_V), lambda b, h, i: (b, 0, h)),
            pl.BlockSpec((None, seq, DA_V), lambda b, h, i: (b, 0, h)),
            pl.BlockSpec((4, DA_QK), lambda b, h, i: (0, 0)),
            pl.BlockSpec((1, DA_V), lambda b, h, i: (0, 0)),
        ],
        out_specs=pl.BlockSpec((None, blk, DA_V), lambda b, h, i: (b, i, h)),
        out_shape=jax.ShapeDtypeStruct((bsz, seq, DA_WIDTH), BF16),
        scratch_shapes=[
            pltpu.VMEM((2, blk, 1), F32),
            pltpu.VMEM((2, blk, 1), F32),
            pltpu.VMEM((2, blk, DA_V), F32),
        ],
        compiler_params=pltpu.CompilerParams(
            dimension_semantics=("arbitrary", "arbitrary", "arbitrary"),
            vmem_limit_bytes=VMEM_LIMIT),
        name="diff_attn",
    )(q, k, v, lam_p, sub_g)


def _s5_kernel(u_ref, toep_ref, bre_ref, bim_ref, cre_ref, cim_ref, lam_ref, y_ref,
               sre, sim, *, nb, nc):
    u = u_ref[...]
    sre[...] = _dot(u, bre_ref[...])
    sim[...] = _dot(u, bim_ref[...])
    lr = lam_ref[0:1, :]
    li = lam_ref[1:2, :]

    def body(c, carry):
        new = []
        for b in range(nb):
            hr, hi = carry[2 * b], carry[2 * b + 1]
            r = b * nc + c
            s_r = sre[pl.ds(r, 1), :]
            s_i = sim[pl.ds(r, 1), :]
            sre[pl.ds(r, 1), :] = hr
            sim[pl.ds(r, 1), :] = hi
            new.append(lr * hr - li * hi + s_r)
            new.append(lr * hi + li * hr + s_i)
        return tuple(new)

    zero = jnp.zeros((1, LANES), F32)
    lax.fori_loop(0, nc, body, (zero,) * (2 * nb))
    y = _dot(u, toep_ref[...])
    y = y + _dot(sre[...].astype(BF16), cre_ref[...])
    y = y + _dot(sim[...].astype(BF16), cim_ref[...])
    y_ref[...] = y


def _s5_scan(u1, toep, bre, bim, cre, cim, lam_l, nb, nc):
    g, r, w = u1.shape
    kern = functools.partial(_s5_kernel, nb=nb, nc=nc)
    grp = lambda i: (i, 0, 0)
    return pl.pallas_call(
        kern,
        grid=(g,),
        in_specs=[
            pl.BlockSpec((None, r, w), grp),
            pl.BlockSpec((None, w, w), grp),
            pl.BlockSpec((None, w, LANES), grp),
            pl.BlockSpec((None, w, LANES), grp),
            pl.BlockSpec((None, LANES, w), grp),
            pl.BlockSpec((None, LANES, w), grp),
            pl.BlockSpec((None, 2, LANES), grp),
        ],
        out_specs=pl.BlockSpec((None, r, w), grp),
        out_shape=jax.ShapeDtypeStruct((g, r, w), F32),
        scratch_shapes=[pltpu.VMEM((r, LANES), F32), pltpu.VMEM((r, LANES), F32)],
        compiler_params=pltpu.CompilerParams(
            dimension_semantics=("arbitrary",), vmem_limit_bytes=VMEM_LIMIT),
        name="s5_scan",
    )(u1, toep, bre, bim, cre, cim, lam_l)


def _s5_operators(lam_re, lam_im, log_dt, b_re, b_im, c_re, c_im, ls):
    hp = lax.Precision.HIGHEST
    dt = jnp.exp(log_dt)[:, None]
    zr, zi = lam_re * dt, lam_im * dt
    er = jnp.exp(zr)
    lbr, lbi = er * jnp.cos(zi), er * jnp.sin(zi)
    den = lam_re * lam_re + lam_im * lam_im
    nr, ni = lbr - 1.0, lbi
    fr = (nr * lam_re + ni * lam_im) / den
    fi = (ni * lam_re - nr * lam_im) / den
    bbr = fr[..., None] * b_re - fi[..., None] * b_im
    bbi = fr[..., None] * b_im + fi[..., None] * b_re
    tau = jnp.arange(ls + 1, dtype=F32)
    pe = jnp.exp(zr[..., None] * tau)
    pr, pi_ = pe * jnp.cos(zi[..., None] * tau), pe * jnp.sin(zi[..., None] * tau)
    cpr = c_re[:, :, :, None] * pr[:, None, :, :ls] - c_im[:, :, :, None] * pi_[:, None, :, :ls]
    cpi = c_re[:, :, :, None] * pi_[:, None, :, :ls] + c_im[:, :, :, None] * pr[:, None, :, :ls]
    kk = (jnp.einsum('ghpt,gpk->gthk', cpr, bbr, precision=hp)
          - jnp.einsum('ghpt,gpk->gthk', cpi, bbi, precision=hp))
    s_idx = jnp.arange(ls)[:, None]
    t_idx = jnp.arange(ls)[None, :]
    lag = t_idx - s_idx
    kt = kk[:, jnp.clip(lag, 0, ls - 1)]
    kt = jnp.where((lag >= 0)[None, :, :, None, None], kt, 0.0)
    g = kk.shape[0]
    w = ls * S5_GROUP
    toep = kt.transpose(0, 1, 4, 2, 3).reshape(g, w, w)
    rev_r, rev_i = pr[:, :, ls - 1::-1][:, :, :ls], pi_[:, :, ls - 1::-1][:, :, :ls]
    bre = rev_r[..., None] * bbr[:, :, None, :] - rev_i[..., None] * bbi[:, :, None, :]
    bim = rev_r[..., None] * bbi[:, :, None, :] + rev_i[..., None] * bbr[:, :, None, :]
    pad = LANES - S5_STATE
    bre = jnp.pad(bre.transpose(0, 2, 3, 1).reshape(g, w, S5_STATE), ((0, 0), (0, 0), (0, pad)))
    bim = jnp.pad(bim.transpose(0, 2, 3, 1).reshape(g, w, S5_STATE), ((0, 0), (0, 0), (0, pad)))
    qr = c_re[:, :, :, None] * pr[:, None, :, 1:] - c_im[:, :, :, None] * pi_[:, None, :, 1:]
    qi = c_re[:, :, :, None] * pi_[:, None, :, 1:] + c_im[:, :, :, None] * pr[:, None, :, 1:]
    cre = jnp.pad(qr.transpose(0, 2, 3, 1).reshape(g, S5_STATE, w), ((0, 0), (0, pad), (0, 0)))
    cim = jnp.pad((-qi).transpose(0, 2, 3, 1).reshape(g, S5_STATE, w), ((0, 0), (0, pad), (0, 0)))
    lam_l = jnp.pad(jnp.stack([pr[:, :, ls], pi_[:, :, ls]], axis=1), ((0, 0), (0, 0), (0, pad)))
    return (toep.astype(BF16), bre.astype(BF16), bim.astype(BF16),
            cre.astype(BF16), cim.astype(BF16), lam_l.astype(F32))


def _mix_kernel(rest_ref, ys5_ref, scw_ref, s5d_ref, wglu_ref, bglu_ref, mcw_ref, mcb_ref,
                dtb_ref, alog_ref, dvec_ref, ng_ref, o_ref, cbuf, mbuf, hst, ybuf, *, L):
    @pl.when(pl.program_id(1) == 0)
    def _():
        cbuf[0:HIST, :] = jnp.zeros((HIST, SC_WIDTH), F32)
        mbuf[0:HIST, :] = jnp.zeros((HIST, M2_XBC), F32)
        hst[...] = jnp.zeros(hst.shape, F32)

    u = rest_ref[:, R_SC + SC_WIDTH:R_SC + 2 * SC_WIDTH] * rest_ref[:, R_SC + 2 * SC_WIDTH:R_SC + 3 * SC_WIDTH]
    cbuf[HIST:HIST + L, :] = u
    conv = (scw_ref[2:3, :] * u + scw_ref[1:2, :] * cbuf[HIST - 1:HIST - 1 + L, :]
            + scw_ref[0:1, :] * cbuf[HIST - 2:HIST - 2 + L, :])
    cbuf[0:HIST, :] = cbuf[L:L + HIST, :]
    o_ref[:, 0:SC_WIDTH] = (rest_ref[:, R_SC:R_SC + SC_WIDTH] * conv).astype(o_ref.dtype)

    y5 = ys5_ref[...] + s5d_ref[...] * rest_ref[:, R_S5:R_S5 + S5_WIDTH]
    gl = jax.nn.gelu(y5, approximate=True)
    gate = _dot(gl.astype(BF16), wglu_ref[...]) + bglu_ref[...]
    o_ref[:, SC_WIDTH:SC_WIDTH + S5_WIDTH] = (gl * (1.0 / (1.0 + jnp.exp(-gate)))).astype(o_ref.dtype)

    xr = rest_ref[:, R_XBC:R_XBC + M2_XBC]
    mbuf[HIST:HIST + L, :] = xr
    conv = (mcw_ref[3:4, :] * xr + mcw_ref[2:3, :] * mbuf[HIST - 1:HIST - 1 + L, :]
            + mcw_ref[1:2, :] * mbuf[HIST - 2:HIST - 2 + L, :]
            + mcw_ref[0:1, :] * mbuf[HIST - 3:HIST - 3 + L, :] + mcb_ref[...])
    mbuf[0:HIST, :] = mbuf[L:L + HIST, :]
    xbc = _silu(conv)
    xs = xbc[:, 0:M2_WIDTH]
    dtr = rest_ref[:, R_DT:R_DT + LANES] + dtb_ref[...]
    dtv = jnp.maximum(dtr, 0.0) + jnp.log(1.0 + jnp.exp(-jnp.abs(dtr)))
    a = -jnp.exp(alog_ref[...]) * dtv
    r_i = lax.broadcasted_iota(jnp.int32, (L, L), 0)
    c_i = lax.broadcasted_iota(jnp.int32, (L, L), 1)
    tril = c_i <= r_i
    tri = jnp.where(tril, 1.0, 0.0).astype(BF16)
    cs = sum(_dot(tri, part) for part in _split_bf16(a, 3))
    cs_t = cs.T
    cs_last = cs[L - 1:L, :]
    ecs = jnp.exp(cs)
    dec = jnp.exp(cs_last - cs)
    ecl = jnp.exp(cs_last)
    lane = lax.broadcasted_iota(jnp.int32, (1, LANES), 1)
    for g in range(M2_GROUPS):
        bg = xbc[:, M2_WIDTH + g * M2_STATE:M2_WIDTH + (g + 1) * M2_STATE].astype(BF16)
        cg = xbc[:, M2_WIDTH + (M2_GROUPS + g) * M2_STATE:M2_WIDTH + (M2_GROUPS + g + 1) * M2_STATE].astype(BF16)
        gram = _dot_nt(cg, bg)
        hs = hst[g]
        yoff = _dot(cg, hs.astype(BF16))
        xdec = []
        for hh in range(2):
            h = 2 * g + hh
            seg = cs[:, h:h + 1] - cs_t[h:h + 1, :]
            dm = jnp.exp(jnp.where(tril, seg, -jnp.inf))
            xdt = xs[:, h * M2_HEAD_DIM:(h + 1) * M2_HEAD_DIM] * dtv[:, h:h + 1]
            yd = _dot((gram * dm).astype(BF16), xdt.astype(BF16))
            yo = yoff[:, hh * M2_HEAD_DIM:(hh + 1) * M2_HEAD_DIM] * ecs[:, h:h + 1]
            ybuf[:, h * M2_HEAD_DIM:(h + 1) * M2_HEAD_DIM] = yd + yo
            xdec.append(xdt * dec[:, h:h + 1])
        upd = _dot_tn(bg, jnp.concatenate(xdec, axis=1).astype(BF16))
        keep = jnp.where(lane < M2_HEAD_DIM, ecl[:, 2 * g:2 * g + 1], ecl[:, 2 * g + 1:2 * g + 2])
        hst[g] = hs * keep + upd
    y = ybuf[...] + dvec_ref[...] * xs
    yg = y * _silu(rest_ref[:, R_Z:R_Z + M2_WIDTH])
    ms = jnp.mean(yg * yg, axis=-1, keepdims=True)
    o_ref[:, SC_WIDTH + S5_WIDTH:] = (yg * lax.rsqrt(ms + EPS) * ng_ref[...]).astype(o_ref.dtype)


def _mixers(rest, ys5, scw, s5d, wglu, bglu, mcw, mcb, dtb, alog, dvec, ng, L):
    bsz, seq, _ = rest.shape
    kern = functools.partial(_mix_kernel, L=L)
    blk = lambda b, c: (b, c, 0)
    wout = SC_WIDTH + S5_WIDTH + M2_WIDTH
    return pl.pallas_call(
        kern,
        grid=(bsz, seq // L),
        in_specs=[
            pl.BlockSpec((None, L, D_REST), blk),
            pl.BlockSpec((None, L, S5_WIDTH), blk),
            _const_spec((3, SC_WIDTH)),
            _const_spec((1, S5_WIDTH)),
            _const_spec((S5_WIDTH, S5_WIDTH)),
            _const_spec((1, S5_WIDTH)),
            _const_spec((4, M2_XBC)),
            _const_spec((1, M2_XBC)),
            _const_spec((1, LANES)),
            _const_spec((1, LANES)),
            _const_spec((1, M2_WIDTH)),
            _const_spec((1, M2_WIDTH)),
        ],
        out_specs=pl.BlockSpec((None, L, wout), blk),
        out_shape=jax.ShapeDtypeStruct((bsz, seq, wout), BF16),
        scratch_shapes=[
            pltpu.VMEM((L + HIST, SC_WIDTH), F32),
            pltpu.VMEM((L + HIST, M2_XBC), F32),
            pltpu.VMEM((M2_GROUPS, M2_STATE, 2 * M2_HEAD_DIM), F32),
            pltpu.VMEM((L, M2_WIDTH), F32),
        ],
        compiler_params=pltpu.CompilerParams(
            dimension_semantics=("arbitrary", "arbitrary"), vmem_limit_bytes=VMEM_LIMIT),
        name="mixers",
    )(rest, ys5, scw, s5d, wglu, bglu, mcw, mcb, dtb, alog, dvec, ng)


def _ffn_kernel(x_ref, oa_ref, ob_ref, woa_ref, wob_ref, g2_ref, wg_ref, wu_ref, cw_ref, wd_ref,
                o_ref, hist, cb, *, tiles_per_seq, tf):
    tm = x_ref.shape[0]

    @pl.when(pl.program_id(0) % tiles_per_seq == 0)
    def _():
        hist[...] = jnp.zeros(hist.shape, F32)

    x1 = x_ref[...] + _dot(oa_ref[...], woa_ref[...]) + _dot(ob_ref[...], wob_ref[...])
    ms = jnp.mean(x1 * x1, axis=-1, keepdims=True)
    h2 = (x1 * lax.rsqrt(ms + EPS) * g2_ref[...]).astype(BF16)
    o_ref[...] = x1
    for c in range(D_FF // tf):
        sl = slice(c * tf, (c + 1) * tf)
        gpre = _dot(h2, wg_ref[:, sl])
        cb[0:HIST, :] = hist[:, sl]
        cb[HIST:HIST + tm, :] = gpre
        gc = (cw_ref[2:3, sl] * gpre + cw_ref[1:2, sl] * cb[HIST - 1:HIST - 1 + tm, :]
              + cw_ref[0:1, sl] * cb[HIST - 2:HIST - 2 + tm, :])
        hist[:, sl] = cb[tm:tm + HIST, :]
        act = (_silu(gc) * _dot(h2, wu_ref[:, sl])).astype(BF16)
        o_ref[...] += _dot(act, wd_ref[sl, :])


def _outproj_ffn(x2, oa, ob, woa, wob, g2, wg, wu, cw, wd, seq, tm, tf):
    t = x2.shape[0]
    wb = ob.shape[1]
    kern = functools.partial(_ffn_kernel, tiles_per_seq=seq // tm, tf=tf)
    row = lambda i: (i, 0)
    return pl.pallas_call(
        kern,
        grid=(t // tm,),
        in_specs=[
            pl.BlockSpec((tm, D_MODEL), row),
            pl.BlockSpec((tm, DA_WIDTH), row),
            pl.BlockSpec((tm, wb), row),
            _const_spec((DA_WIDTH, D_MODEL)),
            _const_spec((wb, D_MODEL)),
            _const_spec((1, D_MODEL)),
            _const_spec((D_MODEL, D_FF)),
            _const_spec((D_MODEL, D_FF)),
            _const_spec((3, D_FF)),
            _const_spec((D_FF, D_MODEL)),
        ],
        out_specs=pl.BlockSpec((tm, D_MODEL), row),
        out_shape=jax.ShapeDtypeStruct((t, D_MODEL), F32),
        scratch_shapes=[
            pltpu.VMEM((HIST, D_FF), F32),
            pltpu.VMEM((tm + HIST, tf), F32),
        ],
        compiler_params=pltpu.CompilerParams(
            dimension_semantics=("arbitrary",), vmem_limit_bytes=VMEM_LIMIT),
        name="outproj_ffn",
    )(x2, oa, ob, woa, wob, g2, wg, wu, cw, wd)


def _rope_lane_tables(seq):
    inv = 1.0 / (ROPE_THETA ** (jnp.arange(0, ROPE_DIM, 2, dtype=F32) / ROPE_DIM))
    ang = jnp.arange(seq, dtype=F32)[:, None] * inv[None, :]
    cos, sin = jnp.cos(ang), jnp.sin(ang)
    half = ROPE_DIM // 2
    ones = jnp.ones((seq, DA_QK - ROPE_DIM), F32)
    cos64 = jnp.concatenate([cos, cos, ones], axis=1)
    sin64 = jnp.concatenate([-sin, sin, 0.0 * ones], axis=1)
    del half
    return jnp.tile(cos64, (1, LANES // DA_QK)), jnp.tile(sin64, (1, LANES // DA_QK))


def kernel(x, ln1_g, w_in, qk_norm_g, da_lambda, subln_g, sc_conv_w, s5_lam_re, s5_lam_im, s5_log_dt, s5_b_re, s5_b_im, s5_c_re, s5_c_im, s5_d, s5_w_glu, s5_b_glu, m2_conv_w, m2_conv_b, m2_dt_bias, m2_a_log, m2_d, m2_norm_g, w_out, ln2_g, ffn_w_gate, ffn_w_up, ffn_conv_w, ffn_w_down):
    bsz, seq, _ = x.shape
    depth = w_in.shape[0]
    t = bsz * seq
    tm = min(512, seq)
    blk = min(256, seq)
    mix_l = min(256, seq)
    ls = S5_CHUNK
    nc = seq // ls
    assert seq % tm == 0 and seq % blk == 0 and seq % mix_l == 0 and seq % ls == 0

    cos_t, sin_t = _rope_lane_tables(seq)
    comp = lax.broadcasted_iota(jnp.int32, (LANES, LANES), 0) // DA_QK
    gm = (comp == comp.T).astype(BF16)
    x2 = x.reshape(t, D_MODEL)
    for l in range(depth):
        lambda_init = 0.8 - 0.6 * math.exp(-0.3 * l)
        w_pad = jnp.pad(w_in[l], ((0, 0), (0, D_PROJ_PAD - D_PROJ))).astype(BF16)
        qkg = jnp.tile(qk_norm_g[l], (1, LANES // DA_QK))
        q, k, v, rest = _inproj(x2, ln1_g[l][None, :], w_pad, qkg, cos_t, sin_t, gm, seq, tm)

        oa = _attention(q.reshape(bsz, seq, DA_WIDTH), k.reshape(bsz, seq, DA_WIDTH),
                        v.reshape(bsz, seq, DA_WIDTH), da_lambda[l], subln_g[l][None, :],
                        lambda_init, blk)

        rest3 = rest.reshape(bsz, seq, D_REST)
        ops = _s5_operators(s5_lam_re[l], s5_lam_im[l], s5_log_dt[l], s5_b_re[l], s5_b_im[l],
                            s5_c_re[l], s5_c_im[l], ls)
        u5 = rest3[:, :, R_S5:R_S5 + S5_WIDTH].astype(BF16)
        u1 = u5.reshape(bsz, nc, ls, S5_GROUPS, S5_GROUP).transpose(3, 0, 1, 2, 4)
        u1 = u1.reshape(S5_GROUPS, bsz * nc, ls * S5_GROUP)
        y1 = _s5_scan(u1, *ops, bsz, nc)
        ys5 = y1.reshape(S5_GROUPS, bsz, nc, ls, S5_GROUP).transpose(1, 2, 3, 0, 4)
        ys5 = ys5.reshape(bsz, seq, S5_WIDTH)

        pad4 = lambda a: jnp.pad(a, (0, LANES - M2_HEADS))[None, :]
        ob = _mixers(rest3, ys5, sc_conv_w[l], s5_d[l][None, :], s5_w_glu[l].astype(BF16),
                     s5_b_glu[l][None, :], m2_conv_w[l], m2_conv_b[l][None, :],
                     pad4(m2_dt_bias[l]), pad4(m2_a_log[l]),
                     jnp.repeat(m2_d[l], M2_HEAD_DIM)[None, :], m2_norm_g[l][None, :], mix_l)

        wo = w_out[l].astype(BF16)
        x2 = _outproj_ffn(x2, oa.reshape(t, DA_WIDTH), ob.reshape(t, D_MIX - DA_WIDTH),
                          wo[:DA_WIDTH], wo[DA_WIDTH:], ln2_g[l][None, :],
                          ffn_w_gate[l].astype(BF16), ffn_w_up[l].astype(BF16), ffn_conv_w[l],
                          ffn_w_down[l].astype(BF16), seq, tm, 256)
    return x2.reshape(bsz, seq, D_MODEL)
```

```python
import functools
import math

import jax
import jax.numpy as jnp
from jax import lax
from jax.experimental import pallas as pl
from jax.experimental.pallas import tpu as pltpu

F32 = jnp.float32
BF16 = jnp.bfloat16

D_MODEL = 1024
DA_HEADS = 4
DA_QK = 64
DA_V = 2 * DA_QK
DA_WIDTH = DA_HEADS * DA_V
ROPE_DIM = DA_QK // 4
ROPE_THETA = 500000.0
SC_WIDTH = 256
S5_WIDTH = 256
S5_GROUP = 16
S5_GROUPS = S5_WIDTH // S5_GROUP
S5_STATE = 64
M2_HEADS = 4
M2_HEAD_DIM = 64
M2_WIDTH = M2_HEADS * M2_HEAD_DIM
M2_GROUPS = 2
M2_STATE = 128
M2_XBC = M2_WIDTH + 2 * M2_GROUPS * M2_STATE
D_MIX = DA_WIDTH + SC_WIDTH + S5_WIDTH + M2_WIDTH
OFF_SC = 3 * DA_WIDTH
D_PROJ = OFF_SC + 3 * SC_WIDTH + S5_WIDTH + M2_WIDTH + M2_XBC + M2_HEADS
D_FF = 2816
EPS = 1e-6
LOG2E = 1.4426950408889634

LANES = 128
HIST = 8
D_PROJ_PAD = D_PROJ - M2_HEADS + LANES
D_REST = D_PROJ_PAD - OFF_SC
S5_CHUNK = 32
VMEM_LIMIT = 56 * 1024 * 1024

R_SC = 0
R_S5 = R_SC + 3 * SC_WIDTH
R_Z = R_S5 + S5_WIDTH
R_XBC = R_Z + M2_WIDTH
R_DT = R_XBC + M2_XBC


def _dot(a, b):
    return jnp.dot(a, b, preferred_element_type=F32)


def _dot_nt(a, b):
    return lax.dot_general(a, b, (((1,), (1,)), ((), ())), preferred_element_type=F32)


def _dot_tn(a, b):
    return lax.dot_general(a, b, (((0,), (0,)), ((), ())), preferred_element_type=F32)


def _split_bf16(x, parts):
    out = []
    r = x
    for _ in range(parts):
        p = r.astype(BF16)
        out.append(p)
        r = r - p.astype(F32)
    return out


def _silu(x):
    return x * (1.0 / (1.0 + jnp.exp(-x)))


def _const_spec(shape):
    nd = len(shape)
    return pl.BlockSpec(shape, lambda *_: (0,) * nd, pipeline_mode=pl.Buffered(1))


def _inproj_kernel(x_ref, g1_ref, w_ref, qkg_ref, cos_ref, sin_ref, gm_ref,
                   q_ref, k_ref, v_ref, rest_ref):
    tm = x_ref.shape[0]
    x = x_ref[...]
    ms = jnp.mean(x * x, axis=-1, keepdims=True)
    hn = (x * lax.rsqrt(ms + EPS) * g1_ref[...]).astype(BF16)
    cosv = cos_ref[...]
    sinv = sin_ref[...]
    gm = gm_ref[...]
    lane = lax.broadcasted_iota(jnp.int32, (tm, LANES), 1)
    pair_up = (lane % DA_QK) < (ROPE_DIM // 2)

    def qk_cols(col0, gvec, scale):
        y = _dot(hn, w_ref[:, col0:col0 + LANES])
        hi, lo = _split_bf16(y * y, 2)
        ss = _dot(hi, gm) + _dot(lo, gm)
        yn = y * lax.rsqrt(ss * (1.0 / DA_QK) + EPS) * gvec
        partner = jnp.where(pair_up,
                            pltpu.roll(yn, LANES - ROPE_DIM // 2, 1),
                            pltpu.roll(yn, ROPE_DIM // 2, 1))
        return ((yn * cosv + partner * sinv) * scale).astype(BF16)

    for h in range(DA_HEADS):
        sl = slice(h * LANES, (h + 1) * LANES)
        q_ref[:, sl] = qk_cols(h * LANES, qkg_ref[0:1, :], DA_QK ** -0.5 * LOG2E)
        k_ref[:, sl] = qk_cols(DA_WIDTH + h * LANES, qkg_ref[1:2, :], 1.0)
    v_ref[...] = _dot(hn, w_ref[:, 2 * DA_WIDTH:3 * DA_WIDTH]).astype(BF16)
    rest_ref[...] = _dot(hn, w_ref[:, OFF_SC:D_PROJ_PAD])


def _inproj(x2, g1, w, qkg, cos_t, sin_t, gm, seq, tm):
    t = x2.shape[0]
    n_seq_tiles = seq // tm
    row = lambda i: (i, 0)
    tab = lambda i: (i % n_seq_tiles, 0)
    return pl.pallas_call(
        _inproj_kernel,
        grid=(t // tm,),
        in_specs=[
            pl.BlockSpec((tm, D_MODEL), row),
            _const_spec((1, D_MODEL)),
            _const_spec((D_MODEL, D_PROJ_PAD)),
            _const_spec((2, LANES)),
            pl.BlockSpec((tm, LANES), tab),
            pl.BlockSpec((tm, LANES), tab),
            _const_spec((LANES, LANES)),
        ],
        out_specs=[
            pl.BlockSpec((tm, DA_WIDTH), row),
            pl.BlockSpec((tm, DA_WIDTH), row),
            pl.BlockSpec((tm, DA_WIDTH), row),
            pl.BlockSpec((tm, D_REST), row),
        ],
        out_shape=[
            jax.ShapeDtypeStruct((t, DA_WIDTH), BF16),
            jax.ShapeDtypeStruct((t, DA_WIDTH), BF16),
            jax.ShapeDtypeStruct((t, DA_WIDTH), BF16),
            jax.ShapeDtypeStruct((t, D_REST), F32),
        ],
        compiler_params=pltpu.CompilerParams(
            dimension_semantics=("arbitrary",), vmem_limit_bytes=VMEM_LIMIT),
        name="inproj",
    )(x2, g1, w, qkg, cos_t, sin_t, gm)


def _attn_kernel(qt_ref, k_ref, vt_ref, lam_ref, sg_ref, o_ref, m_sc, l_sc, acc_sc,
                 *, blk, lambda_init):
    qi = pl.program_id(2)
    qt = qt_ref[...]
    comp = lax.broadcasted_iota(jnp.int32, qt.shape, 0) // DA_QK
    qts = tuple(jnp.where(comp == c, qt, jnp.zeros_like(qt)) for c in range(2))
    m_sc[...] = jnp.full(m_sc.shape, -1e30, F32)
    l_sc[...] = jnp.zeros(l_sc.shape, F32)
    acc_sc[...] = jnp.zeros(acc_sc.shape, F32)

    def update(kb, vtb, masked):
        for c in range(2):
            s = _dot(kb, qts[c])
            if masked:
                kpos = lax.broadcasted_iota(jnp.int32, s.shape, 0)
                qpos = lax.broadcasted_iota(jnp.int32, s.shape, 1)
                s = jnp.where(kpos <= qpos, s, -jnp.inf)
            m_prev = m_sc[c]
            m_new = jnp.maximum(m_prev, jnp.max(s, axis=0, keepdims=True))
            alpha = jnp.exp2(m_prev - m_new)
            p = jnp.exp2(s - m_new)
            l_sc[c] = alpha * l_sc[c] + jnp.sum(p, axis=0, keepdims=True)
            acc_sc[c] = alpha * acc_sc[c] + _dot(vtb, p.astype(BF16))
            m_sc[c] = m_new

    def body(j, carry):
        update(k_ref[j], vt_ref[j], False)
        return carry

    lax.fori_loop(0, qi, body, 0)
    update(k_ref[qi], vt_ref[qi], True)

    lp = lam_ref[...]
    lam = (jnp.exp(jnp.sum(lp[0:1] * lp[1:2], axis=1, keepdims=True))
           - jnp.exp(jnp.sum(lp[2:3] * lp[3:4], axis=1, keepdims=True)) + lambda_init)
    ot = acc_sc[0] / l_sc[0] - lam * (acc_sc[1] / l_sc[1])
    ms = jnp.mean(ot * ot, axis=0, keepdims=True)
    o = (ot * lax.rsqrt(ms + EPS)).T * (sg_ref[...] * (1.0 - lambda_init))
    o_ref[...] = o.astype(o_ref.dtype)


def _attention(qt, kb, vtb, lam_p, sub_g, lambda_init, blk):
    bsz, _, _, seq = qt.shape
    nb = seq // blk
    kern = functools.partial(_attn_kernel, blk=blk, lambda_init=lambda_init)
    return pl.pallas_call(
        kern,
        grid=(bsz, DA_HEADS, nb),
        in_specs=[
            pl.BlockSpec((None, None, DA_V, blk), lambda b, h, i: (b, h, 0, i)),
            pl.BlockSpec((None, None, nb, blk, DA_V), lambda b, h, i: (b, h, 0, 0, 0)),
            pl.BlockSpec((None, None, nb, DA_V, blk), lambda b, h, i: (b, h, 0, 0, 0)),
            pl.BlockSpec((4, DA_QK), lambda b, h, i: (0, 0)),
            pl.BlockSpec((1, DA_V), lambda b, h, i: (0, 0)),
        ],
        out_specs=pl.BlockSpec((None, blk, DA_V), lambda b, h, i: (b, i, h)),
        out_shape=jax.ShapeDtypeStruct((bsz, seq, DA_WIDTH), BF16),
        scratch_shapes=[
            pltpu.VMEM((2, 1, blk), F32),
            pltpu.VMEM((2, 1, blk), F32),
            pltpu.VMEM((2, DA_V, blk), F32),
        ],
        compiler_params=pltpu.CompilerParams(
            dimension_semantics=("arbitrary", "arbitrary", "arbitrary"),
            vmem_limit_bytes=VMEM_LIMIT),
        name="diff_attn",
    )(qt, kb, vtb, lam_p, sub_g)


def _s5_kernel(u_ref, toep_ref, bre_ref, bim_ref, cre_ref, cim_ref, lam_ref, y_ref,
               sre, sim, *, nb, nc):
    u = u_ref[...]
    sre[...] = _dot(u, bre_ref[...])
    sim[...] = _dot(u, bim_ref[...])
    lr = lam_ref[0:1, :]
    li = lam_ref[1:2, :]

    def body(c, carry):
        new = []
        for b in range(nb):
            hr, hi = carry[2 * b], carry[2 * b + 1]
            r = b * nc + c
            s_r = sre[pl.ds(r, 1), :]
            s_i = sim[pl.ds(r, 1), :]
            sre[pl.ds(r, 1), :] = hr
            sim[pl.ds(r, 1), :] = hi
            new.append(lr * hr - li * hi + s_r)
            new.append(lr * hi + li * hr + s_i)
        return tuple(new)

    zero = jnp.zeros((1, LANES), F32)
    lax.fori_loop(0, nc, body, (zero,) * (2 * nb))
    y = _dot(u, toep_ref[...])
    y = y + _dot(sre[...].astype(BF16), cre_ref[...])
    y = y + _dot(sim[...].astype(BF16), cim_ref[...])
    y_ref[...] = y


def _s5_scan(u1, toep, bre, bim, cre, cim, lam_l, nb, nc):
    g, r, w = u1.shape
    kern = functools.partial(_s5_kernel, nb=nb, nc=nc)
    grp = lambda i: (i, 0, 0)
    return pl.pallas_call(
        kern,
        grid=(g,),
        in_specs=[
            pl.BlockSpec((None, r, w), grp),
            pl.BlockSpec((None, w, w), grp),
            pl.BlockSpec((None, w, LANES), grp),
            pl.BlockSpec((None, w, LANES), grp),
            pl.BlockSpec((None, LANES, w), grp),
            pl.BlockSpec((None, LANES, w), grp),
            pl.BlockSpec((None, 2, LANES), grp),
        ],
        out_specs=pl.BlockSpec((None, r, w), grp),
        out_shape=jax.ShapeDtypeStruct((g, r, w), F32),
        scratch_shapes=[pltpu.VMEM((r, LANES), F32), pltpu.VMEM((r, LANES), F32)],
        compiler_params=pltpu.CompilerParams(
            dimension_semantics=("arbitrary",), vmem_limit_bytes=VMEM_LIMIT),
        name="s5_scan",
    )(u1, toep, bre, bim, cre, cim, lam_l)


def _s5_operators(lam_re, lam_im, log_dt, b_re, b_im, c_re, c_im, ls):
    hp = lax.Precision.HIGHEST
    dt = jnp.exp(log_dt)[:, None]
    zr, zi = lam_re * dt, lam_im * dt
    er = jnp.exp(zr)
    lbr, lbi = er * jnp.cos(zi), er * jnp.sin(zi)
    den = lam_re * lam_re + lam_im * lam_im
    nr, ni = lbr - 1.0, lbi
    fr = (nr * lam_re + ni * lam_im) / den
    fi = (ni * lam_re - nr * lam_im) / den
    bbr = fr[..., None] * b_re - fi[..., None] * b_im
    bbi = fr[..., None] * b_im + fi[..., None] * b_re
    tau = jnp.arange(ls + 1, dtype=F32)
    pe = jnp.exp(zr[..., None] * tau)
    pr, pi_ = pe * jnp.cos(zi[..., None] * tau), pe * jnp.sin(zi[..., None] * tau)
    cpr = c_re[:, :, :, None] * pr[:, None, :, :ls] - c_im[:, :, :, None] * pi_[:, None, :, :ls]
    cpi = c_re[:, :, :, None] * pi_[:, None, :, :ls] + c_im[:, :, :, None] * pr[:, None, :, :ls]
    kk = (jnp.einsum('ghpt,gpk->gthk', cpr, bbr, precision=hp)
          - jnp.einsum('ghpt,gpk->gthk', cpi, bbi, precision=hp))
    s_idx = jnp.arange(ls)[:, None]
    t_idx = jnp.arange(ls)[None, :]
    lag = t_idx - s_idx
    kt = kk[:, jnp.clip(lag, 0, ls - 1)]
    kt = jnp.where((lag >= 0)[None, :, :, None, None], kt, 0.0)
    g = kk.shape[0]
    w = ls * S5_GROUP
    toep = kt.transpose(0, 1, 4, 2, 3).reshape(g, w, w)
    rev_r, rev_i = pr[:, :, ls - 1::-1][:, :, :ls], pi_[:, :, ls - 1::-1][:, :, :ls]
    bre = rev_r[..., None] * bbr[:, :, None, :] - rev_i[..., None] * bbi[:, :, None, :]
    bim = rev_r[..., None] * bbi[:, :, None, :] + rev_i[..., None] * bbr[:, :, None, :]
    pad = LANES - S5_STATE
    bre = jnp.pad(bre.transpose(0, 2, 3, 1).reshape(g, w, S5_STATE), ((0, 0), (0, 0), (0, pad)))
    bim = jnp.pad(bim.transpose(0, 2, 3, 1).reshape(g, w, S5_STATE), ((0, 0), (0, 0), (0, pad)))
    qr = c_re[:, :, :, None] * pr[:, None, :, 1:] - c_im[:, :, :, None] * pi_[:, None, :, 1:]
    qi = c_re[:, :, :, None] * pi_[:, None, :, 1:] + c_im[:, :, :, None] * pr[:, None, :, 1:]
    cre = jnp.pad(qr.transpose(0, 2, 3, 1).reshape(g, S5_STATE, w), ((0, 0), (0, pad), (0, 0)))
    cim = jnp.pad((-qi).transpose(0, 2, 3, 1).reshape(g, S5_STATE, w), ((0, 0), (0, pad), (0, 0)))
    lam_l = jnp.pad(jnp.stack([pr[:, :, ls], pi_[:, :, ls]], axis=1), ((0, 0), (0, 0), (0, pad)))
    return (toep.astype(BF16), bre.astype(BF16), bim.astype(BF16),
            cre.astype(BF16), cim.astype(BF16), lam_l.astype(F32))


def _mix_kernel(rest_ref, ys5_ref, scw_ref, s5d_ref, wglu_ref, bglu_ref, mcw_ref, mcb_ref,
                dtb_ref, alog_ref, dvec_ref, ng_ref, o_ref, cbuf, mbuf, hst, ybuf, *, L):
    @pl.when(pl.program_id(1) == 0)
    def _():
        cbuf[0:HIST, :] = jnp.zeros((HIST, SC_WIDTH), F32)
        mbuf[0:HIST, :] = jnp.zeros((HIST, M2_XBC), F32)
        hst[...] = jnp.zeros(hst.shape, F32)

    u = rest_ref[:, R_SC + SC_WIDTH:R_SC + 2 * SC_WIDTH] * rest_ref[:, R_SC + 2 * SC_WIDTH:R_SC + 3 * SC_WIDTH]
    cbuf[HIST:HIST + L, :] = u
    conv = (scw_ref[2:3, :] * u + scw_ref[1:2, :] * cbuf[HIST - 1:HIST - 1 + L, :]
            + scw_ref[0:1, :] * cbuf[HIST - 2:HIST - 2 + L, :])
    cbuf[0:HIST, :] = cbuf[L:L + HIST, :]
    o_ref[:, 0:SC_WIDTH] = (rest_ref[:, R_SC:R_SC + SC_WIDTH] * conv).astype(o_ref.dtype)

    y5 = ys5_ref[...] + s5d_ref[...] * rest_ref[:, R_S5:R_S5 + S5_WIDTH]
    gl = jax.nn.gelu(y5, approximate=True)
    gate = _dot(gl.astype(BF16), wglu_ref[...]) + bglu_ref[...]
    o_ref[:, SC_WIDTH:SC_WIDTH + S5_WIDTH] = (gl * (1.0 / (1.0 + jnp.exp(-gate)))).astype(o_ref.dtype)

    xr = rest_ref[:, R_XBC:R_XBC + M2_XBC]
    mbuf[HIST:HIST + L, :] = xr
    conv = (mcw_ref[3:4, :] * xr + mcw_ref[2:3, :] * mbuf[HIST - 1:HIST - 1 + L, :]
            + mcw_ref[1:2, :] * mbuf[HIST - 2:HIST - 2 + L, :]
            + mcw_ref[0:1, :] * mbuf[HIST - 3:HIST - 3 + L, :] + mcb_ref[...])
    mbuf[0:HIST, :] = mbuf[L:L + HIST, :]
    xbc = _silu(conv)
    xs = xbc[:, 0:M2_WIDTH]
    dtr = rest_ref[:, R_DT:R_DT + LANES] + dtb_ref[...]
    dtv = jnp.maximum(dtr, 0.0) + jnp.log(1.0 + jnp.exp(-jnp.abs(dtr)))
    a = -jnp.exp(alog_ref[...]) * dtv
    r_i = lax.broadcasted_iota(jnp.int32, (L, L), 0)
    c_i = lax.broadcasted_iota(jnp.int32, (L, L), 1)
    tril = c_i <= r_i
    tri = jnp.where(tril, 1.0, 0.0).astype(BF16)
    cs = sum(_dot(tri, part) for part in _split_bf16(a, 3))
    cs_t = cs.T
    cs_last = cs[L - 1:L, :]
    ecs = jnp.exp(cs)
    dec = jnp.exp(cs_last - cs)
    ecl = jnp.exp(cs_last)
    lane = lax.broadcasted_iota(jnp.int32, (1, LANES), 1)
    for g in range(M2_GROUPS):
        bg = xbc[:, M2_WIDTH + g * M2_STATE:M2_WIDTH + (g + 1) * M2_STATE].astype(BF16)
        cg = xbc[:, M2_WIDTH + (M2_GROUPS + g) * M2_STATE:M2_WIDTH + (M2_GROUPS + g + 1) * M2_STATE].astype(BF16)
        gram = _dot_nt(cg, bg)
        hs = hst[g]
        yoff = _dot(cg, hs.astype(BF16))
        xdec = []
        for hh in range(2):
            h = 2 * g + hh
            seg = cs[:, h:h + 1] - cs_t[h:h + 1, :]
            dm = jnp.exp(jnp.where(tril, seg, -jnp.inf))
            xdt = xs[:, h * M2_HEAD_DIM:(h + 1) * M2_HEAD_DIM] * dtv[:, h:h + 1]
            yd = _dot((gram * dm).astype(BF16), xdt.astype(BF16))
            yo = yoff[:, hh * M2_HEAD_DIM:(hh + 1) * M2_HEAD_DIM] * ecs[:, h:h + 1]
            ybuf[:, h * M2_HEAD_DIM:(h + 1) * M2_HEAD_DIM] = yd + yo
            xdec.append(xdt * dec[:, h:h + 1])
        upd = _dot_tn(bg, jnp.concatenate(xdec, axis=1).astype(BF16))
        keep = jnp.where(lane < M2_HEAD_DIM, ecl[:, 2 * g:2 * g + 1], ecl[:, 2 * g + 1:2 * g + 2])
        hst[g] = hs * keep + upd
    y = ybuf[...] + dvec_ref[...] * xs
    yg = y * _silu(rest_ref[:, R_Z:R_Z + M2_WIDTH])
    ms = jnp.mean(yg * yg, axis=-1, keepdims=True)
    o_ref[:, SC_WIDTH + S5_WIDTH:] = (yg * lax.rsqrt(ms + EPS) * ng_ref[...]).astype(o_ref.dtype)


def _mixers(rest, ys5, scw, s5d, wglu, bglu, mcw, mcb, dtb, alog, dvec, ng, L):
    bsz, seq, _ = rest.shape
    kern = functools.partial(_mix_kernel, L=L)
    blk = lambda b, c: (b, c, 0)
    wout = SC_WIDTH + S5_WIDTH + M2_WIDTH
    return pl.pallas_call(
        kern,
        grid=(bsz, seq // L),
        in_specs=[
            pl.BlockSpec((None, L, D_REST), blk),
            pl.BlockSpec((None, L, S5_WIDTH), blk),
            _const_spec((3, SC_WIDTH)),
            _const_spec((1, S5_WIDTH)),
            _const_spec((S5_WIDTH, S5_WIDTH)),
            _const_spec((1, S5_WIDTH)),
            _const_spec((4, M2_XBC)),
            _const_spec((1, M2_XBC)),
            _const_spec((1, LANES)),
            _const_spec((1, LANES)),
            _const_spec((1, M2_WIDTH)),
            _const_spec((1, M2_WIDTH)),
        ],
        out_specs=pl.BlockSpec((None, L, wout), blk),
        out_shape=jax.ShapeDtypeStruct((bsz, seq, wout), BF16),
        scratch_shapes=[
            pltpu.VMEM((L + HIST, SC_WIDTH), F32),
            pltpu.VMEM((L + HIST, M2_XBC), F32),
            pltpu.VMEM((M2_GROUPS, M2_STATE, 2 * M2_HEAD_DIM), F32),
            pltpu.VMEM((L, M2_WIDTH), F32),
        ],
        compiler_params=pltpu.CompilerParams(
            dimension_semantics=("arbitrary", "arbitrary"), vmem_limit_bytes=VMEM_LIMIT),
        name="mixers",
    )(rest, ys5, scw, s5d, wglu, bglu, mcw, mcb, dtb, alog, dvec, ng)


def _ffn_kernel(x_ref, oa_ref, ob_ref, woa_ref, wob_ref, g2_ref, wg_ref, wu_ref, cw_ref, wd_ref,
                o_ref, hist, cb, *, tiles_per_seq, tf):
    tm = x_ref.shape[0]

    @pl.when(pl.program_id(0) % tiles_per_seq == 0)
    def _():
        hist[...] = jnp.zeros(hist.shape, F32)

    x1 = x_ref[...] + _dot(oa_ref[...], woa_ref[...]) + _dot(ob_ref[...], wob_ref[...])
    ms = jnp.mean(x1 * x1, axis=-1, keepdims=True)
    h2 = (x1 * lax.rsqrt(ms + EPS) * g2_ref[...]).astype(BF16)
    o_ref[...] = x1
    for c in range(D_FF // tf):
        sl = slice(c * tf, (c + 1) * tf)
        gpre = _dot(h2, wg_ref[:, sl])
        cb[0:HIST, :] = hist[:, sl]
        cb[HIST:HIST + tm, :] = gpre
        gc = (cw_ref[2:3, sl] * gpre + cw_ref[1:2, sl] * cb[HIST - 1:HIST - 1 + tm, :]
              + cw_ref[0:1, sl] * cb[HIST - 2:HIST - 2 + tm, :])
        hist[:, sl] = cb[tm:tm + HIST, :]
        act = (_silu(gc) * _dot(h2, wu_ref[:, sl])).astype(BF16)
        o_ref[...] += _dot(act, wd_ref[sl, :])


def _outproj_ffn(x2, oa, ob, woa, wob, g2, wg, wu, cw, wd, seq, tm, tf):
    t = x2.shape[0]
    wb = ob.shape[1]
    kern = functools.partial(_ffn_kernel, tiles_per_seq=seq // tm, tf=tf)
    row = lambda i: (i, 0)
    return pl.pallas_call(
        kern,
        grid=(t // tm,),
        in_specs=[
            pl.BlockSpec((tm, D_MODEL), row),
            pl.BlockSpec((tm, DA_WIDTH), row),
            pl.BlockSpec((tm, wb), row),
            _const_spec((DA_WIDTH, D_MODEL)),
            _const_spec((wb, D_MODEL)),
            _const_spec((1, D_MODEL)),
            _const_spec((D_MODEL, D_FF)),
            _const_spec((D_MODEL, D_FF)),
            _const_spec((3, D_FF)),
            _const_spec((D_FF, D_MODEL)),
        ],
        out_specs=pl.BlockSpec((tm, D_MODEL), row),
        out_shape=jax.ShapeDtypeStruct((t, D_MODEL), F32),
        scratch_shapes=[
            pltpu.VMEM((HIST, D_FF), F32),
            pltpu.VMEM((tm + HIST, tf), F32),
        ],
        compiler_params=pltpu.CompilerParams(
            dimension_semantics=("arbitrary",), vmem_limit_bytes=VMEM_LIMIT),
        name="outproj_ffn",
    )(x2, oa, ob, woa, wob, g2, wg, wu, cw, wd)


def _rope_lane_tables(seq):
    inv = 1.0 / (ROPE_THETA ** (jnp.arange(0, ROPE_DIM, 2, dtype=F32) / ROPE_DIM))
    ang = jnp.arange(seq, dtype=F32)[:, None] * inv[None, :]
    cos, sin = jnp.cos(ang), jnp.sin(ang)
    half = ROPE_DIM // 2
    ones = jnp.ones((seq, DA_QK - ROPE_DIM), F32)
    cos64 = jnp.concatenate([cos, cos, ones], axis=1)
    sin64 = jnp.concatenate([-sin, sin, 0.0 * ones], axis=1)
    del half
    return jnp.tile(cos64, (1, LANES // DA_QK)), jnp.tile(sin64, (1, LANES // DA_QK))


def kernel(x, ln1_g, w_in, qk_norm_g, da_lambda, subln_g, sc_conv_w, s5_lam_re, s5_lam_im, s5_log_dt, s5_b_re, s5_b_im, s5_c_re, s5_c_im, s5_d, s5_w_glu, s5_b_glu, m2_conv_w, m2_conv_b, m2_dt_bias, m2_a_log, m2_d, m2_norm_g, w_out, ln2_g, ffn_w_gate, ffn_w_up, ffn_conv_w, ffn_w_down):
    bsz, seq, _ = x.shape
    depth = w_in.shape[0]
    t = bsz * seq
    tm = min(512, seq)
    blk = min(512, seq)
    mix_l = min(256, seq)
    ls = S5_CHUNK
    nc = seq // ls
    assert seq % tm == 0 and seq % blk == 0 and seq % mix_l == 0 and seq % ls == 0

    cos_t, sin_t = _rope_lane_tables(seq)
    comp = lax.broadcasted_iota(jnp.int32, (LANES, LANES), 0) // DA_QK
    gm = (comp == comp.T).astype(BF16)
    x2 = x.reshape(t, D_MODEL)
    for l in range(depth):
        lambda_init = 0.8 - 0.6 * math.exp(-0.3 * l)
        w_pad = jnp.pad(w_in[l], ((0, 0), (0, D_PROJ_PAD - D_PROJ))).astype(BF16)
        qkg = jnp.tile(qk_norm_g[l], (1, LANES // DA_QK))
        q, k, v, rest = _inproj(x2, ln1_g[l][None, :], w_pad, qkg, cos_t, sin_t, gm, seq, tm)

        nblk = seq // blk
        qt = q.reshape(bsz, seq, DA_HEADS, DA_V).transpose(0, 2, 3, 1)
        kb = k.reshape(bsz, nblk, blk, DA_HEADS, DA_V).transpose(0, 3, 1, 2, 4)
        vtb = v.reshape(bsz, nblk, blk, DA_HEADS, DA_V).transpose(0, 3, 1, 4, 2)
        oa = _attention(qt, kb, vtb, da_lambda[l], subln_g[l][None, :], lambda_init, blk)

        rest3 = rest.reshape(bsz, seq, D_REST)
        ops = _s5_operators(s5_lam_re[l], s5_lam_im[l], s5_log_dt[l], s5_b_re[l], s5_b_im[l],
                            s5_c_re[l], s5_c_im[l], ls)
        u5 = rest3[:, :, R_S5:R_S5 + S5_WIDTH].astype(BF16)
        u1 = u5.reshape(bsz, nc, ls, S5_GROUPS, S5_GROUP).transpose(3, 0, 1, 2, 4)
        u1 = u1.reshape(S5_GROUPS, bsz * nc, ls * S5_GROUP)
        y1 = _s5_scan(u1, *ops, bsz, nc)
        ys5 = y1.reshape(S5_GROUPS, bsz, nc, ls, S5_GROUP).transpose(1, 2, 3, 0, 4)
        ys5 = ys5.reshape(bsz, seq, S5_WIDTH)

        pad4 = lambda a: jnp.pad(a, (0, LANES - M2_HEADS))[None, :]
        ob = _mixers(rest3, ys5, sc_conv_w[l], s5_d[l][None, :], s5_w_glu[l].astype(BF16),
                     s5_b_glu[l][None, :], m2_conv_w[l], m2_conv_b[l][None, :],
                     pad4(m2_dt_bias[l]), pad4(m2_a_log[l]),
                     jnp.repeat(m2_d[l], M2_HEAD_DIM)[None, :], m2_norm_g[l][None, :], mix_l)

        wo = w_out[l].astype(BF16)
        x2 = _outproj_ffn(x2, oa.reshape(t, DA_WIDTH), ob.reshape(t, D_MIX - DA_WIDTH),
                          wo[:DA_WIDTH], wo[DA_WIDTH:], ln2_g[l][None, :],
                          ffn_w_gate[l].astype(BF16), ffn_w_up[l].astype(BF16), ffn_conv_w[l],
                          ffn_w_down[l].astype(BF16), seq, tm, 256)
    return x2.reshape(bsz, seq, D_MODEL)
```

```python
import functools
import math

import jax
import jax.numpy as jnp
from jax import lax
from jax.experimental import pallas as pl
from jax.experimental.pallas import tpu as pltpu

F32 = jnp.float32
BF16 = jnp.bfloat16

D_MODEL = 1024
DA_HEADS = 4
DA_QK = 64
DA_V = 2 * DA_QK
DA_WIDTH = DA_HEADS * DA_V
ROPE_DIM = DA_QK // 4
ROPE_THETA = 500000.0
SC_WIDTH = 256
S5_WIDTH = 256
S5_GROUP = 16
S5_GROUPS = S5_WIDTH // S5_GROUP
S5_STATE = 64
M2_HEADS = 4
M2_HEAD_DIM = 64
M2_WIDTH = M2_HEADS * M2_HEAD_DIM
M2_GROUPS = 2
M2_STATE = 128
M2_XBC = M2_WIDTH + 2 * M2_GROUPS * M2_STATE
D_MIX = DA_WIDTH + SC_WIDTH + S5_WIDTH + M2_WIDTH
D_FF = 2816
EPS = 1e-6
LOG2E = 1.4426950408889634

LANES = 128
HIST = 8
VMEM_LIMIT = 56 * 1024 * 1024

W_Q = 0
W_K = W_Q + DA_WIDTH
W_V = W_K + DA_WIDTH
W_REST = W_V + DA_WIDTH
R_SC = 0
R_Z = R_SC + 3 * SC_WIDTH
R_XBC = R_Z + M2_WIDTH
R_DT = R_XBC + M2_XBC
D_REST = R_DT + LANES
W_S5 = W_REST + D_REST
D_PROJ_PAD = W_S5 + S5_WIDTH

S5_CHUNK = 8
S5_HALF = LANES // S5_GROUP
S5_HSTATE = S5_HALF * S5_STATE


def _dot(a, b):
    return jnp.dot(a, b, preferred_element_type=F32)


def _dot_nt(a, b):
    return lax.dot_general(a, b, (((1,), (1,)), ((), ())), preferred_element_type=F32)


def _dot_tn(a, b):
    return lax.dot_general(a, b, (((0,), (0,)), ((), ())), preferred_element_type=F32)


def _split_bf16(x, parts):
    out = []
    r = x
    for _ in range(parts):
        p = r.astype(BF16)
        out.append(p)
        r = r - p.astype(F32)
    return out


def _silu(x):
    return x * (1.0 / (1.0 + jnp.exp(-x)))


def _const_spec(shape):
    nd = len(shape)
    return pl.BlockSpec(shape, lambda *_: (0,) * nd, pipeline_mode=pl.Buffered(1))


def _inproj_kernel(x_ref, g1_ref, w_ref, qkg_ref, cos_ref, sin_ref, gm_ref,
                   qt_ref, kb_ref, vt_ref, rest_ref, u5_ref):
    tm = x_ref.shape[0]
    x = x_ref[...]
    ms = jnp.mean(x * x, axis=-1, keepdims=True)
    hn = (x * lax.rsqrt(ms + EPS) * g1_ref[...]).astype(BF16)
    cosv = cos_ref[...]
    sinv = sin_ref[...]
    gm = gm_ref[...]
    lane = lax.broadcasted_iota(jnp.int32, (tm, LANES), 1)
    pair_up = (lane % DA_QK) < (ROPE_DIM // 2)

    def qk_cols(col0, gvec, scale):
        y = _dot(hn, w_ref[:, col0:col0 + LANES])
        hi, lo = _split_bf16(y * y, 2)
        ss = _dot(hi, gm) + _dot(lo, gm)
        yn = y * lax.rsqrt(ss * (1.0 / DA_QK) + EPS) * gvec
        partner = jnp.where(pair_up,
                            pltpu.roll(yn, LANES - ROPE_DIM // 2, 1),
                            pltpu.roll(yn, ROPE_DIM // 2, 1))
        return (yn * cosv + partner * sinv) * scale

    for h in range(DA_HEADS):
        q = qk_cols(W_Q + h * LANES, qkg_ref[0:1, :], DA_QK ** -0.5 * LOG2E)
        qt_ref[h] = q.T.astype(BF16)
        kb_ref[h] = qk_cols(W_K + h * LANES, qkg_ref[1:2, :], 1.0).astype(BF16)
        v = _dot(hn, w_ref[:, W_V + h * LANES:W_V + (h + 1) * LANES])
        vt_ref[h] = v.T.astype(BF16)
    rest_ref[...] = _dot(hn, w_ref[:, W_REST:W_REST + D_REST])
    for hh in range(S5_WIDTH // LANES):
        u5_ref[hh] = _dot(hn, w_ref[:, W_S5 + hh * LANES:W_S5 + (hh + 1) * LANES])


def _inproj(x2, g1, w, qkg, cos_t, sin_t, gm, bsz, seq, tm):
    t = x2.shape[0]
    nt = seq // tm
    row = lambda i: (i, 0)
    tab = lambda i: (i % nt, 0)
    return pl.pallas_call(
        _inproj_kernel,
        grid=(t // tm,),
        in_specs=[
            pl.BlockSpec((tm, D_MODEL), row),
            _const_spec((1, D_MODEL)),
            _const_spec((D_MODEL, D_PROJ_PAD)),
            _const_spec((2, LANES)),
            pl.BlockSpec((tm, LANES), tab),
            pl.BlockSpec((tm, LANES), tab),
            _const_spec((LANES, LANES)),
        ],
        out_specs=[
            pl.BlockSpec((None, DA_HEADS, DA_V, tm), lambda i: (i // nt, 0, 0, i % nt)),
            pl.BlockSpec((None, DA_HEADS, None, tm, DA_V), lambda i: (i // nt, 0, i % nt, 0, 0)),
            pl.BlockSpec((None, DA_HEADS, None, DA_V, tm), lambda i: (i // nt, 0, i % nt, 0, 0)),
            pl.BlockSpec((tm, D_REST), row),
            pl.BlockSpec((S5_WIDTH // LANES, tm, LANES), lambda i: (0, i, 0)),
        ],
        out_shape=[
            jax.ShapeDtypeStruct((bsz, DA_HEADS, DA_V, seq), BF16),
            jax.ShapeDtypeStruct((bsz, DA_HEADS, nt, tm, DA_V), BF16),
            jax.ShapeDtypeStruct((bsz, DA_HEADS, nt, DA_V, tm), BF16),
            jax.ShapeDtypeStruct((t, D_REST), F32),
            jax.ShapeDtypeStruct((S5_WIDTH // LANES, t, LANES), F32),
        ],
        compiler_params=pltpu.CompilerParams(
            dimension_semantics=("arbitrary",), vmem_limit_bytes=VMEM_LIMIT),
        name="inproj",
    )(x2, g1, w, qkg, cos_t, sin_t, gm)


def _attn_kernel(qt_ref, k_ref, vt_ref, lam_ref, sg_ref, o_ref, qz_sc, s_sc, m_sc, l_sc, acc_sc,
                 *, lambda_init):
    qi = pl.program_id(2)
    qt = qt_ref[...]
    comp = lax.broadcasted_iota(jnp.int32, qt.shape, 0) // DA_QK
    for c in range(2):
        qz_sc[c] = jnp.where(comp == c, qt, jnp.zeros_like(qt))
    m_sc[...] = jnp.full(m_sc.shape, -1e30, F32)
    l_sc[...] = jnp.zeros(l_sc.shape, F32)
    acc_sc[...] = jnp.zeros(acc_sc.shape, F32)

    def scores(j, buf):
        kb = k_ref[j]
        for c in range(2):
            s_sc[buf, c] = _dot(kb, qz_sc[c])

    def absorb(j, buf, masked):
        vtb = vt_ref[j]
        for c in range(2):
            s = s_sc[buf, c]
            if masked:
                kpos = lax.broadcasted_iota(jnp.int32, s.shape, 0)
                qpos = lax.broadcasted_iota(jnp.int32, s.shape, 1)
                s = jnp.where(kpos <= qpos, s, -jnp.inf)
            m_prev = m_sc[c]
            m_new = jnp.maximum(m_prev, jnp.max(s, axis=0, keepdims=True))
            alpha = jnp.exp2(m_prev - m_new)
            p = jnp.exp2(s - m_new)
            l_sc[c] = alpha * l_sc[c] + jnp.sum(p, axis=0, keepdims=True)
            acc_sc[c] = alpha * acc_sc[c] + _dot(vtb, p.astype(BF16))
            m_sc[c] = m_new

    scores(0, 0)
    pairs = qi // 2

    def body(t, carry):
        j = 2 * t
        scores(j + 1, 1)
        absorb(j, 0, False)
        scores(j + 2, 0)
        absorb(j + 1, 1, False)
        return carry

    lax.fori_loop(0, pairs, body, 0)
    j0 = 2 * pairs

    @pl.when(j0 == qi)
    def _():
        absorb(qi, 0, True)

    @pl.when(j0 != qi)
    def _():
        scores(qi, 1)
        absorb(j0, 0, False)
        absorb(qi, 1, True)

    lp = lam_ref[...]
    lam = (jnp.exp(jnp.sum(lp[0:1] * lp[1:2], axis=1, keepdims=True))
           - jnp.exp(jnp.sum(lp[2:3] * lp[3:4], axis=1, keepdims=True)) + lambda_init)
    ot = acc_sc[0] / l_sc[0] - lam * (acc_sc[1] / l_sc[1])
    ms = jnp.mean(ot * ot, axis=0, keepdims=True)
    o = (ot * lax.rsqrt(ms + EPS)).T * (sg_ref[...] * (1.0 - lambda_init))
    o_ref[...] = o.astype(o_ref.dtype)


def _attention(qt, kb, vtb, lam_p, sub_g, lambda_init):
    bsz, _, _, seq = qt.shape
    nb, blk = kb.shape[2], kb.shape[3]
    kern = functools.partial(_attn_kernel, lambda_init=lambda_init)
    return pl.pallas_call(
        kern,
        grid=(bsz, DA_HEADS, nb),
        in_specs=[
            pl.BlockSpec((None, None, DA_V, blk), lambda b, h, i: (b, h, 0, i)),
            pl.BlockSpec((None, None, nb, blk, DA_V), lambda b, h, i: (b, h, 0, 0, 0)),
            pl.BlockSpec((None, None, nb, DA_V, blk), lambda b, h, i: (b, h, 0, 0, 0)),
            pl.BlockSpec((4, DA_QK), lambda b, h, i: (0, 0)),
            pl.BlockSpec((1, DA_V), lambda b, h, i: (0, 0)),
        ],
        out_specs=pl.BlockSpec((None, blk, DA_V), lambda b, h, i: (b, i, h)),
        out_shape=jax.ShapeDtypeStruct((bsz, seq, DA_WIDTH), BF16),
        scratch_shapes=[
            pltpu.VMEM((2, DA_V, blk), BF16),
            pltpu.VMEM((2, 2, blk, blk), F32),
            pltpu.VMEM((2, 1, blk), F32),
            pltpu.VMEM((2, 1, blk), F32),
            pltpu.VMEM((2, DA_V, blk), F32),
        ],
        compiler_params=pltpu.CompilerParams(
            dimension_semantics=("arbitrary", "arbitrary", "arbitrary"),
            vmem_limit_bytes=VMEM_LIMIT),
        name="diff_attn",
    )(qt, kb, vtb, lam_p, sub_g)


def _s5_kernel(u_ref, toep_ref, bpow_ref, cpow_ref, lam_ref, y_ref, st, hc, *, ls, tiles_per_seq):
    tr = u_ref.shape[1] // ls
    hs = S5_HSTATE

    @pl.when(pl.program_id(0) % tiles_per_seq == 0)
    def _():
        hc[...] = jnp.zeros(hc.shape, F32)

    def half_input(hh):
        cols = [u_ref[hh, pl.ds(i, tr, stride=ls), :] for i in range(ls)]
        return jnp.concatenate(cols, axis=1).astype(BF16)

    for hh in range(2):
        st[hh] = _dot(half_input(hh), bpow_ref[hh])

    lam = [(lam_ref[hh, 0:1, :], lam_ref[hh, 1:2, :]) for hh in range(2)]

    def body(c, carry):
        new = []
        for hh in range(2):
            hr, hi = carry[2 * hh], carry[2 * hh + 1]
            lr, li = lam[hh]
            s_r = st[hh, pl.ds(c, 1), 0:hs]
            s_i = st[hh, pl.ds(c, 1), hs:2 * hs]
            st[hh, pl.ds(c, 1), 0:hs] = hr
            st[hh, pl.ds(c, 1), hs:2 * hs] = hi
            new.append(lr * hr - li * hi + s_r)
            new.append(lr * hi + li * hr + s_i)
        return tuple(new)

    init = tuple(hc[hh, :, ri * hs:(ri + 1) * hs] for hh in range(2) for ri in range(2))
    fin = lax.fori_loop(0, tr, body, init)
    for hh in range(2):
        for ri in range(2):
            hc[hh, :, ri * hs:(ri + 1) * hs] = fin[2 * hh + ri]

    for hh in range(2):
        y = _dot(half_input(hh), toep_ref[hh]) + _dot(st[hh].astype(BF16), cpow_ref[hh])
        for i in range(ls):
            y_ref[hh, pl.ds(i, tr, stride=ls), :] = y[:, i * LANES:(i + 1) * LANES]


def _s5_scan(u5, toep, bpow, cpow, lam_l, ls, nc, tr):
    nh, t, w = u5.shape
    rows = tr * ls
    kern = functools.partial(_s5_kernel, ls=ls, tiles_per_seq=nc // tr)
    row = lambda i: (0, i, 0)
    return pl.pallas_call(
        kern,
        grid=(t // rows,),
        in_specs=[
            pl.BlockSpec((nh, rows, w), row),
            _const_spec(toep.shape),
            _const_spec(bpow.shape),
            _const_spec(cpow.shape),
            _const_spec(lam_l.shape),
        ],
        out_specs=pl.BlockSpec((nh, rows, w), row),
        out_shape=jax.ShapeDtypeStruct((nh, t, w), F32),
        scratch_shapes=[pltpu.VMEM((2, tr, 2 * S5_HSTATE), F32), pltpu.VMEM((2, 1, 2 * S5_HSTATE), F32)],
        compiler_params=pltpu.CompilerParams(
            dimension_semantics=("arbitrary",), vmem_limit_bytes=VMEM_LIMIT),
        name="s5_scan",
    )(u5, toep, bpow, cpow, lam_l)


def _s5_operators(lam_re, lam_im, log_dt, b_re, b_im, c_re, c_im, ls):
    hp = lax.Precision.HIGHEST
    dt = jnp.exp(log_dt)[:, None]
    zr, zi = lam_re * dt, lam_im * dt
    er = jnp.exp(zr)
    lbr, lbi = er * jnp.cos(zi), er * jnp.sin(zi)
    den = lam_re * lam_re + lam_im * lam_im
    nr, ni = lbr - 1.0, lbi
    fr = (nr * lam_re + ni * lam_im) / den
    fi = (ni * lam_re - nr * lam_im) / den
    bbr = fr[..., None] * b_re - fi[..., None] * b_im
    bbi = fr[..., None] * b_im + fi[..., None] * b_re
    tau = jnp.arange(ls + 1, dtype=F32)
    pe = jnp.exp(zr[..., None] * tau)
    pr, pi_ = pe * jnp.cos(zi[..., None] * tau), pe * jnp.sin(zi[..., None] * tau)
    cpr = c_re[:, :, :, None] * pr[:, None, :, :ls] - c_im[:, :, :, None] * pi_[:, None, :, :ls]
    cpi = c_re[:, :, :, None] * pi_[:, None, :, :ls] + c_im[:, :, :, None] * pr[:, None, :, :ls]
    kk = (jnp.einsum('ghpt,gpk->gthk', cpr, bbr, precision=hp)
          - jnp.einsum('ghpt,gpk->gthk', cpi, bbi, precision=hp))
    lag = jnp.arange(ls)[None, :] - jnp.arange(ls)[:, None]
    kt = kk[:, jnp.clip(lag, 0, ls - 1)]
    kt = jnp.where((lag >= 0)[None, :, :, None, None], kt, 0.0)
    nh, gh = S5_GROUPS // S5_HALF, S5_HALF
    eye = jnp.eye(gh, dtype=F32)
    w = ls * LANES
    kt = kt.reshape(nh, gh, ls, ls, S5_GROUP, S5_GROUP)
    toep = kt[..., None] * eye[None, :, None, None, None, None, :]
    toep = toep.transpose(0, 2, 1, 5, 3, 6, 4).reshape(nh, w, w)
    tau_rev = (ls - 1) - jnp.arange(ls, dtype=F32)
    pe_rev = jnp.exp(zr[..., None] * tau_rev)
    rev_r, rev_i = pe_rev * jnp.cos(zi[..., None] * tau_rev), pe_rev * jnp.sin(zi[..., None] * tau_rev)
    bre = rev_r[..., None] * bbr[:, :, None, :] - rev_i[..., None] * bbi[:, :, None, :]
    bim = rev_r[..., None] * bbi[:, :, None, :] + rev_i[..., None] * bbr[:, :, None, :]
    bb = jnp.stack([bre, bim], axis=0).reshape(2, nh, gh, S5_STATE, ls, S5_GROUP)
    bpow = bb[..., None] * eye[None, None, :, None, None, None, :]
    bpow = bpow.transpose(1, 4, 2, 5, 0, 6, 3).reshape(nh, w, 2 * S5_HSTATE)
    qr = c_re[:, :, :, None] * pr[:, None, :, 1:] - c_im[:, :, :, None] * pi_[:, None, :, 1:]
    qi = c_re[:, :, :, None] * pi_[:, None, :, 1:] + c_im[:, :, :, None] * pr[:, None, :, 1:]
    cc = jnp.stack([qr, -qi], axis=0).reshape(2, nh, gh, S5_GROUP, S5_STATE, ls)
    cpow = cc[..., None] * eye[None, None, :, None, None, None, :]
    cpow = cpow.transpose(1, 0, 2, 4, 5, 6, 3).reshape(nh, 2 * S5_HSTATE, w)
    lam_l = jnp.stack([pr[:, :, ls].reshape(nh, S5_HSTATE), pi_[:, :, ls].reshape(nh, S5_HSTATE)], axis=1)
    return toep.astype(BF16), bpow.astype(BF16), cpow.astype(BF16), lam_l.astype(F32)


def _mix_kernel(rest_ref, u5_ref, ys5_ref, scw_ref, s5d_ref, wglu_ref, bglu_ref, mcw_ref, mcb_ref,
                dtb_ref, alog_ref, dvec_ref, ng_ref, o_ref, cbuf, mbuf, hst, ybuf, *, L):
    @pl.when(pl.program_id(1) == 0)
    def _():
        cbuf[0:HIST, :] = jnp.zeros((HIST, SC_WIDTH), F32)
        mbuf[0:HIST, :] = jnp.zeros((HIST, M2_XBC), F32)
        hst[...] = jnp.zeros(hst.shape, F32)

    u = rest_ref[:, R_SC + SC_WIDTH:R_SC + 2 * SC_WIDTH] * rest_ref[:, R_SC + 2 * SC_WIDTH:R_SC + 3 * SC_WIDTH]
    cbuf[HIST:HIST + L, :] = u
    conv = (scw_ref[2:3, :] * u + scw_ref[1:2, :] * cbuf[HIST - 1:HIST - 1 + L, :]
            + scw_ref[0:1, :] * cbuf[HIST - 2:HIST - 2 + L, :])
    cbuf[0:HIST, :] = cbuf[L:L + HIST, :]
    o_ref[:, 0:SC_WIDTH] = (rest_ref[:, R_SC:R_SC + SC_WIDTH] * conv).astype(o_ref.dtype)

    halves = lambda ref: jnp.concatenate([ref[hh] for hh in range(S5_WIDTH // LANES)], axis=1)
    y5 = halves(ys5_ref) + s5d_ref[...] * halves(u5_ref)
    gl = jax.nn.gelu(y5, approximate=True)
    gate = _dot(gl.astype(BF16), wglu_ref[...]) + bglu_ref[...]
    o_ref[:, SC_WIDTH:SC_WIDTH + S5_WIDTH] = (gl * (1.0 / (1.0 + jnp.exp(-gate)))).astype(o_ref.dtype)

    xr = rest_ref[:, R_XBC:R_XBC + M2_XBC]
    mbuf[HIST:HIST + L, :] = xr
    conv = (mcw_ref[3:4, :] * xr + mcw_ref[2:3, :] * mbuf[HIST - 1:HIST - 1 + L, :]
            + mcw_ref[1:2, :] * mbuf[HIST - 2:HIST - 2 + L, :]
            + mcw_ref[0:1, :] * mbuf[HIST - 3:HIST - 3 + L, :] + mcb_ref[...])
    mbuf[0:HIST, :] = mbuf[L:L + HIST, :]
    xbc = _silu(conv)
    xs = xbc[:, 0:M2_WIDTH]
    dtr = rest_ref[:, R_DT:R_DT + LANES] + dtb_ref[...]
    dtv = jnp.maximum(dtr, 0.0) + jnp.log(1.0 + jnp.exp(-jnp.abs(dtr)))
    a = -jnp.exp(alog_ref[...]) * dtv
    r_i = lax.broadcasted_iota(jnp.int32, (L, L), 0)
    c_i = lax.broadcasted_iota(jnp.int32, (L, L), 1)
    tril = c_i <= r_i
    tri = jnp.where(tril, 1.0, 0.0).astype(BF16)
    cs = sum(_dot(tri, part) for part in _split_bf16(a, 3))
    cs_t = cs.T
    cs_last = cs[L - 1:L, :]
    ecs = jnp.exp(cs)
    dec = jnp.exp(cs_last - cs)
    ecl = jnp.exp(cs_last)
    lane = lax.broadcasted_iota(jnp.int32, (1, LANES), 1)
    for g in range(M2_GROUPS):
        bg = xbc[:, M2_WIDTH + g * M2_STATE:M2_WIDTH + (g + 1) * M2_STATE].astype(BF16)
        cg = xbc[:, M2_WIDTH + (M2_GROUPS + g) * M2_STATE:M2_WIDTH + (M2_GROUPS + g + 1) * M2_STATE].astype(BF16)
        gram = _dot_nt(cg, bg)
        hs = hst[g]
        yoff = _dot(cg, hs.astype(BF16))
        xdec = []
        for hh in range(2):
            h = 2 * g + hh
            seg = cs[:, h:h + 1] - cs_t[h:h + 1, :]
            dm = jnp.exp(jnp.where(tril, seg, -jnp.inf))
            xdt = xs[:, h * M2_HEAD_DIM:(h + 1) * M2_HEAD_DIM] * dtv[:, h:h + 1]
            yd = _dot((gram * dm).astype(BF16), xdt.astype(BF16))
            yo = yoff[:, hh * M2_HEAD_DIM:(hh + 1) * M2_HEAD_DIM] * ecs[:, h:h + 1]
            ybuf[:, h * M2_HEAD_DIM:(h + 1) * M2_HEAD_DIM] = yd + yo
            xdec.append(xdt * dec[:, h:h + 1])
        upd = _dot_tn(bg, jnp.concatenate(xdec, axis=1).astype(BF16))
        keep = jnp.where(lane < M2_HEAD_DIM, ecl[:, 2 * g:2 * g + 1], ecl[:, 2 * g + 1:2 * g + 2])
        hst[g] = hs * keep + upd
    y = ybuf[...] + dvec_ref[...] * xs
    yg = y * _silu(rest_ref[:, R_Z:R_Z + M2_WIDTH])
    ms = jnp.mean(yg * yg, axis=-1, keepdims=True)
    o_ref[:, SC_WIDTH + S5_WIDTH:] = (yg * lax.rsqrt(ms + EPS) * ng_ref[...]).astype(o_ref.dtype)


def _mixers(rest, u5, ys5, scw, s5d, wglu, bglu, mcw, mcb, dtb, alog, dvec, ng, L):
    bsz, seq, _ = rest.shape
    kern = functools.partial(_mix_kernel, L=L)
    blk = lambda b, c: (b, c, 0)
    wout = SC_WIDTH + S5_WIDTH + M2_WIDTH
    return pl.pallas_call(
        kern,
        grid=(bsz, seq // L),
        in_specs=[
            pl.BlockSpec((None, L, D_REST), blk),
            pl.BlockSpec((S5_WIDTH // LANES, L, LANES), lambda b, c: (0, b * (seq // L) + c, 0)),
            pl.BlockSpec((S5_WIDTH // LANES, L, LANES), lambda b, c: (0, b * (seq // L) + c, 0)),
            _const_spec((3, SC_WIDTH)),
            _const_spec((1, S5_WIDTH)),
            _const_spec((S5_WIDTH, S5_WIDTH)),
            _const_spec((1, S5_WIDTH)),
            _const_spec((4, M2_XBC)),
            _const_spec((1, M2_XBC)),
            _const_spec((1, LANES)),
            _const_spec((1, LANES)),
            _const_spec((1, M2_WIDTH)),
            _const_spec((1, M2_WIDTH)),
        ],
        out_specs=pl.BlockSpec((None, L, wout), blk),
        out_shape=jax.ShapeDtypeStruct((bsz, seq, wout), BF16),
        scratch_shapes=[
            pltpu.VMEM((L + HIST, SC_WIDTH), F32),
            pltpu.VMEM((L + HIST, M2_XBC), F32),
            pltpu.VMEM((M2_GROUPS, M2_STATE, 2 * M2_HEAD_DIM), F32),
            pltpu.VMEM((L, M2_WIDTH), F32),
        ],
        compiler_params=pltpu.CompilerParams(
            dimension_semantics=("arbitrary", "arbitrary"), vmem_limit_bytes=VMEM_LIMIT),
        name="mixers",
    )(rest, u5, ys5, scw, s5d, wglu, bglu, mcw, mcb, dtb, alog, dvec, ng)


def _ffn_kernel(x_ref, oa_ref, ob_ref, woa_ref, wob_ref, g2_ref, wg_ref, wu_ref, cw_ref, wd_ref,
                o_ref, hist, cb, *, tiles_per_seq, tf):
    tm = x_ref.shape[0]

    @pl.when(pl.program_id(0) % tiles_per_seq == 0)
    def _():
        hist[...] = jnp.zeros(hist.shape, F32)

    x1 = x_ref[...] + _dot(oa_ref[...], woa_ref[...]) + _dot(ob_ref[...], wob_ref[...])
    ms = jnp.mean(x1 * x1, axis=-1, keepdims=True)
    h2 = (x1 * lax.rsqrt(ms + EPS) * g2_ref[...]).astype(BF16)
    o_ref[...] = x1
    for c in range(D_FF // tf):
        sl = slice(c * tf, (c + 1) * tf)
        gpre = _dot(h2, wg_ref[:, sl])
        cb[0:HIST, :] = hist[:, sl]
        cb[HIST:HIST + tm, :] = gpre
        gc = (cw_ref[2:3, sl] * gpre + cw_ref[1:2, sl] * cb[HIST - 1:HIST - 1 + tm, :]
              + cw_ref[0:1, sl] * cb[HIST - 2:HIST - 2 + tm, :])
        hist[:, sl] = cb[tm:tm + HIST, :]
        act = (_silu(gc) * _dot(h2, wu_ref[:, sl])).astype(BF16)
        o_ref[...] += _dot(act, wd_ref[sl, :])


def _outproj_ffn(x2, oa, ob, woa, wob, g2, wg, wu, cw, wd, seq, tm, tf):
    t = x2.shape[0]
    wb = ob.shape[1]
    kern = functools.partial(_ffn_kernel, tiles_per_seq=seq // tm, tf=tf)
    row = lambda i: (i, 0)
    return pl.pallas_call(
        kern,
        grid=(t // tm,),
        in_specs=[
            pl.BlockSpec((tm, D_MODEL), row),
            pl.BlockSpec((tm, DA_WIDTH), row),
            pl.BlockSpec((tm, wb), row),
            _const_spec((DA_WIDTH, D_MODEL)),
            _const_spec((wb, D_MODEL)),
            _const_spec((1, D_MODEL)),
            _const_spec((D_MODEL, D_FF)),
            _const_spec((D_MODEL, D_FF)),
            _const_spec((3, D_FF)),
            _const_spec((D_FF, D_MODEL)),
        ],
        out_specs=pl.BlockSpec((tm, D_MODEL), row),
        out_shape=jax.ShapeDtypeStruct((t, D_MODEL), F32),
        scratch_shapes=[
            pltpu.VMEM((HIST, D_FF), F32),
            pltpu.VMEM((tm + HIST, tf), F32),
        ],
        compiler_params=pltpu.CompilerParams(
            dimension_semantics=("arbitrary",), vmem_limit_bytes=VMEM_LIMIT),
        name="outproj_ffn",
    )(x2, oa, ob, woa, wob, g2, wg, wu, cw, wd)


def _rope_lane_tables(seq):
    inv = 1.0 / (ROPE_THETA ** (jnp.arange(0, ROPE_DIM, 2, dtype=F32) / ROPE_DIM))
    ang = jnp.arange(seq, dtype=F32)[:, None] * inv[None, :]
    cos, sin = jnp.cos(ang), jnp.sin(ang)
    ones = jnp.ones((seq, DA_QK - ROPE_DIM), F32)
    cos64 = jnp.concatenate([cos, cos, ones], axis=1)
    sin64 = jnp.concatenate([-sin, sin, 0.0 * ones], axis=1)
    return jnp.tile(cos64, (1, LANES // DA_QK)), jnp.tile(sin64, (1, LANES // DA_QK))


def _forward(x, p, blk, mix_l, s5_tr):
    bsz, seq, _ = x.shape
    depth = p["w_in"].shape[0]
    t = bsz * seq
    ls = S5_CHUNK
    nc = seq // ls
    assert seq % blk == 0 and seq % mix_l == 0 and nc % s5_tr == 0

    cos_t, sin_t = _rope_lane_tables(seq)
    comp = lax.broadcasted_iota(jnp.int32, (LANES, LANES), 0) // DA_QK
    gm = (comp == comp.T).astype(BF16)
    off_s5 = 3 * DA_WIDTH + 3 * SC_WIDTH
    x2 = x.reshape(t, D_MODEL)
    for l in range(depth):
        lambda_init = 0.8 - 0.6 * math.exp(-0.3 * l)
        w = p["w_in"][l]
        w_r = jnp.concatenate([w[:, :off_s5], w[:, off_s5 + S5_WIDTH:],
                               jnp.zeros((D_MODEL, LANES - M2_HEADS), F32),
                               w[:, off_s5:off_s5 + S5_WIDTH]], axis=1).astype(BF16)
        qkg = jnp.tile(p["qk_norm_g"][l], (1, LANES // DA_QK))
        qt, kb, vtb, rest, u5 = _inproj(x2, p["ln1_g"][l][None, :], w_r, qkg, cos_t, sin_t, gm,
                                        bsz, seq, blk)
        oa = _attention(qt, kb, vtb, p["da_lambda"][l], p["subln_g"][l][None, :], lambda_init)

        ops = _s5_operators(p["s5_lam_re"][l], p["s5_lam_im"][l], p["s5_log_dt"][l], p["s5_b_re"][l],
                            p["s5_b_im"][l], p["s5_c_re"][l], p["s5_c_im"][l], ls)
        ys5 = _s5_scan(u5, *ops, ls, nc, s5_tr)

        pad4 = lambda a: jnp.pad(a, (0, LANES - M2_HEADS))[None, :]
        ob = _mixers(rest.reshape(bsz, seq, D_REST), u5, ys5, p["sc_conv_w"][l], p["s5_d"][l][None, :],
                     p["s5_w_glu"][l].astype(BF16), p["s5_b_glu"][l][None, :], p["m2_conv_w"][l],
                     p["m2_conv_b"][l][None, :], pad4(p["m2_dt_bias"][l]), pad4(p["m2_a_log"][l]),
                     jnp.repeat(p["m2_d"][l], M2_HEAD_DIM)[None, :], p["m2_norm_g"][l][None, :], mix_l)

        wo = p["w_out"][l].astype(BF16)
        x2 = _outproj_ffn(x2, oa.reshape(t, DA_WIDTH), ob.reshape(t, D_MIX - DA_WIDTH),
                          wo[:DA_WIDTH], wo[DA_WIDTH:], p["ln2_g"][l][None, :],
                          p["ffn_w_gate"][l].astype(BF16), p["ffn_w_up"][l].astype(BF16),
                          p["ffn_conv_w"][l], p["ffn_w_down"][l].astype(BF16), seq, blk, 256)
    return x2.reshape(bsz, seq, D_MODEL)


def kernel(x, ln1_g, w_in, qk_norm_g, da_lambda, subln_g, sc_conv_w, s5_lam_re, s5_lam_im, s5_log_dt, s5_b_re, s5_b_im, s5_c_re, s5_c_im, s5_d, s5_w_glu, s5_b_glu, m2_conv_w, m2_conv_b, m2_dt_bias, m2_a_log, m2_d, m2_norm_g, w_out, ln2_g, ffn_w_gate, ffn_w_up, ffn_conv_w, ffn_w_down):
    params = dict(ln1_g=ln1_g, w_in=w_in, qk_norm_g=qk_norm_g, da_lambda=da_lambda, subln_g=subln_g,
                  sc_conv_w=sc_conv_w, s5_lam_re=s5_lam_re, s5_lam_im=s5_lam_im, s5_log_dt=s5_log_dt,
                  s5_b_re=s5_b_re, s5_b_im=s5_b_im, s5_c_re=s5_c_re, s5_c_im=s5_c_im, s5_d=s5_d,
                  s5_w_glu=s5_w_glu, s5_b_glu=s5_b_glu, m2_conv_w=m2_conv_w, m2_conv_b=m2_conv_b,
                  m2_dt_bias=m2_dt_bias, m2_a_log=m2_a_log, m2_d=m2_d, m2_norm_g=m2_norm_g,
                  w_out=w_out, ln2_g=ln2_g, ffn_w_gate=ffn_w_gate, ffn_w_up=ffn_w_up,
                  ffn_conv_w=ffn_conv_w, ffn_w_down=ffn_w_down)
    seq = x.shape[1]
    return _forward(x, params, blk=min(512, seq), mix_l=min(256, seq),
                    s5_tr=min(256, seq // S5_CHUNK))
```

```python
import functools
import math

import jax
import jax.numpy as jnp
from jax import lax
from jax.experimental import pallas as pl
from jax.experimental.pallas import tpu as pltpu

F32 = jnp.float32
BF16 = jnp.bfloat16

D_MODEL = 1024
DA_HEADS = 4
DA_QK = 64
DA_V = 2 * DA_QK
DA_WIDTH = DA_HEADS * DA_V
ROPE_DIM = DA_QK // 4
ROPE_THETA = 500000.0
SC_WIDTH = 256
S5_WIDTH = 256
S5_GROUP = 16
S5_GROUPS = S5_WIDTH // S5_GROUP
S5_STATE = 64
M2_HEADS = 4
M2_HEAD_DIM = 64
M2_WIDTH = M2_HEADS * M2_HEAD_DIM
M2_GROUPS = 2
M2_STATE = 128
M2_XBC = M2_WIDTH + 2 * M2_GROUPS * M2_STATE
D_MIX = DA_WIDTH + SC_WIDTH + S5_WIDTH + M2_WIDTH
D_FF = 2816
EPS = 1e-6
LOG2E = 1.4426950408889634

LANES = 128
HIST = 8
VMEM_LIMIT = 56 * 1024 * 1024

W_Q = 0
W_K = W_Q + DA_WIDTH
W_V = W_K + DA_WIDTH
W_SC = W_V + DA_WIDTH
W_S5 = W_SC + 3 * SC_WIDTH
W_Z = W_S5 + S5_WIDTH
W_DT = W_Z + M2_WIDTH + M2_XBC
D_PROJ = W_DT + M2_HEADS
R_SC = 0
R_Z = R_SC + 3 * SC_WIDTH
R_XBC = R_Z + M2_WIDTH
R_DT = R_XBC + M2_XBC
D_REST = R_DT + LANES
QK_COLS = 2 * LANES

S5_CHUNK = 8
S5_HALF = LANES // S5_GROUP
S5_HSTATE = S5_HALF * S5_STATE


def _dot(a, b):
    return jnp.dot(a, b, preferred_element_type=F32)


def _dot_nt(a, b):
    return lax.dot_general(a, b, (((1,), (1,)), ((), ())), preferred_element_type=F32)


def _dot_tn(a, b):
    return lax.dot_general(a, b, (((0,), (0,)), ((), ())), preferred_element_type=F32)


def _split_bf16(x, parts):
    out = []
    r = x
    for _ in range(parts):
        p = r.astype(BF16)
        out.append(p)
        r = r - p.astype(F32)
    return out


def _silu(x):
    return x * (1.0 / (1.0 + jnp.exp(-x)))


def _const_spec(shape):
    nd = len(shape)
    return pl.BlockSpec(shape, lambda *_: (0,) * nd, pipeline_mode=pl.Buffered(1))


def _layer_spec(shape, layer):
    nd = len(shape)
    return pl.BlockSpec((None,) + tuple(shape[1:]), lambda *_: (layer,) + (0,) * (nd - 1),
                        pipeline_mode=pl.Buffered(1))


def _inproj_kernel(x_ref, g1_ref, w_ref, wdt_ref, qkg_ref, cos_ref, sin_ref, gm_ref,
                   qt_ref, kb_ref, vt_ref, rest_ref, u5_ref):
    tm = x_ref.shape[0]
    x = x_ref[...]
    ms = jnp.mean(x * x, axis=-1, keepdims=True)
    hn = (x * lax.rsqrt(ms + EPS) * g1_ref[...]).astype(BF16)
    cosv = cos_ref[...]
    sinv = sin_ref[...]
    gm = gm_ref[...]
    lane = lax.broadcasted_iota(jnp.int32, (tm, LANES), 1)
    pair_up = (lane % DA_QK) < (ROPE_DIM // 2)
    heads_per_dot = QK_COLS // DA_V

    def qk_heads(col0, gvec, scale):
        y = _dot(hn, w_ref[:, col0:col0 + QK_COLS])
        hi, lo = _split_bf16(y * y, 2)
        ss = _dot(hi, gm) + _dot(lo, gm)
        yn = y * lax.rsqrt(ss * (1.0 / DA_QK) + EPS)
        out = []
        for j in range(heads_per_dot):
            yh = yn[:, j * DA_V:(j + 1) * DA_V] * gvec
            partner = jnp.where(pair_up,
                                pltpu.roll(yh, LANES - ROPE_DIM // 2, 1),
                                pltpu.roll(yh, ROPE_DIM // 2, 1))
            out.append((yh * cosv + partner * sinv) * scale)
        return out

    for h0 in range(0, DA_HEADS, heads_per_dot):
        qs = qk_heads(W_Q + h0 * DA_V, qkg_ref[0:1, :], DA_QK ** -0.5 * LOG2E)
        ks = qk_heads(W_K + h0 * DA_V, qkg_ref[1:2, :], 1.0)
        v = _dot(hn, w_ref[:, W_V + h0 * DA_V:W_V + h0 * DA_V + QK_COLS])
        for j in range(heads_per_dot):
            qt_ref[h0 + j] = qs[j].T.astype(BF16)
            kb_ref[h0 + j] = ks[j].astype(BF16)
            vt_ref[h0 + j] = v[:, j * DA_V:(j + 1) * DA_V].T.astype(BF16)
    rest_ref[:, R_SC:R_Z] = _dot(hn, w_ref[:, W_SC:W_S5])
    rest_ref[:, R_Z:R_DT] = _dot(hn, w_ref[:, W_Z:W_DT])
    rest_ref[:, R_DT:D_REST] = _dot(hn, wdt_ref[...])
    u5 = _dot(hn, w_ref[:, W_S5:W_Z])
    for hh in range(S5_WIDTH // LANES):
        u5_ref[hh] = u5[:, hh * LANES:(hh + 1) * LANES]


def _inproj(x2, g1, w, wdt, layer, qkg, cos_t, sin_t, gm, bsz, seq, tm):
    t = x2.shape[0]
    nt = seq // tm
    row = lambda i: (i, 0)
    tab = lambda i: (i % nt, 0)
    return pl.pallas_call(
        _inproj_kernel,
        grid=(t // tm,),
        in_specs=[
            pl.BlockSpec((tm, D_MODEL), row),
            _const_spec((1, D_MODEL)),
            _layer_spec(w.shape, layer),
            _layer_spec(wdt.shape, layer),
            _const_spec((2, LANES)),
            pl.BlockSpec((tm, LANES), tab),
            pl.BlockSpec((tm, LANES), tab),
            _const_spec((QK_COLS, QK_COLS)),
        ],
        out_specs=[
            pl.BlockSpec((None, DA_HEADS, DA_V, tm), lambda i: (i // nt, 0, 0, i % nt)),
            pl.BlockSpec((None, DA_HEADS, None, tm, DA_V), lambda i: (i // nt, 0, i % nt, 0, 0)),
            pl.BlockSpec((None, DA_HEADS, None, DA_V, tm), lambda i: (i // nt, 0, i % nt, 0, 0)),
            pl.BlockSpec((tm, D_REST), row),
            pl.BlockSpec((S5_WIDTH // LANES, tm, LANES), lambda i: (0, i, 0)),
        ],
        out_shape=[
            jax.ShapeDtypeStruct((bsz, DA_HEADS, DA_V, seq), BF16),
            jax.ShapeDtypeStruct((bsz, DA_HEADS, nt, tm, DA_V), BF16),
            jax.ShapeDtypeStruct((bsz, DA_HEADS, nt, DA_V, tm), BF16),
            jax.ShapeDtypeStruct((t, D_REST), F32),
            jax.ShapeDtypeStruct((S5_WIDTH // LANES, t, LANES), F32),
        ],
        compiler_params=pltpu.CompilerParams(
            dimension_semantics=("arbitrary",), vmem_limit_bytes=VMEM_LIMIT),
        name="inproj",
    )(x2, g1, w, wdt, qkg, cos_t, sin_t, gm)


def _attn_kernel(qt_ref, k_ref, vt_ref, lam_ref, sg_ref, o_ref, qz_sc, s_sc, m_sc, l_sc, acc_sc,
                 *, lambda_init):
    qi = pl.program_id(2)
    qt = qt_ref[...]
    comp = lax.broadcasted_iota(jnp.int32, qt.shape, 0) // DA_QK
    for c in range(2):
        qz_sc[c] = jnp.where(comp == c, qt, jnp.zeros_like(qt))
    m_sc[...] = jnp.full(m_sc.shape, -1e30, F32)
    l_sc[...] = jnp.zeros(l_sc.shape, F32)
    acc_sc[...] = jnp.zeros(acc_sc.shape, F32)

    def scores(j, buf):
        kb = k_ref[j]
        for c in range(2):
            s_sc[buf, c] = _dot(kb, qz_sc[c])

    def absorb(j, buf, masked):
        vtb = vt_ref[j]
        for c in range(2):
            s = s_sc[buf, c]
            if masked:
                kpos = lax.broadcasted_iota(jnp.int32, s.shape, 0)
                qpos = lax.broadcasted_iota(jnp.int32, s.shape, 1)
                s = jnp.where(kpos <= qpos, s, -jnp.inf)
            m_prev = m_sc[c]
            m_new = jnp.maximum(m_prev, jnp.max(s, axis=0, keepdims=True))
            alpha = jnp.exp2(m_prev - m_new)
            p = jnp.exp2(s - m_new)
            l_sc[c] = alpha * l_sc[c] + jnp.sum(p, axis=0, keepdims=True)
            acc_sc[c] = alpha * acc_sc[c] + _dot(vtb, p.astype(BF16))
            m_sc[c] = m_new

    scores(0, 0)
    pairs = qi // 2

    def body(t, carry):
        j = 2 * t
        scores(j + 1, 1)
        absorb(j, 0, False)
        scores(j + 2, 0)
        absorb(j + 1, 1, False)
        return carry

    lax.fori_loop(0, pairs, body, 0)
    j0 = 2 * pairs

    @pl.when(j0 == qi)
    def _():
        absorb(qi, 0, True)

    @pl.when(j0 != qi)
    def _():
        scores(qi, 1)
        absorb(j0, 0, False)
        absorb(qi, 1, True)

    lp = lam_ref[...]
    lam = (jnp.exp(jnp.sum(lp[0:1] * lp[1:2], axis=1, keepdims=True))
           - jnp.exp(jnp.sum(lp[2:3] * lp[3:4], axis=1, keepdims=True)) + lambda_init)
    ot = acc_sc[0] / l_sc[0] - lam * (acc_sc[1] / l_sc[1])
    ms = jnp.mean(ot * ot, axis=0, keepdims=True)
    o = (ot * lax.rsqrt(ms + EPS)).T * (sg_ref[...] * (1.0 - lambda_init))
    o_ref[...] = o.astype(o_ref.dtype)


def _attention(qt, kb, vtb, lam_p, sub_g, lambda_init):
    bsz, _, _, seq = qt.shape
    nb, blk = kb.shape[2], kb.shape[3]
    kern = functools.partial(_attn_kernel, lambda_init=lambda_init)
    return pl.pallas_call(
        kern,
        grid=(bsz, DA_HEADS, nb),
        in_specs=[
            pl.BlockSpec((None, None, DA_V, blk), lambda b, h, i: (b, h, 0, i)),
            pl.BlockSpec((None, None, nb, blk, DA_V), lambda b, h, i: (b, h, 0, 0, 0)),
            pl.BlockSpec((None, None, nb, DA_V, blk), lambda b, h, i: (b, h, 0, 0, 0)),
            pl.BlockSpec((4, DA_QK), lambda b, h, i: (0, 0)),
            pl.BlockSpec((1, DA_V), lambda b, h, i: (0, 0)),
        ],
        out_specs=pl.BlockSpec((None, blk, DA_V), lambda b, h, i: (b, i, h)),
        out_shape=jax.ShapeDtypeStruct((bsz, seq, DA_WIDTH), BF16),
        scratch_shapes=[
            pltpu.VMEM((2, DA_V, blk), BF16),
            pltpu.VMEM((2, 2, blk, blk), F32),
            pltpu.VMEM((2, 1, blk), F32),
            pltpu.VMEM((2, 1, blk), F32),
            pltpu.VMEM((2, DA_V, blk), F32),
        ],
        compiler_params=pltpu.CompilerParams(
            dimension_semantics=("arbitrary", "arbitrary", "arbitrary"),
            vmem_limit_bytes=VMEM_LIMIT),
        name="diff_attn",
    )(qt, kb, vtb, lam_p, sub_g)


def _expand_blockdiag(src_ref, dst_ref, row_group, col_div, col_inner):
    rows, wide = dst_ref.shape[1], dst_ref.shape[2]
    r_s = lax.broadcasted_iota(jnp.int32, (LANES, wide), 0)
    c_s = lax.broadcasted_iota(jnp.int32, (LANES, wide), 1)
    sel = jnp.where(r_s == (c_s // col_div) * col_inner + c_s % col_inner, 1.0, 0.0).astype(BF16)
    step = 256
    r_i = lax.broadcasted_iota(jnp.int32, (step, wide), 0)
    c_i = lax.broadcasted_iota(jnp.int32, (step, wide), 1)
    col_group = (c_i // (col_div // S5_HALF)) % S5_HALF
    for hh in range(dst_ref.shape[0]):
        for r0 in range(0, rows, step):
            same = ((r_i + r0) // row_group) % S5_HALF == col_group
            wide_blk = _dot(src_ref[hh, r0:r0 + step, :], sel)
            dst_ref[hh, r0:r0 + step, :] = jnp.where(same, wide_blk, 0.0).astype(BF16)


def _s5_kernel(u_ref, tc_ref, bc_ref, cc_ref, lam_ref, y_ref, toep_ref, bpow_ref, cpow_ref, st, hc,
               *, ls, tiles_per_seq):
    tr = u_ref.shape[1] // ls
    hs = S5_HSTATE

    @pl.when(pl.program_id(0) == 0)
    def _():
        _expand_blockdiag(tc_ref, toep_ref, S5_GROUP, LANES, S5_GROUP)
        _expand_blockdiag(bc_ref, bpow_ref, S5_GROUP, S5_HSTATE, S5_STATE)
        _expand_blockdiag(cc_ref, cpow_ref, S5_STATE, LANES, S5_GROUP)

    @pl.when(pl.program_id(0) % tiles_per_seq == 0)
    def _():
        hc[...] = jnp.zeros(hc.shape, F32)

    def half_input(hh):
        cols = [u_ref[hh, pl.ds(i, tr, stride=ls), :] for i in range(ls)]
        return jnp.concatenate(cols, axis=1).astype(BF16)

    for hh in range(2):
        st[hh] = _dot(half_input(hh), bpow_ref[hh])

    lam = [(lam_ref[hh, 0:1, :], lam_ref[hh, 1:2, :]) for hh in range(2)]

    def body(c, carry):
        new = []
        for hh in range(2):
            hr, hi = carry[2 * hh], carry[2 * hh + 1]
            lr, li = lam[hh]
            s_r = st[hh, pl.ds(c, 1), 0:hs]
            s_i = st[hh, pl.ds(c, 1), hs:2 * hs]
            st[hh, pl.ds(c, 1), 0:hs] = hr
            st[hh, pl.ds(c, 1), hs:2 * hs] = hi
            new.append(lr * hr - li * hi + s_r)
            new.append(lr * hi + li * hr + s_i)
        return tuple(new)

    init = tuple(hc[hh, :, ri * hs:(ri + 1) * hs] for hh in range(2) for ri in range(2))
    fin = lax.fori_loop(0, tr, body, init)
    for hh in range(2):
        for ri in range(2):
            hc[hh, :, ri * hs:(ri + 1) * hs] = fin[2 * hh + ri]

    for hh in range(2):
        y = _dot(half_input(hh), toep_ref[hh]) + _dot(st[hh].astype(BF16), cpow_ref[hh])
        for i in range(ls):
            y_ref[hh, pl.ds(i, tr, stride=ls), :] = y[:, i * LANES:(i + 1) * LANES]


def _s5_scan(u5, tc, bc, cc, lam_l, layer, ls, nc, tr):
    nh, t, w = u5.shape
    rows = tr * ls
    wide = ls * LANES
    kern = functools.partial(_s5_kernel, ls=ls, tiles_per_seq=nc // tr)
    row = lambda i: (0, i, 0)
    return pl.pallas_call(
        kern,
        grid=(t // rows,),
        in_specs=[
            pl.BlockSpec((nh, rows, w), row),
            _layer_spec(tc.shape, layer),
            _layer_spec(bc.shape, layer),
            _layer_spec(cc.shape, layer),
            _layer_spec(lam_l.shape, layer),
        ],
        out_specs=pl.BlockSpec((nh, rows, w), row),
        out_shape=jax.ShapeDtypeStruct((nh, t, w), F32),
        scratch_shapes=[
            pltpu.VMEM((nh, wide, wide), BF16),
            pltpu.VMEM((nh, wide, 2 * S5_HSTATE), BF16),
            pltpu.VMEM((nh, 2 * S5_HSTATE, wide), BF16),
            pltpu.VMEM((nh, tr, 2 * S5_HSTATE), F32),
            pltpu.VMEM((nh, 1, 2 * S5_HSTATE), F32),
        ],
        compiler_params=pltpu.CompilerParams(
            dimension_semantics=("arbitrary",), vmem_limit_bytes=VMEM_LIMIT),
        name="s5_scan",
    )(u5, tc, bc, cc, lam_l)


def _s5_operators(lam_re, lam_im, log_dt, b_re, b_im, c_re, c_im, ls):
    hp = lax.Precision.HIGHEST
    dt = jnp.exp(log_dt)[:, None]
    zr, zi = lam_re * dt, lam_im * dt
    er = jnp.exp(zr)
    lbr, lbi = er * jnp.cos(zi), er * jnp.sin(zi)
    den = lam_re * lam_re + lam_im * lam_im
    nr, ni = lbr - 1.0, lbi
    fr = (nr * lam_re + ni * lam_im) / den
    fi = (ni * lam_re - nr * lam_im) / den
    bbr = fr[..., None] * b_re - fi[..., None] * b_im
    bbi = fr[..., None] * b_im + fi[..., None] * b_re
    tau = jnp.arange(ls + 1, dtype=F32)
    pe = jnp.exp(zr[..., None] * tau)
    pr, pi_ = pe * jnp.cos(zi[..., None] * tau), pe * jnp.sin(zi[..., None] * tau)
    cpr = c_re[:, :, :, None] * pr[:, None, :, :ls] - c_im[:, :, :, None] * pi_[:, None, :, :ls]
    cpi = c_re[:, :, :, None] * pi_[:, None, :, :ls] + c_im[:, :, :, None] * pr[:, None, :, :ls]
    kk = (jnp.einsum('ghpt,gpk->gthk', cpr, bbr, precision=hp)
          - jnp.einsum('ghpt,gpk->gthk', cpi, bbi, precision=hp))
    lag = jnp.arange(ls)[None, :] - jnp.arange(ls)[:, None]
    kt = kk[:, jnp.clip(lag, 0, ls - 1)]
    kt = jnp.where((lag >= 0)[None, :, :, None, None], kt, 0.0)
    nh, gh = S5_GROUPS // S5_HALF, S5_HALF
    w = ls * LANES
    kt = kt.reshape(nh, gh, ls, ls, S5_GROUP, S5_GROUP)
    tc = kt.transpose(0, 2, 1, 5, 3, 4).reshape(nh, w, ls * S5_GROUP)
    tau_rev = (ls - 1) - jnp.arange(ls, dtype=F32)
    pe_rev = jnp.exp(zr[..., None] * tau_rev)
    rev_r, rev_i = pe_rev * jnp.cos(zi[..., None] * tau_rev), pe_rev * jnp.sin(zi[..., None] * tau_rev)
    bre = rev_r[..., None] * bbr[:, :, None, :] - rev_i[..., None] * bbi[:, :, None, :]
    bim = rev_r[..., None] * bbi[:, :, None, :] + rev_i[..., None] * bbr[:, :, None, :]
    bb = jnp.stack([bre, bim], axis=0).reshape(2, nh, gh, S5_STATE, ls, S5_GROUP)
    bc = bb.transpose(1, 4, 2, 5, 0, 3).reshape(nh, w, 2 * S5_STATE)
    qr = c_re[:, :, :, None] * pr[:, None, :, 1:] - c_im[:, :, :, None] * pi_[:, None, :, 1:]
    qi = c_re[:, :, :, None] * pi_[:, None, :, 1:] + c_im[:, :, :, None] * pr[:, None, :, 1:]
    cc = jnp.stack([qr, -qi], axis=0).reshape(2, nh, gh, S5_GROUP, S5_STATE, ls)
    cc = cc.transpose(1, 0, 2, 4, 5, 3).reshape(nh, 2 * S5_HSTATE, ls * S5_GROUP)
    lam_l = jnp.stack([pr[:, :, ls].reshape(nh, S5_HSTATE), pi_[:, :, ls].reshape(nh, S5_HSTATE)], axis=1)
    return tc.astype(BF16), bc.astype(BF16), cc.astype(BF16), lam_l.astype(F32)


def _mix_kernel(rest_ref, u5_ref, ys5_ref, scw_ref, s5d_ref, wglu_ref, bglu_ref, mcw_ref, mcb_ref,
                dtb_ref, alog_ref, dvec_ref, ng_ref, o_ref, cbuf, mbuf, hst, ybuf, *, L):
    @pl.when(pl.program_id(1) == 0)
    def _():
        cbuf[0:HIST, :] = jnp.zeros((HIST, SC_WIDTH), F32)
        mbuf[0:HIST, :] = jnp.zeros((HIST, M2_XBC), F32)
        hst[...] = jnp.zeros(hst.shape, F32)

    u = rest_ref[:, R_SC + SC_WIDTH:R_SC + 2 * SC_WIDTH] * rest_ref[:, R_SC + 2 * SC_WIDTH:R_SC + 3 * SC_WIDTH]
    cbuf[HIST:HIST + L, :] = u
    conv = (scw_ref[2:3, :] * u + scw_ref[1:2, :] * cbuf[HIST - 1:HIST - 1 + L, :]
            + scw_ref[0:1, :] * cbuf[HIST - 2:HIST - 2 + L, :])
    cbuf[0:HIST, :] = cbuf[L:L + HIST, :]
    o_ref[:, 0:SC_WIDTH] = (rest_ref[:, R_SC:R_SC + SC_WIDTH] * conv).astype(o_ref.dtype)

    halves = lambda ref: jnp.concatenate([ref[hh] for hh in range(S5_WIDTH // LANES)], axis=1)
    y5 = halves(ys5_ref) + s5d_ref[...] * halves(u5_ref)
    gl = jax.nn.gelu(y5, approximate=True)
    gate = _dot(gl.astype(BF16), wglu_ref[...]) + bglu_ref[...]
    o_ref[:, SC_WIDTH:SC_WIDTH + S5_WIDTH] = (gl * (1.0 / (1.0 + jnp.exp(-gate)))).astype(o_ref.dtype)

    xr = rest_ref[:, R_XBC:R_XBC + M2_XBC]
    mbuf[HIST:HIST + L, :] = xr
    conv = (mcw_ref[3:4, :] * xr + mcw_ref[2:3, :] * mbuf[HIST - 1:HIST - 1 + L, :]
            + mcw_ref[1:2, :] * mbuf[HIST - 2:HIST - 2 + L, :]
            + mcw_ref[0:1, :] * mbuf[HIST - 3:HIST - 3 + L, :] + mcb_ref[...])
    mbuf[0:HIST, :] = mbuf[L:L + HIST, :]
    xbc = _silu(conv)
    xs = xbc[:, 0:M2_WIDTH]
    dtr = rest_ref[:, R_DT:R_DT + LANES] + dtb_ref[...]
    dtv = jnp.maximum(dtr, 0.0) + jnp.log(1.0 + jnp.exp(-jnp.abs(dtr)))
    a = -jnp.exp(alog_ref[...]) * dtv
    r_i = lax.broadcasted_iota(jnp.int32, (L, L), 0)
    c_i = lax.broadcasted_iota(jnp.int32, (L, L), 1)
    tril = c_i <= r_i
    tri = jnp.where(tril, 1.0, 0.0).astype(BF16)
    cs = sum(_dot(tri, part) for part in _split_bf16(a, 3))
    cs_t = cs.T
    cs_last = cs[L - 1:L, :]
    ecs = jnp.exp(cs)
    dec = jnp.exp(cs_last - cs)
    ecl = jnp.exp(cs_last)
    lane = lax.broadcasted_iota(jnp.int32, (1, LANES), 1)
    for g in range(M2_GROUPS):
        bg = xbc[:, M2_WIDTH + g * M2_STATE:M2_WIDTH + (g + 1) * M2_STATE].astype(BF16)
        cg = xbc[:, M2_WIDTH + (M2_GROUPS + g) * M2_STATE:M2_WIDTH + (M2_GROUPS + g + 1) * M2_STATE].astype(BF16)
        gram = _dot_nt(cg, bg)
        hs = hst[g]
        yoff = _dot(cg, hs.astype(BF16))
        xdec = []
        for hh in range(2):
            h = 2 * g + hh
            seg = cs[:, h:h + 1] - cs_t[h:h + 1, :]
            dm = jnp.exp(jnp.where(tril, seg, -jnp.inf))
            xdt = xs[:, h * M2_HEAD_DIM:(h + 1) * M2_HEAD_DIM] * dtv[:, h:h + 1]
            yd = _dot((gram * dm).astype(BF16), xdt.astype(BF16))
            yo = yoff[:, hh * M2_HEAD_DIM:(hh + 1) * M2_HEAD_DIM] * ecs[:, h:h + 1]
            ybuf[:, h * M2_HEAD_DIM:(h + 1) * M2_HEAD_DIM] = yd + yo
            xdec.append(xdt * dec[:, h:h + 1])
        upd = _dot_tn(bg, jnp.concatenate(xdec, axis=1).astype(BF16))
        keep = jnp.where(lane < M2_HEAD_DIM, ecl[:, 2 * g:2 * g + 1], ecl[:, 2 * g + 1:2 * g + 2])
        hst[g] = hs * keep + upd
    y = ybuf[...] + dvec_ref[...] * xs
    yg = y * _silu(rest_ref[:, R_Z:R_Z + M2_WIDTH])
    ms = jnp.mean(yg * yg, axis=-1, keepdims=True)
    o_ref[:, SC_WIDTH + S5_WIDTH:] = (yg * lax.rsqrt(ms + EPS) * ng_ref[...]).astype(o_ref.dtype)


def _mixers(rest, u5, ys5, scw, s5d, wglu, bglu, mcw, mcb, dtb, alog, dvec, ng, L):
    bsz, seq, _ = rest.shape
    kern = functools.partial(_mix_kernel, L=L)
    blk = lambda b, c: (b, c, 0)
    wout = SC_WIDTH + S5_WIDTH + M2_WIDTH
    return pl.pallas_call(
        kern,
        grid=(bsz, seq // L),
        in_specs=[
            pl.BlockSpec((None, L, D_REST), blk),
            pl.BlockSpec((S5_WIDTH // LANES, L, LANES), lambda b, c: (0, b * (seq // L) + c, 0)),
            pl.BlockSpec((S5_WIDTH // LANES, L, LANES), lambda b, c: (0, b * (seq // L) + c, 0)),
            _const_spec((3, SC_WIDTH)),
            _const_spec((1, S5_WIDTH)),
            _const_spec((S5_WIDTH, S5_WIDTH)),
            _const_spec((1, S5_WIDTH)),
            _const_spec((4, M2_XBC)),
            _const_spec((1, M2_XBC)),
            _const_spec((1, LANES)),
            _const_spec((1, LANES)),
            _const_spec((1, M2_WIDTH)),
            _const_spec((1, M2_WIDTH)),
        ],
        out_specs=pl.BlockSpec((None, L, wout), blk),
        out_shape=jax.ShapeDtypeStruct((bsz, seq, wout), BF16),
        scratch_shapes=[
            pltpu.VMEM((L + HIST, SC_WIDTH), F32),
            pltpu.VMEM((L + HIST, M2_XBC), F32),
            pltpu.VMEM((M2_GROUPS, M2_STATE, 2 * M2_HEAD_DIM), F32),
            pltpu.VMEM((L, M2_WIDTH), F32),
        ],
        compiler_params=pltpu.CompilerParams(
            dimension_semantics=("arbitrary", "arbitrary"), vmem_limit_bytes=VMEM_LIMIT),
        name="mixers",
    )(rest, u5, ys5, scw, s5d, wglu, bglu, mcw, mcb, dtb, alog, dvec, ng)


def _ffn_kernel(x_ref, oa_ref, ob_ref, wo_ref, g2_ref, wg_ref, wu_ref, cw_ref, wd_ref,
                o_ref, hist, cb, *, tiles_per_seq, tf):
    tm = x_ref.shape[0]

    @pl.when(pl.program_id(0) % tiles_per_seq == 0)
    def _():
        hist[...] = jnp.zeros(hist.shape, F32)

    x1 = (x_ref[...] + _dot(oa_ref[...], wo_ref[0:DA_WIDTH, :])
          + _dot(ob_ref[...], wo_ref[DA_WIDTH:D_MIX, :]))
    ms = jnp.mean(x1 * x1, axis=-1, keepdims=True)
    h2 = (x1 * lax.rsqrt(ms + EPS) * g2_ref[...]).astype(BF16)
    o_ref[...] = x1
    for c in range(D_FF // tf):
        sl = slice(c * tf, (c + 1) * tf)
        gpre = _dot(h2, wg_ref[:, sl])
        cb[0:HIST, :] = hist[:, sl]
        cb[HIST:HIST + tm, :] = gpre
        gc = (cw_ref[2:3, sl] * gpre + cw_ref[1:2, sl] * cb[HIST - 1:HIST - 1 + tm, :]
              + cw_ref[0:1, sl] * cb[HIST - 2:HIST - 2 + tm, :])
        hist[:, sl] = cb[tm:tm + HIST, :]
        act = (_silu(gc) * _dot(h2, wu_ref[:, sl])).astype(BF16)
        o_ref[...] += _dot(act, wd_ref[sl, :])


def _outproj_ffn(x2, oa, ob, wo, g2, wg, wu, cw, wd, layer, seq, tm, tf):
    t = x2.shape[0]
    wb = ob.shape[1]
    kern = functools.partial(_ffn_kernel, tiles_per_seq=seq // tm, tf=tf)
    row = lambda i: (i, 0)
    return pl.pallas_call(
        kern,
        grid=(t // tm,),
        in_specs=[
            pl.BlockSpec((tm, D_MODEL), row),
            pl.BlockSpec((tm, DA_WIDTH), row),
            pl.BlockSpec((tm, wb), row),
            _layer_spec(wo.shape, layer),
            _const_spec((1, D_MODEL)),
            _layer_spec(wg.shape, layer),
            _layer_spec(wu.shape, layer),
            _const_spec((3, D_FF)),
            _layer_spec(wd.shape, layer),
        ],
        out_specs=pl.BlockSpec((tm, D_MODEL), row),
        out_shape=jax.ShapeDtypeStruct((t, D_MODEL), F32),
        scratch_shapes=[
            pltpu.VMEM((HIST, D_FF), F32),
            pltpu.VMEM((tm + HIST, tf), F32),
        ],
        compiler_params=pltpu.CompilerParams(
            dimension_semantics=("arbitrary",), vmem_limit_bytes=VMEM_LIMIT),
        name="outproj_ffn",
    )(x2, oa, ob, wo, g2, wg, wu, cw, wd)


def _rope_lane_tables(seq):
    inv = 1.0 / (ROPE_THETA ** (jnp.arange(0, ROPE_DIM, 2, dtype=F32) / ROPE_DIM))
    ang = jnp.arange(seq, dtype=F32)[:, None] * inv[None, :]
    cos, sin = jnp.cos(ang), jnp.sin(ang)
    ones = jnp.ones((seq, DA_QK - ROPE_DIM), F32)
    cos64 = jnp.concatenate([cos, cos, ones], axis=1)
    sin64 = jnp.concatenate([-sin, sin, 0.0 * ones], axis=1)
    return jnp.tile(cos64, (1, LANES // DA_QK)), jnp.tile(sin64, (1, LANES // DA_QK))


def _forward(x, p, blk, mix_l, s5_tr):
    bsz, seq, _ = x.shape
    depth = p["w_in"].shape[0]
    t = bsz * seq
    ls = S5_CHUNK
    nc = seq // ls
    assert seq % blk == 0 and seq % mix_l == 0 and nc % s5_tr == 0

    cos_t, sin_t = _rope_lane_tables(seq)
    comp = lax.broadcasted_iota(jnp.int32, (QK_COLS, QK_COLS), 0) // DA_QK
    gm = (comp == comp.T).astype(BF16)
    w_in = p["w_in"].astype(BF16)
    w_dt = jnp.pad(p["w_in"][:, :, W_DT:], ((0, 0), (0, 0), (0, LANES - M2_HEADS))).astype(BF16)
    w_out = p["w_out"].astype(BF16)
    w_gate, w_up, w_down = (p[k].astype(BF16) for k in ("ffn_w_gate", "ffn_w_up", "ffn_w_down"))
    s5_ops = jax.vmap(functools.partial(_s5_operators, ls=ls))(
        p["s5_lam_re"], p["s5_lam_im"], p["s5_log_dt"], p["s5_b_re"], p["s5_b_im"],
        p["s5_c_re"], p["s5_c_im"])
    x2 = x.reshape(t, D_MODEL)
    for l in range(depth):
        lambda_init = 0.8 - 0.6 * math.exp(-0.3 * l)
        qkg = jnp.tile(p["qk_norm_g"][l], (1, LANES // DA_QK))
        qt, kb, vtb, rest, u5 = _inproj(x2, p["ln1_g"][l][None, :], w_in, w_dt, l, qkg, cos_t, sin_t,
                                        gm, bsz, seq, blk)
        oa = _attention(qt, kb, vtb, p["da_lambda"][l], p["subln_g"][l][None, :], lambda_init)
        ys5 = _s5_scan(u5, *s5_ops, l, ls, nc, s5_tr)

        pad4 = lambda a: jnp.pad(a, (0, LANES - M2_HEADS))[None, :]
        ob = _mixers(rest.reshape(bsz, seq, D_REST), u5, ys5, p["sc_conv_w"][l], p["s5_d"][l][None, :],
                     p["s5_w_glu"][l].astype(BF16), p["s5_b_glu"][l][None, :], p["m2_conv_w"][l],
                     p["m2_conv_b"][l][None, :], pad4(p["m2_dt_bias"][l]), pad4(p["m2_a_log"][l]),
                     jnp.repeat(p["m2_d"][l], M2_HEAD_DIM)[None, :], p["m2_norm_g"][l][None, :], mix_l)

        x2 = _outproj_ffn(x2, oa.reshape(t, DA_WIDTH), ob.reshape(t, D_MIX - DA_WIDTH), w_out,
                          p["ln2_g"][l][None, :], w_gate, w_up, p["ffn_conv_w"][l], w_down,
                          l, seq, blk, 256)
    return x2.reshape(bsz, seq, D_MODEL)


def kernel(x, ln1_g, w_in, qk_norm_g, da_lambda, subln_g, sc_conv_w, s5_lam_re, s5_lam_im, s5_log_dt, s5_b_re, s5_b_im, s5_c_re, s5_c_im, s5_d, s5_w_glu, s5_b_glu, m2_conv_w, m2_conv_b, m2_dt_bias, m2_a_log, m2_d, m2_norm_g, w_out, ln2_g, ffn_w_gate, ffn_w_up, ffn_conv_w, ffn_w_down):
    params = dict(ln1_g=ln1_g, w_in=w_in, qk_norm_g=qk_norm_g, da_lambda=da_lambda, subln_g=subln_g,
                  sc_conv_w=sc_conv_w, s5_lam_re=s5_lam_re, s5_lam_im=s5_lam_im, s5_log_dt=s5_log_dt,
                  s5_b_re=s5_b_re, s5_b_im=s5_b_im, s5_c_re=s5_c_re, s5_c_im=s5_c_im, s5_d=s5_d,
                  s5_w_glu=s5_w_glu, s5_b_glu=s5_b_glu, m2_conv_w=m2_conv_w, m2_conv_b=m2_conv_b,
                  m2_dt_bias=m2_dt_bias, m2_a_log=m2_a_log, m2_d=m2_d, m2_norm_g=m2_norm_g,
                  w_out=w_out, ln2_g=ln2_g, ffn_w_gate=ffn_w_gate, ffn_w_up=ffn_w_up,
                  ffn_conv_w=ffn_conv_w, ffn_w_down=ffn_w_down)
    seq = x.shape[1]
    return _forward(x, params, blk=min(512, seq), mix_l=min(256, seq),
                    s5_tr=min(256, seq // S5_CHUNK))
```

```python
import functools
import math

import jax
import jax.numpy as jnp
from jax import lax
from jax.experimental import pallas as pl
from jax.experimental.pallas import tpu as pltpu

F32 = jnp.float32
BF16 = jnp.bfloat16

D_MODEL = 1024
DA_HEADS = 4
DA_QK = 64
DA_V = 2 * DA_QK
DA_WIDTH = DA_HEADS * DA_V
ROPE_DIM = DA_QK // 4
ROPE_THETA = 500000.0
SC_WIDTH = 256
S5_WIDTH = 256
S5_GROUP = 16
S5_GROUPS = S5_WIDTH // S5_GROUP
S5_STATE = 64
M2_HEADS = 4
M2_HEAD_DIM = 64
M2_WIDTH = M2_HEADS * M2_HEAD_DIM
M2_GROUPS = 2
M2_STATE = 128
M2_XBC = M2_WIDTH + 2 * M2_GROUPS * M2_STATE
D_MIX = DA_WIDTH + SC_WIDTH + S5_WIDTH + M2_WIDTH
D_FF = 2816
EPS = 1e-6
LOG2E = 1.4426950408889634

LANES = 128
HIST = 8
VMEM_LIMIT = 56 * 1024 * 1024

W_Q = 0
W_K = W_Q + DA_WIDTH
W_V = W_K + DA_WIDTH
W_SC = W_V + DA_WIDTH
W_S5 = W_SC + 3 * SC_WIDTH
W_Z = W_S5 + S5_WIDTH
W_DT = W_Z + M2_WIDTH + M2_XBC
D_PROJ = W_DT + M2_HEADS
R_SC = 0
R_Z = R_SC + 3 * SC_WIDTH
R_XBC = R_Z + M2_WIDTH
R_DT = R_XBC + M2_XBC
D_REST = R_DT + LANES
QK_COLS = 2 * LANES
ATT_HEADS_PER_STEP = 2
FFN_ROWS = 512

S5_CHUNK = 8
S5_HALF = LANES // S5_GROUP
S5_HSTATE = S5_HALF * S5_STATE


def _dot(a, b):
    return jnp.dot(a, b, preferred_element_type=F32)


def _dot_nt(a, b):
    return lax.dot_general(a, b, (((1,), (1,)), ((), ())), preferred_element_type=F32)


def _dot_tn(a, b):
    return lax.dot_general(a, b, (((0,), (0,)), ((), ())), preferred_element_type=F32)


def _split_bf16(x, parts):
    out = []
    r = x
    for _ in range(parts):
        p = r.astype(BF16)
        out.append(p)
        r = r - p.astype(F32)
    return out


def _silu(x):
    return x * (1.0 / (1.0 + jnp.exp(-x)))


def _const_spec(shape):
    nd = len(shape)
    return pl.BlockSpec(shape, lambda *_: (0,) * nd, pipeline_mode=pl.Buffered(1))


def _layer_spec(shape, layer):
    nd = len(shape)
    return pl.BlockSpec((None,) + tuple(shape[1:]), lambda *_: (layer,) + (0,) * (nd - 1),
                        pipeline_mode=pl.Buffered(1))


def _inproj_kernel(x_ref, g1_ref, w_ref, wdt_ref, qkg_ref, cos_ref, sin_ref, gm_ref,
                   qt_ref, kb_ref, vt_ref, rest_ref, u5_ref):
    tm = x_ref.shape[0]
    x = x_ref[...]
    ms = jnp.mean(x * x, axis=-1, keepdims=True)
    hn = (x * lax.rsqrt(ms + EPS) * g1_ref[...]).astype(BF16)
    cosv = cos_ref[...]
    sinv = sin_ref[...]
    gm = gm_ref[...]
    lane = lax.broadcasted_iota(jnp.int32, (tm, LANES), 1)
    pair_up = (lane % DA_QK) < (ROPE_DIM // 2)
    heads_per_dot = QK_COLS // DA_V

    def qk_heads(col0, gvec, scale):
        y = _dot(hn, w_ref[:, col0:col0 + QK_COLS])
        hi, lo = _split_bf16(y * y, 2)
        ss = _dot(hi, gm) + _dot(lo, gm)
        yn = y * lax.rsqrt(ss * (1.0 / DA_QK) + EPS)
        out = []
        for j in range(heads_per_dot):
            yh = yn[:, j * DA_V:(j + 1) * DA_V] * gvec
            partner = jnp.where(pair_up,
                                pltpu.roll(yh, LANES - ROPE_DIM // 2, 1),
                                pltpu.roll(yh, ROPE_DIM // 2, 1))
            out.append((yh * cosv + partner * sinv) * scale)
        return out

    for h0 in range(0, DA_HEADS, heads_per_dot):
        qs = qk_heads(W_Q + h0 * DA_V, qkg_ref[0:1, :], DA_QK ** -0.5 * LOG2E)
        ks = qk_heads(W_K + h0 * DA_V, qkg_ref[1:2, :], 1.0)
        v = _dot(hn, w_ref[:, W_V + h0 * DA_V:W_V + h0 * DA_V + QK_COLS])
        for j in range(heads_per_dot):
            qt_ref[h0 + j] = qs[j].T.astype(BF16)
            kb_ref[h0 + j] = ks[j].astype(BF16)
            vt_ref[h0 + j] = v[:, j * DA_V:(j + 1) * DA_V].T.astype(BF16)
    rest_ref[:, R_SC:R_Z] = _dot(hn, w_ref[:, W_SC:W_S5])
    rest_ref[:, R_Z:R_DT] = _dot(hn, w_ref[:, W_Z:W_DT])
    rest_ref[:, R_DT:D_REST] = _dot(hn, wdt_ref[...])
    u5 = _dot(hn, w_ref[:, W_S5:W_Z])
    for hh in range(S5_WIDTH // LANES):
        u5_ref[hh] = u5[:, hh * LANES:(hh + 1) * LANES]


def _inproj(x2, g1, w, wdt, layer, qkg, cos_t, sin_t, gm, bsz, seq, tm):
    t = x2.shape[0]
    nt = seq // tm
    row = lambda i: (i, 0)
    tab = lambda i: (i % nt, 0)
    return pl.pallas_call(
        _inproj_kernel,
        grid=(t // tm,),
        in_specs=[
            pl.BlockSpec((tm, D_MODEL), row),
            _const_spec((1, D_MODEL)),
            _layer_spec(w.shape, layer),
            _layer_spec(wdt.shape, layer),
            _const_spec((2, LANES)),
            pl.BlockSpec((tm, LANES), tab),
            pl.BlockSpec((tm, LANES), tab),
            _const_spec((QK_COLS, QK_COLS)),
        ],
        out_specs=[
            pl.BlockSpec((None, DA_HEADS, DA_V, tm), lambda i: (i // nt, 0, 0, i % nt)),
            pl.BlockSpec((None, DA_HEADS, None, tm, DA_V), lambda i: (i // nt, 0, i % nt, 0, 0)),
            pl.BlockSpec((None, DA_HEADS, None, DA_V, tm), lambda i: (i // nt, 0, i % nt, 0, 0)),
            pl.BlockSpec((tm, D_REST), row),
            pl.BlockSpec((S5_WIDTH // LANES, tm, LANES), lambda i: (0, i, 0)),
        ],
        out_shape=[
            jax.ShapeDtypeStruct((bsz, DA_HEADS, DA_V, seq), BF16),
            jax.ShapeDtypeStruct((bsz, DA_HEADS, nt, tm, DA_V), BF16),
            jax.ShapeDtypeStruct((bsz, DA_HEADS, nt, DA_V, tm), BF16),
            jax.ShapeDtypeStruct((t, D_REST), F32),
            jax.ShapeDtypeStruct((S5_WIDTH // LANES, t, LANES), F32),
        ],
        compiler_params=pltpu.CompilerParams(
            dimension_semantics=("arbitrary",), vmem_limit_bytes=VMEM_LIMIT),
        name="inproj",
    )(x2, g1, w, wdt, qkg, cos_t, sin_t, gm)


def _attn_kernel(qt_ref, k_ref, vt_ref, lam_ref, sg_ref, o_ref, qz_sc, s_sc, m_sc, l_sc, acc_sc,
                 *, lambda_init):
    qi = pl.program_id(2)
    heads = qt_ref.shape[0]
    streams = [(hd, c) for hd in range(heads) for c in range(2)]
    for n, (hd, c) in enumerate(streams):
        qt = qt_ref[hd]
        comp = lax.broadcasted_iota(jnp.int32, qt.shape, 0) // DA_QK
        qz_sc[n] = jnp.where(comp == c, qt, jnp.zeros_like(qt))
    m_sc[...] = jnp.full(m_sc.shape, -1e30, F32)
    l_sc[...] = jnp.zeros(l_sc.shape, F32)
    acc_sc[...] = jnp.zeros(acc_sc.shape, F32)

    def scores(j, buf):
        for n, (hd, c) in enumerate(streams):
            s_sc[buf, n] = _dot(k_ref[hd, j], qz_sc[n])

    def absorb(j, buf, masked):
        for n, (hd, c) in enumerate(streams):
            s = s_sc[buf, n]
            if masked:
                kpos = lax.broadcasted_iota(jnp.int32, s.shape, 0)
                qpos = lax.broadcasted_iota(jnp.int32, s.shape, 1)
                s = jnp.where(kpos <= qpos, s, -jnp.inf)
            m_prev = m_sc[n]
            m_new = jnp.maximum(m_prev, jnp.max(s, axis=0, keepdims=True))
            alpha = jnp.exp2(m_prev - m_new)
            p = jnp.exp2(s - m_new)
            l_sc[n] = alpha * l_sc[n] + jnp.sum(p, axis=0, keepdims=True)
            acc_sc[n] = alpha * acc_sc[n] + _dot(vt_ref[hd, j], p.astype(BF16))
            m_sc[n] = m_new

    scores(0, 0)
    pairs = qi // 2

    def body(t, carry):
        j = 2 * t
        scores(j + 1, 1)
        absorb(j, 0, False)
        scores(j + 2, 0)
        absorb(j + 1, 1, False)
        return carry

    lax.fori_loop(0, pairs, body, 0)
    j0 = 2 * pairs

    @pl.when(j0 == qi)
    def _():
        absorb(qi, 0, True)

    @pl.when(j0 != qi)
    def _():
        scores(qi, 1)
        absorb(j0, 0, False)
        absorb(qi, 1, True)

    lp = lam_ref[...]
    lam = (jnp.exp(jnp.sum(lp[0:1] * lp[1:2], axis=1, keepdims=True))
           - jnp.exp(jnp.sum(lp[2:3] * lp[3:4], axis=1, keepdims=True)) + lambda_init)
    for hd in range(heads):
        ot = acc_sc[2 * hd] / l_sc[2 * hd] - lam * (acc_sc[2 * hd + 1] / l_sc[2 * hd + 1])
        ms = jnp.mean(ot * ot, axis=0, keepdims=True)
        o = (ot * lax.rsqrt(ms + EPS)).T * (sg_ref[...] * (1.0 - lambda_init))
        o_ref[:, hd * DA_V:(hd + 1) * DA_V] = o.astype(o_ref.dtype)


def _attention(qt, kb, vtb, lam_p, sub_g, lambda_init):
    bsz, _, _, seq = qt.shape
    nb, blk = kb.shape[2], kb.shape[3]
    hps = ATT_HEADS_PER_STEP
    kern = functools.partial(_attn_kernel, lambda_init=lambda_init)
    return pl.pallas_call(
        kern,
        grid=(bsz, DA_HEADS // hps, nb),
        in_specs=[
            pl.BlockSpec((None, hps, DA_V, blk), lambda b, h, i: (b, h, 0, i)),
            pl.BlockSpec((None, hps, nb, blk, DA_V), lambda b, h, i: (b, h, 0, 0, 0)),
            pl.BlockSpec((None, hps, nb, DA_V, blk), lambda b, h, i: (b, h, 0, 0, 0)),
            pl.BlockSpec((4, DA_QK), lambda b, h, i: (0, 0)),
            pl.BlockSpec((1, DA_V), lambda b, h, i: (0, 0)),
        ],
        out_specs=pl.BlockSpec((None, blk, hps * DA_V), lambda b, h, i: (b, i, h)),
        out_shape=jax.ShapeDtypeStruct((bsz, seq, DA_WIDTH), BF16),
        scratch_shapes=[
            pltpu.VMEM((2 * hps, DA_V, blk), BF16),
            pltpu.VMEM((2, 2 * hps, blk, blk), F32),
            pltpu.VMEM((2 * hps, 1, blk), F32),
            pltpu.VMEM((2 * hps, 1, blk), F32),
            pltpu.VMEM((2 * hps, DA_V, blk), F32),
        ],
        compiler_params=pltpu.CompilerParams(
            dimension_semantics=("arbitrary", "arbitrary", "arbitrary"),
            vmem_limit_bytes=VMEM_LIMIT),
        name="diff_attn",
    )(qt, kb, vtb, lam_p, sub_g)


def _expand_blockdiag(src_ref, dst_ref, row_group, col_div, col_inner):
    rows, wide = dst_ref.shape[1], dst_ref.shape[2]
    r_s = lax.broadcasted_iota(jnp.int32, (LANES, wide), 0)
    c_s = lax.broadcasted_iota(jnp.int32, (LANES, wide), 1)
    sel = jnp.where(r_s == (c_s // col_div) * col_inner + c_s % col_inner, 1.0, 0.0).astype(BF16)
    step = 256
    r_i = lax.broadcasted_iota(jnp.int32, (step, wide), 0)
    c_i = lax.broadcasted_iota(jnp.int32, (step, wide), 1)
    col_group = (c_i // (col_div // S5_HALF)) % S5_HALF
    for hh in range(dst_ref.shape[0]):
        for r0 in range(0, rows, step):
            same = ((r_i + r0) // row_group) % S5_HALF == col_group
            wide_blk = _dot(src_ref[hh, r0:r0 + step, :], sel)
            dst_ref[hh, r0:r0 + step, :] = jnp.where(same, wide_blk, 0.0).astype(BF16)


def _s5_kernel(u_ref, tc_ref, bc_ref, cc_ref, lam_ref, y_ref, toep_ref, bpow_ref, cpow_ref, st, hc,
               *, ls, tiles_per_seq):
    tr = u_ref.shape[1] // ls
    hs = S5_HSTATE

    @pl.when(pl.program_id(0) == 0)
    def _():
        _expand_blockdiag(tc_ref, toep_ref, S5_GROUP, LANES, S5_GROUP)
        _expand_blockdiag(bc_ref, bpow_ref, S5_GROUP, S5_HSTATE, S5_STATE)
        _expand_blockdiag(cc_ref, cpow_ref, S5_STATE, LANES, S5_GROUP)

    @pl.when(pl.program_id(0) % tiles_per_seq == 0)
    def _():
        hc[...] = jnp.zeros(hc.shape, F32)

    def half_input(hh):
        cols = [u_ref[hh, pl.ds(i, tr, stride=ls), :] for i in range(ls)]
        return jnp.concatenate(cols, axis=1).astype(BF16)

    for hh in range(2):
        st[hh] = _dot(half_input(hh), bpow_ref[hh])

    lam = [(lam_ref[hh, 0:1, :], lam_ref[hh, 1:2, :]) for hh in range(2)]

    def body(c, carry):
        new = []
        for hh in range(2):
            hr, hi = carry[2 * hh], carry[2 * hh + 1]
            lr, li = lam[hh]
            s_r = st[hh, pl.ds(c, 1), 0:hs]
            s_i = st[hh, pl.ds(c, 1), hs:2 * hs]
            st[hh, pl.ds(c, 1), 0:hs] = hr
            st[hh, pl.ds(c, 1), hs:2 * hs] = hi
            new.append(lr * hr - li * hi + s_r)
            new.append(lr * hi + li * hr + s_i)
        return tuple(new)

    init = tuple(hc[hh, :, ri * hs:(ri + 1) * hs] for hh in range(2) for ri in range(2))
    fin = lax.fori_loop(0, tr, body, init)
    for hh in range(2):
        for ri in range(2):
            hc[hh, :, ri * hs:(ri + 1) * hs] = fin[2 * hh + ri]

    for hh in range(2):
        y = _dot(half_input(hh), toep_ref[hh]) + _dot(st[hh].astype(BF16), cpow_ref[hh])
        for i in range(ls):
            y_ref[hh, pl.ds(i, tr, stride=ls), :] = y[:, i * LANES:(i + 1) * LANES]


def _s5_scan(u5, tc, bc, cc, lam_l, layer, ls, nc, tr):
    nh, t, w = u5.shape
    rows = tr * ls
    wide = ls * LANES
    kern = functools.partial(_s5_kernel, ls=ls, tiles_per_seq=nc // tr)
    row = lambda i: (0, i, 0)
    return pl.pallas_call(
        kern,
        grid=(t // rows,),
        in_specs=[
            pl.BlockSpec((nh, rows, w), row),
            _layer_spec(tc.shape, layer),
            _layer_spec(bc.shape, layer),
            _layer_spec(cc.shape, layer),
            _layer_spec(lam_l.shape, layer),
        ],
        out_specs=pl.BlockSpec((nh, rows, w), row),
        out_shape=jax.ShapeDtypeStruct((nh, t, w), F32),
        scratch_shapes=[
            pltpu.VMEM((nh, wide, wide), BF16),
            pltpu.VMEM((nh, wide, 2 * S5_HSTATE), BF16),
            pltpu.VMEM((nh, 2 * S5_HSTATE, wide), BF16),
            pltpu.VMEM((nh, tr, 2 * S5_HSTATE), F32),
            pltpu.VMEM((nh, 1, 2 * S5_HSTATE), F32),
        ],
        compiler_params=pltpu.CompilerParams(
            dimension_semantics=("arbitrary",), vmem_limit_bytes=VMEM_LIMIT),
        name="s5_scan",
    )(u5, tc, bc, cc, lam_l)


def _s5_operators(lam_re, lam_im, log_dt, b_re, b_im, c_re, c_im, ls):
    hp = lax.Precision.HIGHEST
    dt = jnp.exp(log_dt)[:, None]
    zr, zi = lam_re * dt, lam_im * dt
    er = jnp.exp(zr)
    lbr, lbi = er * jnp.cos(zi), er * jnp.sin(zi)
    den = lam_re * lam_re + lam_im * lam_im
    nr, ni = lbr - 1.0, lbi
    fr = (nr * lam_re + ni * lam_im) / den
    fi = (ni * lam_re - nr * lam_im) / den
    bbr = fr[..., None] * b_re - fi[..., None] * b_im
    bbi = fr[..., None] * b_im + fi[..., None] * b_re
    tau = jnp.arange(ls + 1, dtype=F32)
    pe = jnp.exp(zr[..., None] * tau)
    pr, pi_ = pe * jnp.cos(zi[..., None] * tau), pe * jnp.sin(zi[..., None] * tau)
    cpr = c_re[:, :, :, None] * pr[:, None, :, :ls] - c_im[:, :, :, None] * pi_[:, None, :, :ls]
    cpi = c_re[:, :, :, None] * pi_[:, None, :, :ls] + c_im[:, :, :, None] * pr[:, None, :, :ls]
    kk = (jnp.einsum('ghpt,gpk->gthk', cpr, bbr, precision=hp)
          - jnp.einsum('ghpt,gpk->gthk', cpi, bbi, precision=hp))
    lag = jnp.arange(ls)[None, :] - jnp.arange(ls)[:, None]
    kt = kk[:, jnp.clip(lag, 0, ls - 1)]
    kt = jnp.where((lag >= 0)[None, :, :, None, None], kt, 0.0)
    nh, gh = S5_GROUPS // S5_HALF, S5_HALF
    w = ls * LANES
    kt = kt.reshape(nh, gh, ls, ls, S5_GROUP, S5_GROUP)
    tc = kt.transpose(0, 2, 1, 5, 3, 4).reshape(nh, w, ls * S5_GROUP)
    tau_rev = (ls - 1) - jnp.arange(ls, dtype=F32)
    pe_rev = jnp.exp(zr[..., None] * tau_rev)
    rev_r, rev_i = pe_rev * jnp.cos(zi[..., None] * tau_rev), pe_rev * jnp.sin(zi[..., None] * tau_rev)
    bre = rev_r[..., None] * bbr[:, :, None, :] - rev_i[..., None] * bbi[:, :, None, :]
    bim = rev_r[..., None] * bbi[:, :, None, :] + rev_i[..., None] * bbr[:, :, None, :]
    bb = jnp.stack([bre, bim], axis=0).reshape(2, nh, gh, S5_STATE, ls, S5_GROUP)
    bc = bb.transpose(1, 4, 2, 5, 0, 3).reshape(nh, w, 2 * S5_STATE)
    qr = c_re[:, :, :, None] * pr[:, None, :, 1:] - c_im[:, :, :, None] * pi_[:, None, :, 1:]
    qi = c_re[:, :, :, None] * pi_[:, None, :, 1:] + c_im[:, :, :, None] * pr[:, None, :, 1:]
    cc = jnp.stack([qr, -qi], axis=0).reshape(2, nh, gh, S5_GROUP, S5_STATE, ls)
    cc = cc.transpose(1, 0, 2, 4, 5, 3).reshape(nh, 2 * S5_HSTATE, ls * S5_GROUP)
    lam_l = jnp.stack([pr[:, :, ls].reshape(nh, S5_HSTATE), pi_[:, :, ls].reshape(nh, S5_HSTATE)], axis=1)
    return tc.astype(BF16), bc.astype(BF16), cc.astype(BF16), lam_l.astype(F32)


def _mix_kernel(rest_ref, u5_ref, ys5_ref, scw_ref, s5d_ref, wglu_ref, bglu_ref, mcw_ref, mcb_ref,
                dtb_ref, alog_ref, dvec_ref, ng_ref, o_ref, cbuf, mbuf, hst, ybuf, *, L):
    @pl.when(pl.program_id(1) == 0)
    def _():
        cbuf[0:HIST, :] = jnp.zeros((HIST, SC_WIDTH), F32)
        mbuf[0:HIST, :] = jnp.zeros((HIST, M2_XBC), F32)
        hst[...] = jnp.zeros(hst.shape, F32)

    u = rest_ref[:, R_SC + SC_WIDTH:R_SC + 2 * SC_WIDTH] * rest_ref[:, R_SC + 2 * SC_WIDTH:R_SC + 3 * SC_WIDTH]
    cbuf[HIST:HIST + L, :] = u
    conv = (scw_ref[2:3, :] * u + scw_ref[1:2, :] * cbuf[HIST - 1:HIST - 1 + L, :]
            + scw_ref[0:1, :] * cbuf[HIST - 2:HIST - 2 + L, :])
    cbuf[0:HIST, :] = cbuf[L:L + HIST, :]
    o_ref[:, 0:SC_WIDTH] = (rest_ref[:, R_SC:R_SC + SC_WIDTH] * conv).astype(o_ref.dtype)

    halves = lambda ref: jnp.concatenate([ref[hh] for hh in range(S5_WIDTH // LANES)], axis=1)
    y5 = halves(ys5_ref) + s5d_ref[...] * halves(u5_ref)
    gl = jax.nn.gelu(y5, approximate=True)
    gate = _dot(gl.astype(BF16), wglu_ref[...]) + bglu_ref[...]
    o_ref[:, SC_WIDTH:SC_WIDTH + S5_WIDTH] = (gl * (1.0 / (1.0 + jnp.exp(-gate)))).astype(o_ref.dtype)

    xr = rest_ref[:, R_XBC:R_XBC + M2_XBC]
    mbuf[HIST:HIST + L, :] = xr
    conv = (mcw_ref[3:4, :] * xr + mcw_ref[2:3, :] * mbuf[HIST - 1:HIST - 1 + L, :]
            + mcw_ref[1:2, :] * mbuf[HIST - 2:HIST - 2 + L, :]
            + mcw_ref[0:1, :] * mbuf[HIST - 3:HIST - 3 + L, :] + mcb_ref[...])
    mbuf[0:HIST, :] = mbuf[L:L + HIST, :]
    xbc = _silu(conv)
    xs = xbc[:, 0:M2_WIDTH]
    dtr = rest_ref[:, R_DT:R_DT + LANES] + dtb_ref[...]
    dtv = jnp.maximum(dtr, 0.0) + jnp.log(1.0 + jnp.exp(-jnp.abs(dtr)))
    a = -jnp.exp(alog_ref[...]) * dtv
    r_i = lax.broadcasted_iota(jnp.int32, (L, L), 0)
    c_i = lax.broadcasted_iota(jnp.int32, (L, L), 1)
    tril = c_i <= r_i
    tri = jnp.where(tril, 1.0, 0.0).astype(BF16)
    cs = sum(_dot(tri, part) for part in _split_bf16(a, 3))
    cs_t = cs.T
    cs_last = cs[L - 1:L, :]
    ecs = jnp.exp(cs)
    dec = jnp.exp(cs_last - cs)
    ecl = jnp.exp(cs_last)
    lane = lax.broadcasted_iota(jnp.int32, (1, LANES), 1)
    for g in range(M2_GROUPS):
        bg = xbc[:, M2_WIDTH + g * M2_STATE:M2_WIDTH + (g + 1) * M2_STATE].astype(BF16)
        cg = xbc[:, M2_WIDTH + (M2_GROUPS + g) * M2_STATE:M2_WIDTH + (M2_GROUPS + g + 1) * M2_STATE].astype(BF16)
        gram = _dot_nt(cg, bg)
        hs = hst[g]
        yoff = _dot(cg, hs.astype(BF16))
        xdec = []
        for hh in range(2):
            h = 2 * g + hh
            seg = cs[:, h:h + 1] - cs_t[h:h + 1, :]
            dm = jnp.exp(jnp.where(tril, seg, -jnp.inf))
            xdt = xs[:, h * M2_HEAD_DIM:(h + 1) * M2_HEAD_DIM] * dtv[:, h:h + 1]
            yd = _dot((gram * dm).astype(BF16), xdt.astype(BF16))
            yo = yoff[:, hh * M2_HEAD_DIM:(hh + 1) * M2_HEAD_DIM] * ecs[:, h:h + 1]
            ybuf[:, h * M2_HEAD_DIM:(h + 1) * M2_HEAD_DIM] = yd + yo
            xdec.append(xdt * dec[:, h:h + 1])
        upd = _dot_tn(bg, jnp.concatenate(xdec, axis=1).astype(BF16))
        keep = jnp.where(lane < M2_HEAD_DIM, ecl[:, 2 * g:2 * g + 1], ecl[:, 2 * g + 1:2 * g + 2])
        hst[g] = hs * keep + upd
    y = ybuf[...] + dvec_ref[...] * xs
    yg = y * _silu(rest_ref[:, R_Z:R_Z + M2_WIDTH])
    ms = jnp.mean(yg * yg, axis=-1, keepdims=True)
    o_ref[:, SC_WIDTH + S5_WIDTH:] = (yg * lax.rsqrt(ms + EPS) * ng_ref[...]).astype(o_ref.dtype)


def _mixers(rest, u5, ys5, scw, s5d, wglu, bglu, mcw, mcb, dtb, alog, dvec, ng, L):
    bsz, seq, _ = rest.shape
    kern = functools.partial(_mix_kernel, L=L)
    blk = lambda b, c: (b, c, 0)
    wout = SC_WIDTH + S5_WIDTH + M2_WIDTH
    return pl.pallas_call(
        kern,
        grid=(bsz, seq // L),
        in_specs=[
            pl.BlockSpec((None, L, D_REST), blk),
            pl.BlockSpec((S5_WIDTH // LANES, L, LANES), lambda b, c: (0, b * (seq // L) + c, 0)),
            pl.BlockSpec((S5_WIDTH // LANES, L, LANES), lambda b, c: (0, b * (seq // L) + c, 0)),
            _const_spec((3, SC_WIDTH)),
            _const_spec((1, S5_WIDTH)),
            _const_spec((S5_WIDTH, S5_WIDTH)),
            _const_spec((1, S5_WIDTH)),
            _const_spec((4, M2_XBC)),
            _const_spec((1, M2_XBC)),
            _const_spec((1, LANES)),
            _const_spec((1, LANES)),
            _const_spec((1, M2_WIDTH)),
            _const_spec((1, M2_WIDTH)),
        ],
        out_specs=pl.BlockSpec((None, L, wout), blk),
        out_shape=jax.ShapeDtypeStruct((bsz, seq, wout), BF16),
        scratch_shapes=[
            pltpu.VMEM((L + HIST, SC_WIDTH), F32),
            pltpu.VMEM((L + HIST, M2_XBC), F32),
            pltpu.VMEM((M2_GROUPS, M2_STATE, 2 * M2_HEAD_DIM), F32),
            pltpu.VMEM((L, M2_WIDTH), F32),
        ],
        compiler_params=pltpu.CompilerParams(
            dimension_semantics=("arbitrary", "arbitrary"), vmem_limit_bytes=VMEM_LIMIT),
        name="mixers",
    )(rest, u5, ys5, scw, s5d, wglu, bglu, mcw, mcb, dtb, alog, dvec, ng)


def _ffn_kernel(x_ref, oa_ref, ob_ref, wo_ref, g2_ref, wg_ref, wu_ref, cw_ref, wd_ref,
                o_ref, hist, cb, *, tiles_per_seq, tf):
    tm = x_ref.shape[0]

    @pl.when(pl.program_id(0) % tiles_per_seq == 0)
    def _():
        hist[...] = jnp.zeros(hist.shape, F32)

    x1 = (x_ref[...] + _dot(oa_ref[...], wo_ref[0:DA_WIDTH, :])
          + _dot(ob_ref[...], wo_ref[DA_WIDTH:D_MIX, :]))
    ms = jnp.mean(x1 * x1, axis=-1, keepdims=True)
    h2 = (x1 * lax.rsqrt(ms + EPS) * g2_ref[...]).astype(BF16)
    o_ref[...] = x1
    for c in range(D_FF // tf):
        sl = slice(c * tf, (c + 1) * tf)
        gpre = _dot(h2, wg_ref[:, sl])
        cb[0:HIST, :] = hist[:, sl]
        cb[HIST:HIST + tm, :] = gpre
        gc = (cw_ref[2:3, sl] * gpre + cw_ref[1:2, sl] * cb[HIST - 1:HIST - 1 + tm, :]
              + cw_ref[0:1, sl] * cb[HIST - 2:HIST - 2 + tm, :])
        hist[:, sl] = cb[tm:tm + HIST, :]
        act = (_silu(gc) * _dot(h2, wu_ref[:, sl])).astype(BF16)
        o_ref[...] += _dot(act, wd_ref[sl, :])


def _outproj_ffn(x2, oa, ob, wo, g2, wg, wu, cw, wd, layer, seq, tm, tf):
    t = x2.shape[0]
    wb = ob.shape[1]
    kern = functools.partial(_ffn_kernel, tiles_per_seq=seq // tm, tf=tf)
    row = lambda i: (i, 0)
    return pl.pallas_call(
        kern,
        grid=(t // tm,),
        in_specs=[
            pl.BlockSpec((tm, D_MODEL), row),
            pl.BlockSpec((tm, DA_WIDTH), row),
            pl.BlockSpec((tm, wb), row),
            _layer_spec(wo.shape, layer),
            _const_spec((1, D_MODEL)),
            _layer_spec(wg.shape, layer),
            _layer_spec(wu.shape, layer),
            _const_spec((3, D_FF)),
            _layer_spec(wd.shape, layer),
        ],
        out_specs=pl.BlockSpec((tm, D_MODEL), row),
        out_shape=jax.ShapeDtypeStruct((t, D_MODEL), F32),
        scratch_shapes=[
            pltpu.VMEM((HIST, D_FF), F32),
            pltpu.VMEM((tm + HIST, tf), F32),
        ],
        compiler_params=pltpu.CompilerParams(
            dimension_semantics=("arbitrary",), vmem_limit_bytes=VMEM_LIMIT),
        name="outproj_ffn",
    )(x2, oa, ob, wo, g2, wg, wu, cw, wd)


def _rope_lane_tables(seq):
    inv = 1.0 / (ROPE_THETA ** (jnp.arange(0, ROPE_DIM, 2, dtype=F32) / ROPE_DIM))
    ang = jnp.arange(seq, dtype=F32)[:, None] * inv[None, :]
    cos, sin = jnp.cos(ang), jnp.sin(ang)
    ones = jnp.ones((seq, DA_QK - ROPE_DIM), F32)
    cos64 = jnp.concatenate([cos, cos, ones], axis=1)
    sin64 = jnp.concatenate([-sin, sin, 0.0 * ones], axis=1)
    return jnp.tile(cos64, (1, LANES // DA_QK)), jnp.tile(sin64, (1, LANES // DA_QK))


def _forward(x, p, blk, mix_l, s5_tr):
    bsz, seq, _ = x.shape
    depth = p["w_in"].shape[0]
    t = bsz * seq
    ls = S5_CHUNK
    nc = seq // ls
    assert seq % blk == 0 and seq % mix_l == 0 and nc % s5_tr == 0

    cos_t, sin_t = _rope_lane_tables(seq)
    comp = lax.broadcasted_iota(jnp.int32, (QK_COLS, QK_COLS), 0) // DA_QK
    gm = (comp == comp.T).astype(BF16)
    w_in = p["w_in"].astype(BF16)
    w_dt = jnp.pad(p["w_in"][:, :, W_DT:], ((0, 0), (0, 0), (0, LANES - M2_HEADS))).astype(BF16)
    w_out = p["w_out"].astype(BF16)
    w_gate, w_up, w_down = (p[k].astype(BF16) for k in ("ffn_w_gate", "ffn_w_up", "ffn_w_down"))
    s5_ops = jax.vmap(functools.partial(_s5_operators, ls=ls))(
        p["s5_lam_re"], p["s5_lam_im"], p["s5_log_dt"], p["s5_b_re"], p["s5_b_im"],
        p["s5_c_re"], p["s5_c_im"])
    x2 = x.reshape(t, D_MODEL)
    for l in range(depth):
        lambda_init = 0.8 - 0.6 * math.exp(-0.3 * l)
        qkg = jnp.tile(p["qk_norm_g"][l], (1, LANES // DA_QK))
        qt, kb, vtb, rest, u5 = _inproj(x2, p["ln1_g"][l][None, :], w_in, w_dt, l, qkg, cos_t, sin_t,
                                        gm, bsz, seq, blk)
        oa = _attention(qt, kb, vtb, p["da_lambda"][l], p["subln_g"][l][None, :], lambda_init)
        ys5 = _s5_scan(u5, *s5_ops, l, ls, nc, s5_tr)

        pad4 = lambda a: jnp.pad(a, (0, LANES - M2_HEADS))[None, :]
        ob = _mixers(rest.reshape(bsz, seq, D_REST), u5, ys5, p["sc_conv_w"][l], p["s5_d"][l][None, :],
                     p["s5_w_glu"][l].astype(BF16), p["s5_b_glu"][l][None, :], p["m2_conv_w"][l],
                     p["m2_conv_b"][l][None, :], pad4(p["m2_dt_bias"][l]), pad4(p["m2_a_log"][l]),
                     jnp.repeat(p["m2_d"][l], M2_HEAD_DIM)[None, :], p["m2_norm_g"][l][None, :], mix_l)

        x2 = _outproj_ffn(x2, oa.reshape(t, DA_WIDTH), ob.reshape(t, D_MIX - DA_WIDTH), w_out,
                          p["ln2_g"][l][None, :], w_gate, w_up, p["ffn_conv_w"][l], w_down,
                          l, seq, min(FFN_ROWS, seq), 256)
    return x2.reshape(bsz, seq, D_MODEL)


def kernel(x, ln1_g, w_in, qk_norm_g, da_lambda, subln_g, sc_conv_w, s5_lam_re, s5_lam_im, s5_log_dt, s5_b_re, s5_b_im, s5_c_re, s5_c_im, s5_d, s5_w_glu, s5_b_glu, m2_conv_w, m2_conv_b, m2_dt_bias, m2_a_log, m2_d, m2_norm_g, w_out, ln2_g, ffn_w_gate, ffn_w_up, ffn_conv_w, ffn_w_down):
    params = dict(ln1_g=ln1_g, w_in=w_in, qk_norm_g=qk_norm_g, da_lambda=da_lambda, subln_g=subln_g,
                  sc_conv_w=sc_conv_w, s5_lam_re=s5_lam_re, s5_lam_im=s5_lam_im, s5_log_dt=s5_log_dt,
                  s5_b_re=s5_b_re, s5_b_im=s5_b_im, s5_c_re=s5_c_re, s5_c_im=s5_c_im, s5_d=s5_d,
                  s5_w_glu=s5_w_glu, s5_b_glu=s5_b_glu, m2_conv_w=m2_conv_w, m2_conv_b=m2_conv_b,
                  m2_dt_bias=m2_dt_bias, m2_a_log=m2_a_log, m2_d=m2_d, m2_norm_g=m2_norm_g,
                  w_out=w_out, ln2_g=ln2_g, ffn_w_gate=ffn_w_gate, ffn_w_up=ffn_w_up,
                  ffn_conv_w=ffn_conv_w, ffn_w_down=ffn_w_down)
    seq = x.shape[1]
    return _forward(x, params, blk=min(512, seq), mix_l=min(256, seq),
                    s5_tr=min(256, seq // S5_CHUNK))
```

```python
import functools
import math

import jax
import jax.numpy as jnp
from jax import lax
from jax.experimental import pallas as pl
from jax.experimental.pallas import tpu as pltpu

F32 = jnp.float32
BF16 = jnp.bfloat16

D_MODEL = 1024
DA_HEADS = 4
DA_QK = 64
DA_V = 2 * DA_QK
DA_WIDTH = DA_HEADS * DA_V
ROPE_DIM = DA_QK // 4
ROPE_THETA = 500000.0
SC_WIDTH = 256
S5_WIDTH = 256
S5_GROUP = 16
S5_GROUPS = S5_WIDTH // S5_GROUP
S5_STATE = 64
M2_HEADS = 4
M2_HEAD_DIM = 64
M2_WIDTH = M2_HEADS * M2_HEAD_DIM
M2_GROUPS = 2
M2_STATE = 128
M2_XBC = M2_WIDTH + 2 * M2_GROUPS * M2_STATE
D_MIX = DA_WIDTH + SC_WIDTH + S5_WIDTH + M2_WIDTH
D_FF = 2816
EPS = 1e-6
LOG2E = 1.4426950408889634

LANES = 128
HIST = 8
VMEM_LIMIT = 56 * 1024 * 1024

W_Q = 0
W_K = W_Q + DA_WIDTH
W_V = W_K + DA_WIDTH
W_SC = W_V + DA_WIDTH
W_S5 = W_SC + 3 * SC_WIDTH
W_Z = W_S5 + S5_WIDTH
W_DT = W_Z + M2_WIDTH + M2_XBC
D_PROJ = W_DT + M2_HEADS
R_SC = 0
R_Z = R_SC + 3 * SC_WIDTH
R_XBC = R_Z + M2_WIDTH
R_DT = R_XBC + M2_XBC
D_REST = R_DT + LANES
QK_COLS = 2 * LANES
ATT_HEADS_PER_STEP = 2
FFN_ROWS = 512
FFN_COLS = 2816

S5_CHUNK = 8
S5_HALF = LANES // S5_GROUP
S5_HSTATE = S5_HALF * S5_STATE


def _dot(a, b):
    return jnp.dot(a, b, preferred_element_type=F32)


def _dot_nt(a, b):
    return lax.dot_general(a, b, (((1,), (1,)), ((), ())), preferred_element_type=F32)


def _dot_tn(a, b):
    return lax.dot_general(a, b, (((0,), (0,)), ((), ())), preferred_element_type=F32)


def _split_bf16(x, parts):
    out = []
    r = x
    for _ in range(parts):
        p = r.astype(BF16)
        out.append(p)
        r = r - p.astype(F32)
    return out


def _silu(x):
    return x * (1.0 / (1.0 + jnp.exp(-x)))


def _const_spec(shape):
    nd = len(shape)
    return pl.BlockSpec(shape, lambda *_: (0,) * nd, pipeline_mode=pl.Buffered(1))


def _layer_spec(shape, layer):
    nd = len(shape)
    return pl.BlockSpec((None,) + tuple(shape[1:]), lambda *_: (layer,) + (0,) * (nd - 1),
                        pipeline_mode=pl.Buffered(1))


def _inproj_kernel(x_ref, g1_ref, w_ref, wdt_ref, qkg_ref, cos_ref, sin_ref, gm_ref,
                   qt_ref, kb_ref, vt_ref, rest_ref, u5_ref):
    tm = x_ref.shape[0]
    x = x_ref[...]
    ms = jnp.mean(x * x, axis=-1, keepdims=True)
    hn = (x * lax.rsqrt(ms + EPS) * g1_ref[...]).astype(BF16)
    cosv = cos_ref[...]
    sinv = sin_ref[...]
    gm = gm_ref[...]
    lane = lax.broadcasted_iota(jnp.int32, (tm, LANES), 1)
    pair_up = (lane % DA_QK) < (ROPE_DIM // 2)
    heads_per_dot = QK_COLS // DA_V

    def qk_heads(col0, gvec, scale):
        y = _dot(hn, w_ref[:, col0:col0 + QK_COLS])
        hi, lo = _split_bf16(y * y, 2)
        ss = _dot(hi, gm) + _dot(lo, gm)
        yn = y * lax.rsqrt(ss * (1.0 / DA_QK) + EPS)
        out = []
        for j in range(heads_per_dot):
            yh = yn[:, j * DA_V:(j + 1) * DA_V] * gvec
            partner = jnp.where(pair_up,
                                pltpu.roll(yh, LANES - ROPE_DIM // 2, 1),
                                pltpu.roll(yh, ROPE_DIM // 2, 1))
            out.append((yh * cosv + partner * sinv) * scale)
        return out

    for h0 in range(0, DA_HEADS, heads_per_dot):
        qs = qk_heads(W_Q + h0 * DA_V, qkg_ref[0:1, :], DA_QK ** -0.5 * LOG2E)
        ks = qk_heads(W_K + h0 * DA_V, qkg_ref[1:2, :], 1.0)
        v = _dot(hn, w_ref[:, W_V + h0 * DA_V:W_V + h0 * DA_V + QK_COLS])
        for j in range(heads_per_dot):
            qt_ref[h0 + j] = qs[j].T.astype(BF16)
            kb_ref[h0 + j] = ks[j].astype(BF16)
            vt_ref[h0 + j] = v[:, j * DA_V:(j + 1) * DA_V].T.astype(BF16)
    rest_ref[:, R_SC:R_Z] = _dot(hn, w_ref[:, W_SC:W_S5])
    rest_ref[:, R_Z:R_DT] = _dot(hn, w_ref[:, W_Z:W_DT])
    rest_ref[:, R_DT:D_REST] = _dot(hn, wdt_ref[...])
    u5 = _dot(hn, w_ref[:, W_S5:W_Z])
    for hh in range(S5_WIDTH // LANES):
        u5_ref[hh] = u5[:, hh * LANES:(hh + 1) * LANES]


def _inproj(x2, g1, w, wdt, layer, qkg, cos_t, sin_t, gm, bsz, seq, tm):
    t = x2.shape[0]
    nt = seq // tm
    row = lambda i: (i, 0)
    tab = lambda i: (i % nt, 0)
    return pl.pallas_call(
        _inproj_kernel,
        grid=(t // tm,),
        in_specs=[
            pl.BlockSpec((tm, D_MODEL), row),
            _const_spec((1, D_MODEL)),
            _layer_spec(w.shape, layer),
            _layer_spec(wdt.shape, layer),
            _const_spec((2, LANES)),
            pl.BlockSpec((tm, LANES), tab),
            pl.BlockSpec((tm, LANES), tab),
            _const_spec((QK_COLS, QK_COLS)),
        ],
        out_specs=[
            pl.BlockSpec((None, DA_HEADS, DA_V, tm), lambda i: (i // nt, 0, 0, i % nt)),
            pl.BlockSpec((None, DA_HEADS, None, tm, DA_V), lambda i: (i // nt, 0, i % nt, 0, 0)),
            pl.BlockSpec((None, DA_HEADS, None, DA_V, tm), lambda i: (i // nt, 0, i % nt, 0, 0)),
            pl.BlockSpec((tm, D_REST), row),
            pl.BlockSpec((S5_WIDTH // LANES, tm, LANES), lambda i: (0, i, 0)),
        ],
        out_shape=[
            jax.ShapeDtypeStruct((bsz, DA_HEADS, DA_V, seq), BF16),
            jax.ShapeDtypeStruct((bsz, DA_HEADS, nt, tm, DA_V), BF16),
            jax.ShapeDtypeStruct((bsz, DA_HEADS, nt, DA_V, tm), BF16),
            jax.ShapeDtypeStruct((t, D_REST), F32),
            jax.ShapeDtypeStruct((S5_WIDTH // LANES, t, LANES), F32),
        ],
        compiler_params=pltpu.CompilerParams(
            dimension_semantics=("arbitrary",), vmem_limit_bytes=VMEM_LIMIT),
        name="inproj",
    )(x2, g1, w, wdt, qkg, cos_t, sin_t, gm)


def _attn_kernel(qt_ref, k_ref, vt_ref, lam_ref, sg_ref, o_ref, qz_sc, s_sc, m_sc, l_sc, acc_sc,
                 *, lambda_init):
    qi = pl.program_id(2)
    heads = qt_ref.shape[0]
    streams = [(hd, c) for hd in range(heads) for c in range(2)]
    for n, (hd, c) in enumerate(streams):
        qt = qt_ref[hd]
        comp = lax.broadcasted_iota(jnp.int32, qt.shape, 0) // DA_QK
        qz_sc[n] = jnp.where(comp == c, qt, jnp.zeros_like(qt))
    m_sc[...] = jnp.full(m_sc.shape, -1e30, F32)
    l_sc[...] = jnp.zeros(l_sc.shape, F32)
    acc_sc[...] = jnp.zeros(acc_sc.shape, F32)

    def scores(j, buf):
        for n, (hd, c) in enumerate(streams):
            s_sc[buf, n] = _dot(k_ref[hd, j], qz_sc[n])

    def absorb(j, buf, masked):
        for n, (hd, c) in enumerate(streams):
            s = s_sc[buf, n]
            if masked:
                kpos = lax.broadcasted_iota(jnp.int32, s.shape, 0)
                qpos = lax.broadcasted_iota(jnp.int32, s.shape, 1)
                s = jnp.where(kpos <= qpos, s, -jnp.inf)
            m_prev = m_sc[n]
            m_new = jnp.maximum(m_prev, jnp.max(s, axis=0, keepdims=True))
            alpha = jnp.exp2(m_prev - m_new)
            p = jnp.exp2(s - m_new)
            l_sc[n] = alpha * l_sc[n] + jnp.sum(p, axis=0, keepdims=True)
            acc_sc[n] = alpha * acc_sc[n] + _dot(vt_ref[hd, j], p.astype(BF16))
            m_sc[n] = m_new

    scores(0, 0)
    pairs = qi // 2

    def body(t, carry):
        j = 2 * t
        scores(j + 1, 1)
        absorb(j, 0, False)
        scores(j + 2, 0)
        absorb(j + 1, 1, False)
        return carry

    lax.fori_loop(0, pairs, body, 0)
    j0 = 2 * pairs

    @pl.when(j0 == qi)
    def _():
        absorb(qi, 0, True)

    @pl.when(j0 != qi)
    def _():
        scores(qi, 1)
        absorb(j0, 0, False)
        absorb(qi, 1, True)

    lp = lam_ref[...]
    lam = (jnp.exp(jnp.sum(lp[0:1] * lp[1:2], axis=1, keepdims=True))
           - jnp.exp(jnp.sum(lp[2:3] * lp[3:4], axis=1, keepdims=True)) + lambda_init)
    for hd in range(heads):
        ot = acc_sc[2 * hd] / l_sc[2 * hd] - lam * (acc_sc[2 * hd + 1] / l_sc[2 * hd + 1])
        ms = jnp.mean(ot * ot, axis=0, keepdims=True)
        o = (ot * lax.rsqrt(ms + EPS)).T * (sg_ref[...] * (1.0 - lambda_init))
        o_ref[:, hd * DA_V:(hd + 1) * DA_V] = o.astype(o_ref.dtype)


def _attention(qt, kb, vtb, lam_p, sub_g, lambda_init):
    bsz, _, _, seq = qt.shape
    nb, blk = kb.shape[2], kb.shape[3]
    hps = ATT_HEADS_PER_STEP
    kern = functools.partial(_attn_kernel, lambda_init=lambda_init)
    return pl.pallas_call(
        kern,
        grid=(bsz, DA_HEADS // hps, nb),
        in_specs=[
            pl.BlockSpec((None, hps, DA_V, blk), lambda b, h, i: (b, h, 0, i)),
            pl.BlockSpec((None, hps, nb, blk, DA_V), lambda b, h, i: (b, h, 0, 0, 0)),
            pl.BlockSpec((None, hps, nb, DA_V, blk), lambda b, h, i: (b, h, 0, 0, 0)),
            pl.BlockSpec((4, DA_QK), lambda b, h, i: (0, 0)),
            pl.BlockSpec((1, DA_V), lambda b, h, i: (0, 0)),
        ],
        out_specs=pl.BlockSpec((None, blk, hps * DA_V), lambda b, h, i: (b, i, h)),
        out_shape=jax.ShapeDtypeStruct((bsz, seq, DA_WIDTH), BF16),
        scratch_shapes=[
            pltpu.VMEM((2 * hps, DA_V, blk), BF16),
            pltpu.VMEM((2, 2 * hps, blk, blk), F32),
            pltpu.VMEM((2 * hps, 1, blk), F32),
            pltpu.VMEM((2 * hps, 1, blk), F32),
            pltpu.VMEM((2 * hps, DA_V, blk), F32),
        ],
        compiler_params=pltpu.CompilerParams(
            dimension_semantics=("arbitrary", "arbitrary", "arbitrary"),
            vmem_limit_bytes=VMEM_LIMIT),
        name="diff_attn",
    )(qt, kb, vtb, lam_p, sub_g)


def _expand_blockdiag(src_ref, dst_ref, row_group, col_div, col_inner):
    rows, wide = dst_ref.shape[1], dst_ref.shape[2]
    r_s = lax.broadcasted_iota(jnp.int32, (LANES, wide), 0)
    c_s = lax.broadcasted_iota(jnp.int32, (LANES, wide), 1)
    sel = jnp.where(r_s == (c_s // col_div) * col_inner + c_s % col_inner, 1.0, 0.0).astype(BF16)
    step = 256
    r_i = lax.broadcasted_iota(jnp.int32, (step, wide), 0)
    c_i = lax.broadcasted_iota(jnp.int32, (step, wide), 1)
    col_group = (c_i // (col_div // S5_HALF)) % S5_HALF
    for hh in range(dst_ref.shape[0]):
        for r0 in range(0, rows, step):
            same = ((r_i + r0) // row_group) % S5_HALF == col_group
            wide_blk = _dot(src_ref[hh, r0:r0 + step, :], sel)
            dst_ref[hh, r0:r0 + step, :] = jnp.where(same, wide_blk, 0.0).astype(BF16)


def _s5_kernel(u_ref, tc_ref, bc_ref, cc_ref, lam_ref, y_ref, toep_ref, bpow_ref, cpow_ref, st, hc,
               *, ls, tiles_per_seq):
    tr = u_ref.shape[1] // ls
    hs = S5_HSTATE

    @pl.when(pl.program_id(0) == 0)
    def _():
        _expand_blockdiag(tc_ref, toep_ref, S5_GROUP, LANES, S5_GROUP)
        _expand_blockdiag(bc_ref, bpow_ref, S5_GROUP, S5_HSTATE, S5_STATE)
        _expand_blockdiag(cc_ref, cpow_ref, S5_STATE, LANES, S5_GROUP)

    @pl.when(pl.program_id(0) % tiles_per_seq == 0)
    def _():
        hc[...] = jnp.zeros(hc.shape, F32)

    def half_input(hh):
        cols = [u_ref[hh, pl.ds(i, tr, stride=ls), :] for i in range(ls)]
        return jnp.concatenate(cols, axis=1).astype(BF16)

    for hh in range(2):
        st[hh] = _dot(half_input(hh), bpow_ref[hh])

    lam = [(lam_ref[hh, 0:1, :], lam_ref[hh, 1:2, :]) for hh in range(2)]

    def body(c, carry):
        new = []
        for hh in range(2):
            hr, hi = carry[2 * hh], carry[2 * hh + 1]
            lr, li = lam[hh]
            s_r = st[hh, pl.ds(c, 1), 0:hs]
            s_i = st[hh, pl.ds(c, 1), hs:2 * hs]
            st[hh, pl.ds(c, 1), 0:hs] = hr
            st[hh, pl.ds(c, 1), hs:2 * hs] = hi
            new.append(lr * hr - li * hi + s_r)
            new.append(lr * hi + li * hr + s_i)
        return tuple(new)

    init = tuple(hc[hh, :, ri * hs:(ri + 1) * hs] for hh in range(2) for ri in range(2))
    fin = lax.fori_loop(0, tr, body, init)
    for hh in range(2):
        for ri in range(2):
            hc[hh, :, ri * hs:(ri + 1) * hs] = fin[2 * hh + ri]

    for hh in range(2):
        y = _dot(half_input(hh), toep_ref[hh]) + _dot(st[hh].astype(BF16), cpow_ref[hh])
        for i in range(ls):
            y_ref[hh, pl.ds(i, tr, stride=ls), :] = y[:, i * LANES:(i + 1) * LANES]


def _s5_scan(u5, tc, bc, cc, lam_l, layer, ls, nc, tr):
    nh, t, w = u5.shape
    rows = tr * ls
    wide = ls * LANES
    kern = functools.partial(_s5_kernel, ls=ls, tiles_per_seq=nc // tr)
    row = lambda i: (0, i, 0)
    return pl.pallas_call(
        kern,
        grid=(t // rows,),
        in_specs=[
            pl.BlockSpec((nh, rows, w), row),
            _layer_spec(tc.shape, layer),
            _layer_spec(bc.shape, layer),
            _layer_spec(cc.shape, layer),
            _layer_spec(lam_l.shape, layer),
        ],
        out_specs=pl.BlockSpec((nh, rows, w), row),
        out_shape=jax.ShapeDtypeStruct((nh, t, w), F32),
        scratch_shapes=[
            pltpu.VMEM((nh, wide, wide), BF16),
            pltpu.VMEM((nh, wide, 2 * S5_HSTATE), BF16),
            pltpu.VMEM((nh, 2 * S5_HSTATE, wide), BF16),
            pltpu.VMEM((nh, tr, 2 * S5_HSTATE), F32),
            pltpu.VMEM((nh, 1, 2 * S5_HSTATE), F32),
        ],
        compiler_params=pltpu.CompilerParams(
            dimension_semantics=("arbitrary",), vmem_limit_bytes=VMEM_LIMIT),
        name="s5_scan",
    )(u5, tc, bc, cc, lam_l)


def _s5_operators(lam_re, lam_im, log_dt, b_re, b_im, c_re, c_im, ls):
    hp = lax.Precision.HIGHEST
    dt = jnp.exp(log_dt)[:, None]
    zr, zi = lam_re * dt, lam_im * dt
    er = jnp.exp(zr)
    lbr, lbi = er * jnp.cos(zi), er * jnp.sin(zi)
    den = lam_re * lam_re + lam_im * lam_im
    nr, ni = lbr - 1.0, lbi
    fr = (nr * lam_re + ni * lam_im) / den
    fi = (ni * lam_re - nr * lam_im) / den
    bbr = fr[..., None] * b_re - fi[..., None] * b_im
    bbi = fr[..., None] * b_im + fi[..., None] * b_re
    tau = jnp.arange(ls + 1, dtype=F32)
    pe = jnp.exp(zr[..., None] * tau)
    pr, pi_ = pe * jnp.cos(zi[..., None] * tau), pe * jnp.sin(zi[..., None] * tau)
    cpr = c_re[:, :, :, None] * pr[:, None, :, :ls] - c_im[:, :, :, None] * pi_[:, None, :, :ls]
    cpi = c_re[:, :, :, None] * pi_[:, None, :, :ls] + c_im[:, :, :, None] * pr[:, None, :, :ls]
    kk = (jnp.einsum('ghpt,gpk->gthk', cpr, bbr, precision=hp)
          - jnp.einsum('ghpt,gpk->gthk', cpi, bbi, precision=hp))
    lag = jnp.arange(ls)[None, :] - jnp.arange(ls)[:, None]
    kt = kk[:, jnp.clip(lag, 0, ls - 1)]
    kt = jnp.where((lag >= 0)[None, :, :, None, None], kt, 0.0)
    nh, gh = S5_GROUPS // S5_HALF, S5_HALF
    w = ls * LANES
    kt = kt.reshape(nh, gh, ls, ls, S5_GROUP, S5_GROUP)
    tc = kt.transpose(0, 2, 1, 5, 3, 4).reshape(nh, w, ls * S5_GROUP)
    tau_rev = (ls - 1) - jnp.arange(ls, dtype=F32)
    pe_rev = jnp.exp(zr[..., None] * tau_rev)
    rev_r, rev_i = pe_rev * jnp.cos(zi[..., None] * tau_rev), pe_rev * jnp.sin(zi[..., None] * tau_rev)
    bre = rev_r[..., None] * bbr[:, :, None, :] - rev_i[..., None] * bbi[:, :, None, :]
    bim = rev_r[..., None] * bbi[:, :, None, :] + rev_i[..., None] * bbr[:, :, None, :]
    bb = jnp.stack([bre, bim], axis=0).reshape(2, nh, gh, S5_STATE, ls, S5_GROUP)
    bc = bb.transpose(1, 4, 2, 5, 0, 3).reshape(nh, w, 2 * S5_STATE)
    qr = c_re[:, :, :, None] * pr[:, None, :, 1:] - c_im[:, :, :, None] * pi_[:, None, :, 1:]
    qi = c_re[:, :, :, None] * pi_[:, None, :, 1:] + c_im[:, :, :, None] * pr[:, None, :, 1:]
    cc = jnp.stack([qr, -qi], axis=0).reshape(2, nh, gh, S5_GROUP, S5_STATE, ls)
    cc = cc.transpose(1, 0, 2, 4, 5, 3).reshape(nh, 2 * S5_HSTATE, ls * S5_GROUP)
    lam_l = jnp.stack([pr[:, :, ls].reshape(nh, S5_HSTATE), pi_[:, :, ls].reshape(nh, S5_HSTATE)], axis=1)
    return tc.astype(BF16), bc.astype(BF16), cc.astype(BF16), lam_l.astype(F32)


def _mix_kernel(rest_ref, u5_ref, ys5_ref, scw_ref, s5d_ref, wglu_ref, bglu_ref, mcw_ref, mcb_ref,
                dtb_ref, alog_ref, dvec_ref, ng_ref, o_ref, cbuf, mbuf, hst, ybuf, *, L):
    @pl.when(pl.program_id(1) == 0)
    def _():
        cbuf[0:HIST, :] = jnp.zeros((HIST, SC_WIDTH), F32)
        mbuf[0:HIST, :] = jnp.zeros((HIST, M2_XBC), F32)
        hst[...] = jnp.zeros(hst.shape, F32)

    u = rest_ref[:, R_SC + SC_WIDTH:R_SC + 2 * SC_WIDTH] * rest_ref[:, R_SC + 2 * SC_WIDTH:R_SC + 3 * SC_WIDTH]
    cbuf[HIST:HIST + L, :] = u
    conv = (scw_ref[2:3, :] * u + scw_ref[1:2, :] * cbuf[HIST - 1:HIST - 1 + L, :]
            + scw_ref[0:1, :] * cbuf[HIST - 2:HIST - 2 + L, :])
    cbuf[0:HIST, :] = cbuf[L:L + HIST, :]
    o_ref[:, 0:SC_WIDTH] = (rest_ref[:, R_SC:R_SC + SC_WIDTH] * conv).astype(o_ref.dtype)

    halves = lambda ref: jnp.concatenate([ref[hh] for hh in range(S5_WIDTH // LANES)], axis=1)
    y5 = halves(ys5_ref) + s5d_ref[...] * halves(u5_ref)
    gl = jax.nn.gelu(y5, approximate=True)
    gate = _dot(gl.astype(BF16), wglu_ref[...]) + bglu_ref[...]
    o_ref[:, SC_WIDTH:SC_WIDTH + S5_WIDTH] = (gl * (1.0 / (1.0 + jnp.exp(-gate)))).astype(o_ref.dtype)

    xr = rest_ref[:, R_XBC:R_XBC + M2_XBC]
    mbuf[HIST:HIST + L, :] = xr
    conv = (mcw_ref[3:4, :] * xr + mcw_ref[2:3, :] * mbuf[HIST - 1:HIST - 1 + L, :]
            + mcw_ref[1:2, :] * mbuf[HIST - 2:HIST - 2 + L, :]
            + mcw_ref[0:1, :] * mbuf[HIST - 3:HIST - 3 + L, :] + mcb_ref[...])
    mbuf[0:HIST, :] = mbuf[L:L + HIST, :]
    xbc = _silu(conv)
    xs = xbc[:, 0:M2_WIDTH]
    dtr = rest_ref[:, R_DT:R_DT + LANES] + dtb_ref[...]
    dtv = jnp.maximum(dtr, 0.0) + jnp.log(1.0 + jnp.exp(-jnp.abs(dtr)))
    a = -jnp.exp(alog_ref[...]) * dtv
    r_i = lax.broadcasted_iota(jnp.int32, (L, L), 0)
    c_i = lax.broadcasted_iota(jnp.int32, (L, L), 1)
    tril = c_i <= r_i
    tri = jnp.where(tril, 1.0, 0.0).astype(BF16)
    cs = sum(_dot(tri, part) for part in _split_bf16(a, 3))
    cs_t = cs.T
    cs_last = cs[L - 1:L, :]
    ecs = jnp.exp(cs)
    dec = jnp.exp(cs_last - cs)
    ecl = jnp.exp(cs_last)
    lane = lax.broadcasted_iota(jnp.int32, (1, LANES), 1)
    for g in range(M2_GROUPS):
        bg = xbc[:, M2_WIDTH + g * M2_STATE:M2_WIDTH + (g + 1) * M2_STATE].astype(BF16)
        cg = xbc[:, M2_WIDTH + (M2_GROUPS + g) * M2_STATE:M2_WIDTH + (M2_GROUPS + g + 1) * M2_STATE].astype(BF16)
        gram = _dot_nt(cg, bg)
        hs = hst[g]
        yoff = _dot(cg, hs.astype(BF16))
        xdec = []
        for hh in range(2):
            h = 2 * g + hh
            seg = cs[:, h:h + 1] - cs_t[h:h + 1, :]
            dm = jnp.exp(jnp.where(tril, seg, -jnp.inf))
            xdt = xs[:, h * M2_HEAD_DIM:(h + 1) * M2_HEAD_DIM] * dtv[:, h:h + 1]
            yd = _dot((gram * dm).astype(BF16), xdt.astype(BF16))
            yo = yoff[:, hh * M2_HEAD_DIM:(hh + 1) * M2_HEAD_DIM] * ecs[:, h:h + 1]
            ybuf[:, h * M2_HEAD_DIM:(h + 1) * M2_HEAD_DIM] = yd + yo
            xdec.append(xdt * dec[:, h:h + 1])
        upd = _dot_tn(bg, jnp.concatenate(xdec, axis=1).astype(BF16))
        keep = jnp.where(lane < M2_HEAD_DIM, ecl[:, 2 * g:2 * g + 1], ecl[:, 2 * g + 1:2 * g + 2])
        hst[g] = hs * keep + upd
    y = ybuf[...] + dvec_ref[...] * xs
    yg = y * _silu(rest_ref[:, R_Z:R_Z + M2_WIDTH])
    ms = jnp.mean(yg * yg, axis=-1, keepdims=True)
    o_ref[:, SC_WIDTH + S5_WIDTH:] = (yg * lax.rsqrt(ms + EPS) * ng_ref[...]).astype(o_ref.dtype)


def _mixers(rest, u5, ys5, scw, s5d, wglu, bglu, mcw, mcb, dtb, alog, dvec, ng, L):
    bsz, seq, _ = rest.shape
    kern = functools.partial(_mix_kernel, L=L)
    blk = lambda b, c: (b, c, 0)
    wout = SC_WIDTH + S5_WIDTH + M2_WIDTH
    return pl.pallas_call(
        kern,
        grid=(bsz, seq // L),
        in_specs=[
            pl.BlockSpec((None, L, D_REST), blk),
            pl.BlockSpec((S5_WIDTH // LANES, L, LANES), lambda b, c: (0, b * (seq // L) + c, 0)),
            pl.BlockSpec((S5_WIDTH // LANES, L, LANES), lambda b, c: (0, b * (seq // L) + c, 0)),
            _const_spec((3, SC_WIDTH)),
            _const_spec((1, S5_WIDTH)),
            _const_spec((S5_WIDTH, S5_WIDTH)),
            _const_spec((1, S5_WIDTH)),
            _const_spec((4, M2_XBC)),
            _const_spec((1, M2_XBC)),
            _const_spec((1, LANES)),
            _const_spec((1, LANES)),
            _const_spec((1, M2_WIDTH)),
            _const_spec((1, M2_WIDTH)),
        ],
        out_specs=pl.BlockSpec((None, L, wout), blk),
        out_shape=jax.ShapeDtypeStruct((bsz, seq, wout), BF16),
        scratch_shapes=[
            pltpu.VMEM((L + HIST, SC_WIDTH), F32),
            pltpu.VMEM((L + HIST, M2_XBC), F32),
            pltpu.VMEM((M2_GROUPS, M2_STATE, 2 * M2_HEAD_DIM), F32),
            pltpu.VMEM((L, M2_WIDTH), F32),
        ],
        compiler_params=pltpu.CompilerParams(
            dimension_semantics=("arbitrary", "arbitrary"), vmem_limit_bytes=VMEM_LIMIT),
        name="mixers",
    )(rest, u5, ys5, scw, s5d, wglu, bglu, mcw, mcb, dtb, alog, dvec, ng)


def _ffn_kernel(x_ref, oa_ref, ob_ref, wo_ref, g2_ref, wg_ref, wu_ref, cw_ref, wd_ref,
                o_ref, hist, cb, *, tiles_per_seq, tf):
    tm = x_ref.shape[0]

    @pl.when(pl.program_id(0) % tiles_per_seq == 0)
    def _():
        hist[...] = jnp.zeros(hist.shape, F32)

    x1 = (x_ref[...] + _dot(oa_ref[...], wo_ref[0:DA_WIDTH, :])
          + _dot(ob_ref[...], wo_ref[DA_WIDTH:D_MIX, :]))
    ms = jnp.mean(x1 * x1, axis=-1, keepdims=True)
    h2 = (x1 * lax.rsqrt(ms + EPS) * g2_ref[...]).astype(BF16)
    o_ref[...] = x1
    for c in range(D_FF // tf):
        sl = slice(c * tf, (c + 1) * tf)
        gpre = _dot(h2, wg_ref[:, sl])
        cb[0:HIST, :] = hist[:, sl]
        cb[HIST:HIST + tm, :] = gpre
        gc = (cw_ref[2:3, sl] * gpre + cw_ref[1:2, sl] * cb[HIST - 1:HIST - 1 + tm, :]
              + cw_ref[0:1, sl] * cb[HIST - 2:HIST - 2 + tm, :])
        hist[:, sl] = cb[tm:tm + HIST, :]
        act = (_silu(gc) * _dot(h2, wu_ref[:, sl])).astype(BF16)
        o_ref[...] += _dot(act, wd_ref[sl, :])


def _outproj_ffn(x2, oa, ob, wo, g2, wg, wu, cw, wd, layer, seq, tm, tf):
    t = x2.shape[0]
    wb = ob.shape[1]
    kern = functools.partial(_ffn_kernel, tiles_per_seq=seq // tm, tf=tf)
    row = lambda i: (i, 0)
    return pl.pallas_call(
        kern,
        grid=(t // tm,),
        in_specs=[
            pl.BlockSpec((tm, D_MODEL), row),
            pl.BlockSpec((tm, DA_WIDTH), row),
            pl.BlockSpec((tm, wb), row),
            _layer_spec(wo.shape, layer),
            _const_spec((1, D_MODEL)),
            _layer_spec(wg.shape, layer),
            _layer_spec(wu.shape, layer),
            _const_spec((3, D_FF)),
            _layer_spec(wd.shape, layer),
        ],
        out_specs=pl.BlockSpec((tm, D_MODEL), row),
        out_shape=jax.ShapeDtypeStruct((t, D_MODEL), F32),
        scratch_shapes=[
            pltpu.VMEM((HIST, D_FF), F32),
            pltpu.VMEM((tm + HIST, tf), F32),
        ],
        compiler_params=pltpu.CompilerParams(
            dimension_semantics=("arbitrary",), vmem_limit_bytes=VMEM_LIMIT),
        name="outproj_ffn",
    )(x2, oa, ob, wo, g2, wg, wu, cw, wd)


def _rope_lane_tables(seq):
    inv = 1.0 / (ROPE_THETA ** (jnp.arange(0, ROPE_DIM, 2, dtype=F32) / ROPE_DIM))
    ang = jnp.arange(seq, dtype=F32)[:, None] * inv[None, :]
    cos, sin = jnp.cos(ang), jnp.sin(ang)
    ones = jnp.ones((seq, DA_QK - ROPE_DIM), F32)
    cos64 = jnp.concatenate([cos, cos, ones], axis=1)
    sin64 = jnp.concatenate([-sin, sin, 0.0 * ones], axis=1)
    return jnp.tile(cos64, (1, LANES // DA_QK)), jnp.tile(sin64, (1, LANES // DA_QK))


def _forward(x, p, blk, mix_l, s5_tr):
    bsz, seq, _ = x.shape
    depth = p["w_in"].shape[0]
    t = bsz * seq
    ls = S5_CHUNK
    nc = seq // ls
    assert seq % blk == 0 and seq % mix_l == 0 and nc % s5_tr == 0

    cos_t, sin_t = _rope_lane_tables(seq)
    comp = lax.broadcasted_iota(jnp.int32, (QK_COLS, QK_COLS), 0) // DA_QK
    gm = (comp == comp.T).astype(BF16)
    w_in = p["w_in"].astype(BF16)
    w_dt = jnp.pad(p["w_in"][:, :, W_DT:], ((0, 0), (0, 0), (0, LANES - M2_HEADS))).astype(BF16)
    w_out = p["w_out"].astype(BF16)
    w_gate, w_up, w_down = (p[k].astype(BF16) for k in ("ffn_w_gate", "ffn_w_up", "ffn_w_down"))
    s5_ops = jax.vmap(functools.partial(_s5_operators, ls=ls))(
        p["s5_lam_re"], p["s5_lam_im"], p["s5_log_dt"], p["s5_b_re"], p["s5_b_im"],
        p["s5_c_re"], p["s5_c_im"])
    x2 = x.reshape(t, D_MODEL)
    for l in range(depth):
        lambda_init = 0.8 - 0.6 * math.exp(-0.3 * l)
        qkg = jnp.tile(p["qk_norm_g"][l], (1, LANES // DA_QK))
        qt, kb, vtb, rest, u5 = _inproj(x2, p["ln1_g"][l][None, :], w_in, w_dt, l, qkg, cos_t, sin_t,
                                        gm, bsz, seq, blk)
        oa = _attention(qt, kb, vtb, p["da_lambda"][l], p["subln_g"][l][None, :], lambda_init)
        ys5 = _s5_scan(u5, *s5_ops, l, ls, nc, s5_tr)

        pad4 = lambda a: jnp.pad(a, (0, LANES - M2_HEADS))[None, :]
        ob = _mixers(rest.reshape(bsz, seq, D_REST), u5, ys5, p["sc_conv_w"][l], p["s5_d"][l][None, :],
                     p["s5_w_glu"][l].astype(BF16), p["s5_b_glu"][l][None, :], p["m2_conv_w"][l],
                     p["m2_conv_b"][l][None, :], pad4(p["m2_dt_bias"][l]), pad4(p["m2_a_log"][l]),
                     jnp.repeat(p["m2_d"][l], M2_HEAD_DIM)[None, :], p["m2_norm_g"][l][None, :], mix_l)

        x2 = _outproj_ffn(x2, oa.reshape(t, DA_WIDTH), ob.reshape(t, D_MIX - DA_WIDTH), w_out,
                          p["ln2_g"][l][None, :], w_gate, w_up, p["ffn_conv_w"][l], w_down,
                          l, seq, min(FFN_ROWS, seq), FFN_COLS)
    return x2.reshape(bsz, seq, D_MODEL)


def kernel(x, ln1_g, w_in, qk_norm_g, da_lambda, subln_g, sc_conv_w, s5_lam_re, s5_lam_im, s5_log_dt, s5_b_re, s5_b_im, s5_c_re, s5_c_im, s5_d, s5_w_glu, s5_b_glu, m2_conv_w, m2_conv_b, m2_dt_bias, m2_a_log, m2_d, m2_norm_g, w_out, ln2_g, ffn_w_gate, ffn_w_up, ffn_conv_w, ffn_w_down):
    params = dict(ln1_g=ln1_g, w_in=w_in, qk_norm_g=qk_norm_g, da_lambda=da_lambda, subln_g=subln_g,
                  sc_conv_w=sc_conv_w, s5_lam_re=s5_lam_re, s5_lam_im=s5_lam_im, s5_log_dt=s5_log_dt,
                  s5_b_re=s5_b_re, s5_b_im=s5_b_im, s5_c_re=s5_c_re, s5_c_im=s5_c_im, s5_d=s5_d,
                  s5_w_glu=s5_w_glu, s5_b_glu=s5_b_glu, m2_conv_w=m2_conv_w, m2_conv_b=m2_conv_b,
                  m2_dt_bias=m2_dt_bias, m2_a_log=m2_a_log, m2_d=m2_d, m2_norm_g=m2_norm_g,
                  w_out=w_out, ln2_g=ln2_g, ffn_w_gate=ffn_w_gate, ffn_w_up=ffn_w_up,
                  ffn_conv_w=ffn_conv_w, ffn_w_down=ffn_w_down)
    seq = x.shape[1]
    return _forward(x, params, blk=min(512, seq), mix_l=min(256, seq),
                    s5_tr=min(256, seq // S5_CHUNK))
```

```python
import functools
import math

import jax
import jax.numpy as jnp
from jax import lax
from jax.experimental import pallas as pl
from jax.experimental.pallas import tpu as pltpu

F32 = jnp.float32
BF16 = jnp.bfloat16

D_MODEL = 1024
DA_HEADS = 4
DA_QK = 64
DA_V = 2 * DA_QK
DA_VX = DA_V + 16
DA_WIDTH = DA_HEADS * DA_V
ROPE_DIM = DA_QK // 4
ROPE_THETA = 500000.0
SC_WIDTH = 256
S5_WIDTH = 256
S5_GROUP = 16
S5_GROUPS = S5_WIDTH // S5_GROUP
S5_STATE = 64
M2_HEADS = 4
M2_HEAD_DIM = 64
M2_WIDTH = M2_HEADS * M2_HEAD_DIM
M2_GROUPS = 2
M2_STATE = 128
M2_XBC = M2_WIDTH + 2 * M2_GROUPS * M2_STATE
D_MIX = DA_WIDTH + SC_WIDTH + S5_WIDTH + M2_WIDTH
D_FF = 2816
EPS = 1e-6
LOG2E = 1.4426950408889634

LANES = 128
HIST = 8
VMEM_LIMIT = 56 * 1024 * 1024

W_Q = 0
W_K = W_Q + DA_WIDTH
W_V = W_K + DA_WIDTH
W_SC = W_V + DA_WIDTH
W_S5 = W_SC + 3 * SC_WIDTH
W_Z = W_S5 + S5_WIDTH
W_DT = W_Z + M2_WIDTH + M2_XBC
D_PROJ = W_DT + M2_HEADS
R_SC = 0
R_Z = R_SC + 3 * SC_WIDTH
R_XBC = R_Z + M2_WIDTH
R_DT = R_XBC + M2_XBC
D_REST = R_DT + LANES
QK_COLS = 2 * LANES
ATT_HEADS_PER_STEP = 2
FFN_ROWS = 512
FFN_COLS = 2816

S5_CHUNK = 8
S5_HALF = LANES // S5_GROUP
S5_HSTATE = S5_HALF * S5_STATE


def _dot(a, b):
    return jnp.dot(a, b, preferred_element_type=F32)


def _dot_nt(a, b):
    return lax.dot_general(a, b, (((1,), (1,)), ((), ())), preferred_element_type=F32)


def _dot_tn(a, b):
    return lax.dot_general(a, b, (((0,), (0,)), ((), ())), preferred_element_type=F32)


def _split_bf16(x, parts):
    out = []
    r = x
    for _ in range(parts):
        p = r.astype(BF16)
        out.append(p)
        r = r - p.astype(F32)
    return out


def _sigmoid(x):
    return 0.5 + 0.5 * jnp.tanh(0.5 * x)


def _silu(x):
    return x * _sigmoid(x)


def _const_spec(shape):
    nd = len(shape)
    return pl.BlockSpec(shape, lambda *_: (0,) * nd, pipeline_mode=pl.Buffered(1))


def _layer_spec(shape, layer):
    nd = len(shape)
    return pl.BlockSpec((None,) + tuple(shape[1:]), lambda *_: (layer,) + (0,) * (nd - 1),
                        pipeline_mode=pl.Buffered(1))


def _inproj_kernel(x_ref, g1_ref, w_ref, wdt_ref, qkg_ref, cos_ref, sin_ref, gm_ref,
                   qt_ref, kb_ref, vt_ref, rest_ref, u5_ref):
    tm = x_ref.shape[0]
    x = x_ref[...]
    ms = jnp.mean(x * x, axis=-1, keepdims=True)
    hn = (x * lax.rsqrt(ms + EPS) * g1_ref[...]).astype(BF16)
    cosv = cos_ref[...]
    sinv = sin_ref[...]
    gm = gm_ref[...]
    lane = lax.broadcasted_iota(jnp.int32, (tm, LANES), 1)
    pair_up = (lane % DA_QK) < (ROPE_DIM // 2)
    heads_per_dot = QK_COLS // DA_V

    def qk_heads(col0, gvec, scale):
        y = _dot(hn, w_ref[:, col0:col0 + QK_COLS])
        hi, lo = _split_bf16(y * y, 2)
        ss = _dot(hi, gm) + _dot(lo, gm)
        yn = y * lax.rsqrt(ss * (1.0 / DA_QK) + EPS)
        out = []
        for j in range(heads_per_dot):
            yh = yn[:, j * DA_V:(j + 1) * DA_V] * gvec
            partner = jnp.where(pair_up,
                                pltpu.roll(yh, LANES - ROPE_DIM // 2, 1),
                                pltpu.roll(yh, ROPE_DIM // 2, 1))
            out.append((yh * cosv + partner * sinv) * scale)
        return out

    for h0 in range(0, DA_HEADS, heads_per_dot):
        qs = qk_heads(W_Q + h0 * DA_V, qkg_ref[0:1, :], DA_QK ** -0.5 * LOG2E)
        ks = qk_heads(W_K + h0 * DA_V, qkg_ref[1:2, :], 1.0)
        v = _dot(hn, w_ref[:, W_V + h0 * DA_V:W_V + h0 * DA_V + QK_COLS])
        for j in range(heads_per_dot):
            qt_ref[h0 + j] = qs[j].T.astype(BF16)
            kb_ref[h0 + j] = ks[j].astype(BF16)
            vt_ref[h0 + j, 0:DA_V, :] = v[:, j * DA_V:(j + 1) * DA_V].T.astype(BF16)
            vt_ref[h0 + j, DA_V:DA_VX, :] = jnp.ones((DA_VX - DA_V, tm), BF16)
    rest_ref[:, R_SC:R_Z] = _dot(hn, w_ref[:, W_SC:W_S5])
    rest_ref[:, R_Z:R_DT] = _dot(hn, w_ref[:, W_Z:W_DT])
    rest_ref[:, R_DT:D_REST] = _dot(hn, wdt_ref[...])
    u5 = _dot(hn, w_ref[:, W_S5:W_Z])
    for hh in range(S5_WIDTH // LANES):
        u5_ref[hh] = u5[:, hh * LANES:(hh + 1) * LANES]


def _inproj(x2, g1, w, wdt, layer, qkg, cos_t, sin_t, gm, bsz, seq, tm):
    t = x2.shape[0]
    nt = seq // tm
    row = lambda i: (i, 0)
    tab = lambda i: (i % nt, 0)
    return pl.pallas_call(
        _inproj_kernel,
        grid=(t // tm,),
        in_specs=[
            pl.BlockSpec((tm, D_MODEL), row),
            _const_spec((1, D_MODEL)),
            _layer_spec(w.shape, layer),
            _layer_spec(wdt.shape, layer),
            _const_spec((2, LANES)),
            pl.BlockSpec((tm, LANES), tab),
            pl.BlockSpec((tm, LANES), tab),
            _const_spec((QK_COLS, QK_COLS)),
        ],
        out_specs=[
            pl.BlockSpec((None, DA_HEADS, DA_V, tm), lambda i: (i // nt, 0, 0, i % nt)),
            pl.BlockSpec((None, DA_HEADS, None, tm, DA_V), lambda i: (i // nt, 0, i % nt, 0, 0)),
            pl.BlockSpec((None, DA_HEADS, None, DA_VX, tm), lambda i: (i // nt, 0, i % nt, 0, 0)),
            pl.BlockSpec((tm, D_REST), row),
            pl.BlockSpec((S5_WIDTH // LANES, tm, LANES), lambda i: (0, i, 0)),
        ],
        out_shape=[
            jax.ShapeDtypeStruct((bsz, DA_HEADS, DA_V, seq), BF16),
            jax.ShapeDtypeStruct((bsz, DA_HEADS, nt, tm, DA_V), BF16),
            jax.ShapeDtypeStruct((bsz, DA_HEADS, nt, DA_VX, tm), BF16),
            jax.ShapeDtypeStruct((t, D_REST), F32),
            jax.ShapeDtypeStruct((S5_WIDTH // LANES, t, LANES), F32),
        ],
        compiler_params=pltpu.CompilerParams(
            dimension_semantics=("arbitrary",), vmem_limit_bytes=VMEM_LIMIT),
        name="inproj",
    )(x2, g1, w, wdt, qkg, cos_t, sin_t, gm)


def _attn_kernel(qt_ref, k_ref, vt_ref, lam_ref, sg_ref, o_ref, qz_sc, s_sc, m_sc, acc_sc,
                 *, lambda_init):
    qi = pl.program_id(2)
    heads = qt_ref.shape[0]
    streams = [(hd, c) for hd in range(heads) for c in range(2)]
    for n, (hd, c) in enumerate(streams):
        qt = qt_ref[hd]
        comp = lax.broadcasted_iota(jnp.int32, qt.shape, 0) // DA_QK
        qz_sc[n] = jnp.where(comp == c, qt, jnp.zeros_like(qt))
    m_sc[...] = jnp.full(m_sc.shape, -1e30, F32)
    acc_sc[...] = jnp.zeros(acc_sc.shape, F32)

    def scores(j, buf):
        for n, (hd, c) in enumerate(streams):
            s_sc[buf, n] = _dot(k_ref[hd, j], qz_sc[n])

    def absorb(j, buf, masked):
        for n, (hd, c) in enumerate(streams):
            s = s_sc[buf, n]
            if masked:
                kpos = lax.broadcasted_iota(jnp.int32, s.shape, 0)
                qpos = lax.broadcasted_iota(jnp.int32, s.shape, 1)
                s = jnp.where(kpos <= qpos, s, -jnp.inf)
            m_prev = m_sc[n]
            m_new = jnp.maximum(m_prev, jnp.max(s, axis=0, keepdims=True))
            alpha = jnp.exp2(m_prev - m_new)
            p = jnp.exp2(s - m_new).astype(BF16)
            acc_sc[n] = alpha * acc_sc[n] + _dot(vt_ref[hd, j], p)
            m_sc[n] = m_new

    scores(0, 0)
    pairs = qi // 2

    def body(t, carry):
        j = 2 * t
        scores(j + 1, 1)
        absorb(j, 0, False)
        scores(j + 2, 0)
        absorb(j + 1, 1, False)
        return carry

    lax.fori_loop(0, pairs, body, 0)
    j0 = 2 * pairs

    @pl.when(j0 == qi)
    def _():
        absorb(qi, 0, True)

    @pl.when(j0 != qi)
    def _():
        scores(qi, 1)
        absorb(j0, 0, False)
        absorb(qi, 1, True)

    lp = lam_ref[...]
    lam = (jnp.exp(jnp.sum(lp[0:1] * lp[1:2], axis=1, keepdims=True))
           - jnp.exp(jnp.sum(lp[2:3] * lp[3:4], axis=1, keepdims=True)) + lambda_init)
    for hd in range(heads):
        a0, a1 = acc_sc[2 * hd], acc_sc[2 * hd + 1]
        ot = (a0[0:DA_V] / a0[DA_V:DA_V + 1] - lam * (a1[0:DA_V] / a1[DA_V:DA_V + 1]))
        ms = jnp.mean(ot * ot, axis=0, keepdims=True)
        o = (ot * lax.rsqrt(ms + EPS)).T * (sg_ref[...] * (1.0 - lambda_init))
        o_ref[:, hd * DA_V:(hd + 1) * DA_V] = o.astype(o_ref.dtype)


def _attention(qt, kb, vtb, lam_p, sub_g, lambda_init):
    bsz, _, _, seq = qt.shape
    nb, blk = kb.shape[2], kb.shape[3]
    hps = ATT_HEADS_PER_STEP
    kern = functools.partial(_attn_kernel, lambda_init=lambda_init)
    return pl.pallas_call(
        kern,
        grid=(bsz, DA_HEADS // hps, nb),
        in_specs=[
            pl.BlockSpec((None, hps, DA_V, blk), lambda b, h, i: (b, h, 0, i)),
            pl.BlockSpec((None, hps, nb, blk, DA_V), lambda b, h, i: (b, h, 0, 0, 0)),
            pl.BlockSpec((None, hps, nb, DA_VX, blk), lambda b, h, i: (b, h, 0, 0, 0)),
            pl.BlockSpec((4, DA_QK), lambda b, h, i: (0, 0)),
            pl.BlockSpec((1, DA_V), lambda b, h, i: (0, 0)),
        ],
        out_specs=pl.BlockSpec((None, blk, hps * DA_V), lambda b, h, i: (b, i, h)),
        out_shape=jax.ShapeDtypeStruct((bsz, seq, DA_WIDTH), BF16),
        scratch_shapes=[
            pltpu.VMEM((2 * hps, DA_V, blk), BF16),
            pltpu.VMEM((2, 2 * hps, blk, blk), F32),
            pltpu.VMEM((2 * hps, 1, blk), F32),
            pltpu.VMEM((2 * hps, DA_VX, blk), F32),
        ],
        compiler_params=pltpu.CompilerParams(
            dimension_semantics=("arbitrary", "arbitrary", "arbitrary"),
            vmem_limit_bytes=VMEM_LIMIT),
        name="diff_attn",
    )(qt, kb, vtb, lam_p, sub_g)


def _expand_blockdiag(src_ref, dst_ref, row_group, col_div, col_inner):
    rows, wide = dst_ref.shape[1], dst_ref.shape[2]
    r_s = lax.broadcasted_iota(jnp.int32, (LANES, wide), 0)
    c_s = lax.broadcasted_iota(jnp.int32, (LANES, wide), 1)
    sel = jnp.where(r_s == (c_s // col_div) * col_inner + c_s % col_inner, 1.0, 0.0).astype(BF16)
    step = 256
    r_i = lax.broadcasted_iota(jnp.int32, (step, wide), 0)
    c_i = lax.broadcasted_iota(jnp.int32, (step, wide), 1)
    col_group = (c_i // (col_div // S5_HALF)) % S5_HALF
    for hh in range(dst_ref.shape[0]):
        for r0 in range(0, rows, step):
            same = ((r_i + r0) // row_group) % S5_HALF == col_group
            wide_blk = _dot(src_ref[hh, r0:r0 + step, :], sel)
            dst_ref[hh, r0:r0 + step, :] = jnp.where(same, wide_blk, 0.0).astype(BF16)


def _s5_kernel(u_ref, tc_ref, bc_ref, cc_ref, lam_ref, y_ref, toep_ref, bpow_ref, cpow_ref, st, hc,
               *, ls, tiles_per_seq):
    tr = u_ref.shape[1] // ls
    hs = S5_HSTATE

    @pl.when(pl.program_id(0) == 0)
    def _():
        _expand_blockdiag(tc_ref, toep_ref, S5_GROUP, LANES, S5_GROUP)
        _expand_blockdiag(bc_ref, bpow_ref, S5_GROUP, S5_HSTATE, S5_STATE)
        _expand_blockdiag(cc_ref, cpow_ref, S5_STATE, LANES, S5_GROUP)

    @pl.when(pl.program_id(0) % tiles_per_seq == 0)
    def _():
        hc[...] = jnp.zeros(hc.shape, F32)

    def half_input(hh):
        cols = [u_ref[hh, pl.ds(i, tr, stride=ls), :] for i in range(ls)]
        return jnp.concatenate(cols, axis=1).astype(BF16)

    for hh in range(2):
        st[hh] = _dot(half_input(hh), bpow_ref[hh])

    lam = [(lam_ref[hh, 0:1, :], lam_ref[hh, 1:2, :]) for hh in range(2)]

    def body(c, carry):
        new = []
        for hh in range(2):
            hr, hi = carry[2 * hh], carry[2 * hh + 1]
            lr, li = lam[hh]
            s_r = st[hh, pl.ds(c, 1), 0:hs]
            s_i = st[hh, pl.ds(c, 1), hs:2 * hs]
            st[hh, pl.ds(c, 1), 0:hs] = hr
            st[hh, pl.ds(c, 1), hs:2 * hs] = hi
            new.append(lr * hr - li * hi + s_r)
            new.append(lr * hi + li * hr + s_i)
        return tuple(new)

    init = tuple(hc[hh, :, ri * hs:(ri + 1) * hs] for hh in range(2) for ri in range(2))
    fin = lax.fori_loop(0, tr, body, init)
    for hh in range(2):
        for ri in range(2):
            hc[hh, :, ri * hs:(ri + 1) * hs] = fin[2 * hh + ri]

    for hh in range(2):
        y = _dot(half_input(hh), toep_ref[hh]) + _dot(st[hh].astype(BF16), cpow_ref[hh])
        for i in range(ls):
            y_ref[hh, pl.ds(i, tr, stride=ls), :] = y[:, i * LANES:(i + 1) * LANES]


def _s5_scan(u5, tc, bc, cc, lam_l, layer, ls, nc, tr):
    nh, t, w = u5.shape
    rows = tr * ls
    wide = ls * LANES
    kern = functools.partial(_s5_kernel, ls=ls, tiles_per_seq=nc // tr)
    row = lambda i: (0, i, 0)
    return pl.pallas_call(
        kern,
        grid=(t // rows,),
        in_specs=[
            pl.BlockSpec((nh, rows, w), row),
            _layer_spec(tc.shape, layer),
            _layer_spec(bc.shape, layer),
            _layer_spec(cc.shape, layer),
            _layer_spec(lam_l.shape, layer),
        ],
        out_specs=pl.BlockSpec((nh, rows, w), row),
        out_shape=jax.ShapeDtypeStruct((nh, t, w), F32),
        scratch_shapes=[
            pltpu.VMEM((nh, wide, wide), BF16),
            pltpu.VMEM((nh, wide, 2 * S5_HSTATE), BF16),
            pltpu.VMEM((nh, 2 * S5_HSTATE, wide), BF16),
            pltpu.VMEM((nh, tr, 2 * S5_HSTATE), F32),
            pltpu.VMEM((nh, 1, 2 * S5_HSTATE), F32),
        ],
        compiler_params=pltpu.CompilerParams(
            dimension_semantics=("arbitrary",), vmem_limit_bytes=VMEM_LIMIT),
        name="s5_scan",
    )(u5, tc, bc, cc, lam_l)


def _s5_operators(lam_re, lam_im, log_dt, b_re, b_im, c_re, c_im, ls):
    hp = lax.Precision.HIGHEST
    dt = jnp.exp(log_dt)[:, None]
    zr, zi = lam_re * dt, lam_im * dt
    er = jnp.exp(zr)
    lbr, lbi = er * jnp.cos(zi), er * jnp.sin(zi)
    den = lam_re * lam_re + lam_im * lam_im
    nr, ni = lbr - 1.0, lbi
    fr = (nr * lam_re + ni * lam_im) / den
    fi = (ni * lam_re - nr * lam_im) / den
    bbr = fr[..., None] * b_re - fi[..., None] * b_im
    bbi = fr[..., None] * b_im + fi[..., None] * b_re
    tau = jnp.arange(ls + 1, dtype=F32)
    pe = jnp.exp(zr[..., None] * tau)
    pr, pi_ = pe * jnp.cos(zi[..., None] * tau), pe * jnp.sin(zi[..., None] * tau)
    cpr = c_re[:, :, :, None] * pr[:, None, :, :ls] - c_im[:, :, :, None] * pi_[:, None, :, :ls]
    cpi = c_re[:, :, :, None] * pi_[:, None, :, :ls] + c_im[:, :, :, None] * pr[:, None, :, :ls]
    kk = (jnp.einsum('ghpt,gpk->gthk', cpr, bbr, precision=hp)
          - jnp.einsum('ghpt,gpk->gthk', cpi, bbi, precision=hp))
    lag = jnp.arange(ls)[None, :] - jnp.arange(ls)[:, None]
    kt = kk[:, jnp.clip(lag, 0, ls - 1)]
    kt = jnp.where((lag >= 0)[None, :, :, None, None], kt, 0.0)
    nh, gh = S5_GROUPS // S5_HALF, S5_HALF
    w = ls * LANES
    kt = kt.reshape(nh, gh, ls, ls, S5_GROUP, S5_GROUP)
    tc = kt.transpose(0, 2, 1, 5, 3, 4).reshape(nh, w, ls * S5_GROUP)
    tau_rev = (ls - 1) - jnp.arange(ls, dtype=F32)
    pe_rev = jnp.exp(zr[..., None] * tau_rev)
    rev_r, rev_i = pe_rev * jnp.cos(zi[..., None] * tau_rev), pe_rev * jnp.sin(zi[..., None] * tau_rev)
    bre = rev_r[..., None] * bbr[:, :, None, :] - rev_i[..., None] * bbi[:, :, None, :]
    bim = rev_r[..., None] * bbi[:, :, None, :] + rev_i[..., None] * bbr[:, :, None, :]
    bb = jnp.stack([bre, bim], axis=0).reshape(2, nh, gh, S5_STATE, ls, S5_GROUP)
    bc = bb.transpose(1, 4, 2, 5, 0, 3).reshape(nh, w, 2 * S5_STATE)
    qr = c_re[:, :, :, None] * pr[:, None, :, 1:] - c_im[:, :, :, None] * pi_[:, None, :, 1:]
    qi = c_re[:, :, :, None] * pi_[:, None, :, 1:] + c_im[:, :, :, None] * pr[:, None, :, 1:]
    cc = jnp.stack([qr, -qi], axis=0).reshape(2, nh, gh, S5_GROUP, S5_STATE, ls)
    cc = cc.transpose(1, 0, 2, 4, 5, 3).reshape(nh, 2 * S5_HSTATE, ls * S5_GROUP)
    lam_l = jnp.stack([pr[:, :, ls].reshape(nh, S5_HSTATE), pi_[:, :, ls].reshape(nh, S5_HSTATE)], axis=1)
    return tc.astype(BF16), bc.astype(BF16), cc.astype(BF16), lam_l.astype(F32)


def _mix_kernel(rest_ref, u5_ref, ys5_ref, scw_ref, s5d_ref, wglu_ref, bglu_ref, mcw_ref, mcb_ref,
                dtb_ref, alog_ref, dvec_ref, ng_ref, o_ref, cbuf, mbuf, hst, ybuf, *, L):
    @pl.when(pl.program_id(1) == 0)
    def _():
        cbuf[0:HIST, :] = jnp.zeros((HIST, SC_WIDTH), F32)
        mbuf[0:HIST, :] = jnp.zeros((HIST, M2_XBC), F32)
        hst[...] = jnp.zeros(hst.shape, F32)

    u = rest_ref[:, R_SC + SC_WIDTH:R_SC + 2 * SC_WIDTH] * rest_ref[:, R_SC + 2 * SC_WIDTH:R_SC + 3 * SC_WIDTH]
    cbuf[HIST:HIST + L, :] = u
    conv = (scw_ref[2:3, :] * u + scw_ref[1:2, :] * cbuf[HIST - 1:HIST - 1 + L, :]
            + scw_ref[0:1, :] * cbuf[HIST - 2:HIST - 2 + L, :])
    cbuf[0:HIST, :] = cbuf[L:L + HIST, :]
    o_ref[:, 0:SC_WIDTH] = (rest_ref[:, R_SC:R_SC + SC_WIDTH] * conv).astype(o_ref.dtype)

    halves = lambda ref: jnp.concatenate([ref[hh] for hh in range(S5_WIDTH // LANES)], axis=1)
    y5 = halves(ys5_ref) + s5d_ref[...] * halves(u5_ref)
    gl = jax.nn.gelu(y5, approximate=True)
    gate = _dot(gl.astype(BF16), wglu_ref[...]) + bglu_ref[...]
    o_ref[:, SC_WIDTH:SC_WIDTH + S5_WIDTH] = (gl * _sigmoid(gate)).astype(o_ref.dtype)

    xr = rest_ref[:, R_XBC:R_XBC + M2_XBC]
    mbuf[HIST:HIST + L, :] = xr
    conv = (mcw_ref[3:4, :] * xr + mcw_ref[2:3, :] * mbuf[HIST - 1:HIST - 1 + L, :]
            + mcw_ref[1:2, :] * mbuf[HIST - 2:HIST - 2 + L, :]
            + mcw_ref[0:1, :] * mbuf[HIST - 3:HIST - 3 + L, :] + mcb_ref[...])
    mbuf[0:HIST, :] = mbuf[L:L + HIST, :]
    xbc = _silu(conv)
    xs = xbc[:, 0:M2_WIDTH]
    dtr = rest_ref[:, R_DT:R_DT + LANES] + dtb_ref[...]
    dtv = jnp.maximum(dtr, 0.0) + jnp.log(1.0 + jnp.exp(-jnp.abs(dtr)))
    a = -jnp.exp(alog_ref[...]) * dtv
    r_i = lax.broadcasted_iota(jnp.int32, (L, L), 0)
    c_i = lax.broadcasted_iota(jnp.int32, (L, L), 1)
    tril = c_i <= r_i
    tri = jnp.where(tril, 1.0, 0.0).astype(BF16)
    cs = sum(_dot(tri, part) for part in _split_bf16(a, 3))
    cs_t = cs.T
    cs_last = cs[L - 1:L, :]
    ecs = jnp.exp(cs)
    dec = jnp.exp(cs_last - cs)
    ecl = jnp.exp(cs_last)
    lane = lax.broadcasted_iota(jnp.int32, (1, LANES), 1)
    for g in range(M2_GROUPS):
        bg = xbc[:, M2_WIDTH + g * M2_STATE:M2_WIDTH + (g + 1) * M2_STATE].astype(BF16)
        cg = xbc[:, M2_WIDTH + (M2_GROUPS + g) * M2_STATE:M2_WIDTH + (M2_GROUPS + g + 1) * M2_STATE].astype(BF16)
        gram = _dot_nt(cg, bg)
        hs = hst[g]
        yoff = _dot(cg, hs.astype(BF16))
        xdec = []
        for hh in range(2):
            h = 2 * g + hh
            seg = cs[:, h:h + 1] - cs_t[h:h + 1, :]
            dm = jnp.exp(jnp.where(tril, seg, -jnp.inf))
            xdt = xs[:, h * M2_HEAD_DIM:(h + 1) * M2_HEAD_DIM] * dtv[:, h:h + 1]
            yd = _dot((gram * dm).astype(BF16), xdt.astype(BF16))
            yo = yoff[:, hh * M2_HEAD_DIM:(hh + 1) * M2_HEAD_DIM] * ecs[:, h:h + 1]
            ybuf[:, h * M2_HEAD_DIM:(h + 1) * M2_HEAD_DIM] = yd + yo
            xdec.append(xdt * dec[:, h:h + 1])
        upd = _dot_tn(bg, jnp.concatenate(xdec, axis=1).astype(BF16))
        keep = jnp.where(lane < M2_HEAD_DIM, ecl[:, 2 * g:2 * g + 1], ecl[:, 2 * g + 1:2 * g + 2])
        hst[g] = hs * keep + upd
    y = ybuf[...] + dvec_ref[...] * xs
    yg = y * _silu(rest_ref[:, R_Z:R_Z + M2_WIDTH])
    ms = jnp.mean(yg * yg, axis=-1, keepdims=True)
    o_ref[:, SC_WIDTH + S5_WIDTH:] = (yg * lax.rsqrt(ms + EPS) * ng_ref[...]).astype(o_ref.dtype)


def _mixers(rest, u5, ys5, scw, s5d, wglu, bglu, mcw, mcb, dtb, alog, dvec, ng, L):
    bsz, seq, _ = rest.shape
    kern = functools.partial(_mix_kernel, L=L)
    blk = lambda b, c: (b, c, 0)
    wout = SC_WIDTH + S5_WIDTH + M2_WIDTH
    return pl.pallas_call(
        kern,
        grid=(bsz, seq // L),
        in_specs=[
            pl.BlockSpec((None, L, D_REST), blk),
            pl.BlockSpec((S5_WIDTH // LANES, L, LANES), lambda b, c: (0, b * (seq // L) + c, 0)),
            pl.BlockSpec((S5_WIDTH // LANES, L, LANES), lambda b, c: (0, b * (seq // L) + c, 0)),
            _const_spec((3, SC_WIDTH)),
            _const_spec((1, S5_WIDTH)),
            _const_spec((S5_WIDTH, S5_WIDTH)),
            _const_spec((1, S5_WIDTH)),
            _const_spec((4, M2_XBC)),
            _const_spec((1, M2_XBC)),
            _const_spec((1, LANES)),
            _const_spec((1, LANES)),
            _const_spec((1, M2_WIDTH)),
            _const_spec((1, M2_WIDTH)),
        ],
        out_specs=pl.BlockSpec((None, L, wout), blk),
        out_shape=jax.ShapeDtypeStruct((bsz, seq, wout), BF16),
        scratch_shapes=[
            pltpu.VMEM((L + HIST, SC_WIDTH), F32),
            pltpu.VMEM((L + HIST, M2_XBC), F32),
            pltpu.VMEM((M2_GROUPS, M2_STATE, 2 * M2_HEAD_DIM), F32),
            pltpu.VMEM((L, M2_WIDTH), F32),
        ],
        compiler_params=pltpu.CompilerParams(
            dimension_semantics=("arbitrary", "arbitrary"), vmem_limit_bytes=VMEM_LIMIT),
        name="mixers",
    )(rest, u5, ys5, scw, s5d, wglu, bglu, mcw, mcb, dtb, alog, dvec, ng)


def _ffn_kernel(x_ref, oa_ref, ob_ref, wo_ref, g2_ref, wg_ref, wu_ref, cw_ref, wd_ref,
                o_ref, hist, cb, *, tiles_per_seq, tf):
    tm = x_ref.shape[0]

    @pl.when(pl.program_id(0) % tiles_per_seq == 0)
    def _():
        hist[...] = jnp.zeros(hist.shape, F32)

    x1 = (x_ref[...] + _dot(oa_ref[...], wo_ref[0:DA_WIDTH, :])
          + _dot(ob_ref[...], wo_ref[DA_WIDTH:D_MIX, :]))
    ms = jnp.mean(x1 * x1, axis=-1, keepdims=True)
    h2 = (x1 * lax.rsqrt(ms + EPS) * g2_ref[...]).astype(BF16)
    o_ref[...] = x1
    for c in range(D_FF // tf):
        sl = slice(c * tf, (c + 1) * tf)
        gpre = _dot(h2, wg_ref[:, sl])
        cb[0:HIST, :] = hist[:, sl]
        cb[HIST:HIST + tm, :] = gpre
        gc = (cw_ref[2:3, sl] * gpre + cw_ref[1:2, sl] * cb[HIST - 1:HIST - 1 + tm, :]
              + cw_ref[0:1, sl] * cb[HIST - 2:HIST - 2 + tm, :])
        hist[:, sl] = cb[tm:tm + HIST, :]
        act = (_silu(gc) * _dot(h2, wu_ref[:, sl])).astype(BF16)
        o_ref[...] += _dot(act, wd_ref[sl, :])


def _outproj_ffn(x2, oa, ob, wo, g2, wg, wu, cw, wd, layer, seq, tm, tf):
    t = x2.shape[0]
    wb = ob.shape[1]
    kern = functools.partial(_ffn_kernel, tiles_per_seq=seq // tm, tf=tf)
    row = lambda i: (i, 0)
    return pl.pallas_call(
        kern,
        grid=(t // tm,),
        in_specs=[
            pl.BlockSpec((tm, D_MODEL), row),
            pl.BlockSpec((tm, DA_WIDTH), row),
            pl.BlockSpec((tm, wb), row),
            _layer_spec(wo.shape, layer),
            _const_spec((1, D_MODEL)),
            _layer_spec(wg.shape, layer),
            _layer_spec(wu.shape, layer),
            _const_spec((3, D_FF)),
            _layer_spec(wd.shape, layer),
        ],
        out_specs=pl.BlockSpec((tm, D_MODEL), row),
        out_shape=jax.ShapeDtypeStruct((t, D_MODEL), F32),
        scratch_shapes=[
            pltpu.VMEM((HIST, D_FF), F32),
            pltpu.VMEM((tm + HIST, tf), F32),
        ],
        compiler_params=pltpu.CompilerParams(
            dimension_semantics=("arbitrary",), vmem_limit_bytes=VMEM_LIMIT),
        name="outproj_ffn",
    )(x2, oa, ob, wo, g2, wg, wu, cw, wd)


def _rope_lane_tables(seq):
    inv = 1.0 / (ROPE_THETA ** (jnp.arange(0, ROPE_DIM, 2, dtype=F32) / ROPE_DIM))
    ang = jnp.arange(seq, dtype=F32)[:, None] * inv[None, :]
    cos, sin = jnp.cos(ang), jnp.sin(ang)
    ones = jnp.ones((seq, DA_QK - ROPE_DIM), F32)
    cos64 = jnp.concatenate([cos, cos, ones], axis=1)
    sin64 = jnp.concatenate([-sin, sin, 0.0 * ones], axis=1)
    return jnp.tile(cos64, (1, LANES // DA_QK)), jnp.tile(sin64, (1, LANES // DA_QK))


def _forward(x, p, blk, mix_l, s5_tr):
    bsz, seq, _ = x.shape
    depth = p["w_in"].shape[0]
    t = bsz * seq
    ls = S5_CHUNK
    nc = seq // ls
    assert seq % blk == 0 and seq % mix_l == 0 and nc % s5_tr == 0

    cos_t, sin_t = _rope_lane_tables(seq)
    comp = lax.broadcasted_iota(jnp.int32, (QK_COLS, QK_COLS), 0) // DA_QK
    gm = (comp == comp.T).astype(BF16)
    w_in = p["w_in"].astype(BF16)
    w_dt = jnp.pad(p["w_in"][:, :, W_DT:], ((0, 0), (0, 0), (0, LANES - M2_HEADS))).astype(BF16)
    w_out = p["w_out"].astype(BF16)
    w_gate, w_up, w_down = (p[k].astype(BF16) for k in ("ffn_w_gate", "ffn_w_up", "ffn_w_down"))
    s5_ops = jax.vmap(functools.partial(_s5_operators, ls=ls))(
        p["s5_lam_re"], p["s5_lam_im"], p["s5_log_dt"], p["s5_b_re"], p["s5_b_im"],
        p["s5_c_re"], p["s5_c_im"])
    x2 = x.reshape(t, D_MODEL)
    for l in range(depth):
        lambda_init = 0.8 - 0.6 * math.exp(-0.3 * l)
        qkg = jnp.tile(p["qk_norm_g"][l], (1, LANES // DA_QK))
        qt, kb, vtb, rest, u5 = _inproj(x2, p["ln1_g"][l][None, :], w_in, w_dt, l, qkg, cos_t, sin_t,
                                        gm, bsz, seq, blk)
        oa = _attention(qt, kb, vtb, p["da_lambda"][l], p["subln_g"][l][None, :], lambda_init)
        ys5 = _s5_scan(u5, *s5_ops, l, ls, nc, s5_tr)

        pad4 = lambda a: jnp.pad(a, (0, LANES - M2_HEADS))[None, :]
        ob = _mixers(rest.reshape(bsz, seq, D_REST), u5, ys5, p["sc_conv_w"][l], p["s5_d"][l][None, :],
                     p["s5_w_glu"][l].astype(BF16), p["s5_b_glu"][l][None, :], p["m2_conv_w"][l],
                     p["m2_conv_b"][l][None, :], pad4(p["m2_dt_bias"][l]), pad4(p["m2_a_log"][l]),
                     jnp.repeat(p["m2_d"][l], M2_HEAD_DIM)[None, :], p["m2_norm_g"][l][None, :], mix_l)

        x2 = _outproj_ffn(x2, oa.reshape(t, DA_WIDTH), ob.reshape(t, D_MIX - DA_WIDTH), w_out,
                          p["ln2_g"][l][None, :], w_gate, w_up, p["ffn_conv_w"][l], w_down,
                          l, seq, min(FFN_ROWS, seq), FFN_COLS)
    return x2.reshape(bsz, seq, D_MODEL)


def kernel(x, ln1_g, w_in, qk_norm_g, da_lambda, subln_g, sc_conv_w, s5_lam_re, s5_lam_im, s5_log_dt, s5_b_re, s5_b_im, s5_c_re, s5_c_im, s5_d, s5_w_glu, s5_b_glu, m2_conv_w, m2_conv_b, m2_dt_bias, m2_a_log, m2_d, m2_norm_g, w_out, ln2_g, ffn_w_gate, ffn_w_up, ffn_conv_w, ffn_w_down):
    params = dict(ln1_g=ln1_g, w_in=w_in, qk_norm_g=qk_norm_g, da_lambda=da_lambda, subln_g=subln_g,
                  sc_conv_w=sc_conv_w, s5_lam_re=s5_lam_re, s5_lam_im=s5_lam_im, s5_log_dt=s5_log_dt,
                  s5_b_re=s5_b_re, s5_b_im=s5_b_im, s5_c_re=s5_c_re, s5_c_im=s5_c_im, s5_d=s5_d,
                  s5_w_glu=s5_w_glu, s5_b_glu=s5_b_glu, m2_conv_w=m2_conv_w, m2_conv_b=m2_conv_b,
                  m2_dt_bias=m2_dt_bias, m2_a_log=m2_a_log, m2_d=m2_d, m2_norm_g=m2_norm_g,
                  w_out=w_out, ln2_g=ln2_g, ffn_w_gate=ffn_w_gate, ffn_w_up=ffn_w_up,
                  ffn_conv_w=ffn_conv_w, ffn_w_down=ffn_w_down)
    seq = x.shape[1]
    return _forward(x, params, blk=min(512, seq), mix_l=min(256, seq),
                    s5_tr=min(256, seq // S5_CHUNK))
```

```python
import functools
import math

import jax
import jax.numpy as jnp
from jax import lax
from jax.experimental import pallas as pl
from jax.experimental.pallas import tpu as pltpu

F32 = jnp.float32
BF16 = jnp.bfloat16

D_MODEL = 1024
DA_HEADS = 4
DA_QK = 64
DA_V = 2 * DA_QK
DA_VX = DA_V + 16
DA_WIDTH = DA_HEADS * DA_V
ROPE_DIM = DA_QK // 4
ROPE_THETA = 500000.0
SC_WIDTH = 256
S5_WIDTH = 256
S5_GROUP = 16
S5_GROUPS = S5_WIDTH // S5_GROUP
S5_STATE = 64
M2_HEADS = 4
M2_HEAD_DIM = 64
M2_WIDTH = M2_HEADS * M2_HEAD_DIM
M2_GROUPS = 2
M2_STATE = 128
M2_XBC = M2_WIDTH + 2 * M2_GROUPS * M2_STATE
D_MIX = DA_WIDTH + SC_WIDTH + S5_WIDTH + M2_WIDTH
D_FF = 2816
EPS = 1e-6
LOG2E = 1.4426950408889634

LANES = 128
HIST = 8
VMEM_LIMIT = 56 * 1024 * 1024

W_Q = 0
W_K = W_Q + DA_WIDTH
W_V = W_K + DA_WIDTH
W_SC = W_V + DA_WIDTH
W_S5 = W_SC + 3 * SC_WIDTH
W_Z = W_S5 + S5_WIDTH
W_DT = W_Z + M2_WIDTH + M2_XBC
D_PROJ = W_DT + M2_HEADS
R_SC = 0
R_Z = R_SC + 3 * SC_WIDTH
R_XBC = R_Z + M2_WIDTH
R_DT = R_XBC + M2_XBC
D_REST = R_DT + LANES
QK_COLS = 2 * LANES
ATT_HEADS_PER_STEP = 2
FFN_ROWS = 512
FFN_COLS = 2816

S5_CHUNK = 8
S5_HALF = LANES // S5_GROUP
S5_HSTATE = S5_HALF * S5_STATE


def _dot(a, b):
    return jnp.dot(a, b, preferred_element_type=F32)


def _dot_nt(a, b):
    return lax.dot_general(a, b, (((1,), (1,)), ((), ())), preferred_element_type=F32)


def _dot_tn(a, b):
    return lax.dot_general(a, b, (((0,), (0,)), ((), ())), preferred_element_type=F32)


def _split_bf16(x, parts):
    out = []
    r = x
    for _ in range(parts):
        p = r.astype(BF16)
        out.append(p)
        r = r - p.astype(F32)
    return out


def _sigmoid(x):
    return 0.5 + 0.5 * jnp.tanh(0.5 * x)


def _silu(x):
    return x * _sigmoid(x)


def _const_spec(shape):
    nd = len(shape)
    return pl.BlockSpec(shape, lambda *_: (0,) * nd, pipeline_mode=pl.Buffered(1))


def _layer_spec(shape, layer):
    nd = len(shape)
    return pl.BlockSpec((None,) + tuple(shape[1:]), lambda *_: (layer,) + (0,) * (nd - 1),
                        pipeline_mode=pl.Buffered(1))


def _inproj_kernel(x_ref, g1_ref, w_ref, wdt_ref, qkg_ref, cos_ref, sin_ref, gm_ref,
                   scw_ref, mcw_ref, mcb_ref, dtb_ref, alog_ref, dvec_ref, ng_ref,
                   qt_ref, kb_ref, vt_ref, obd_ref, u5_ref,
                   rest_ref, cbuf, mbuf, hst, ybuf, *, tiles_per_seq, mix_l):
    tm = x_ref.shape[0]

    @pl.when(pl.program_id(0) % tiles_per_seq == 0)
    def _():
        _mix_reset(cbuf, mbuf, hst)

    x = x_ref[...]
    ms = jnp.mean(x * x, axis=-1, keepdims=True)
    hn = (x * lax.rsqrt(ms + EPS) * g1_ref[...]).astype(BF16)
    rest_ref[:, R_SC:R_Z] = _dot(hn, w_ref[:, W_SC:W_S5])
    rest_ref[:, R_Z:R_DT] = _dot(hn, w_ref[:, W_Z:W_DT])
    rest_ref[:, R_DT:D_REST] = _dot(hn, wdt_ref[...])
    for r0 in range(0, tm, mix_l):
        _mix_chunk(rest_ref, r0, mix_l, scw_ref, mcw_ref, mcb_ref, dtb_ref, alog_ref, dvec_ref, ng_ref,
                   obd_ref, cbuf, mbuf, hst, ybuf)
    cosv = cos_ref[...]
    sinv = sin_ref[...]
    gm = gm_ref[...]
    lane = lax.broadcasted_iota(jnp.int32, (tm, LANES), 1)
    pair_up = (lane % DA_QK) < (ROPE_DIM // 2)
    heads_per_dot = QK_COLS // DA_V

    def qk_heads(col0, gvec, scale):
        y = _dot(hn, w_ref[:, col0:col0 + QK_COLS])
        hi, lo = _split_bf16(y * y, 2)
        ss = _dot(hi, gm) + _dot(lo, gm)
        yn = y * lax.rsqrt(ss * (1.0 / DA_QK) + EPS)
        out = []
        for j in range(heads_per_dot):
            yh = yn[:, j * DA_V:(j + 1) * DA_V] * gvec
            partner = jnp.where(pair_up,
                                pltpu.roll(yh, LANES - ROPE_DIM // 2, 1),
                                pltpu.roll(yh, ROPE_DIM // 2, 1))
            out.append((yh * cosv + partner * sinv) * scale)
        return out

    for h0 in range(0, DA_HEADS, heads_per_dot):
        qs = qk_heads(W_Q + h0 * DA_V, qkg_ref[0:1, :], DA_QK ** -0.5 * LOG2E)
        ks = qk_heads(W_K + h0 * DA_V, qkg_ref[1:2, :], 1.0)
        v = _dot(hn, w_ref[:, W_V + h0 * DA_V:W_V + h0 * DA_V + QK_COLS])
        for j in range(heads_per_dot):
            qt_ref[h0 + j] = qs[j].T.astype(BF16)
            kb_ref[h0 + j] = ks[j].astype(BF16)
            vt_ref[h0 + j, 0:DA_V, :] = v[:, j * DA_V:(j + 1) * DA_V].T.astype(BF16)
            vt_ref[h0 + j, DA_V:DA_VX, :] = jnp.ones((DA_VX - DA_V, tm), BF16)
    u5 = _dot(hn, w_ref[:, W_S5:W_Z])
    for hh in range(S5_WIDTH // LANES):
        u5_ref[hh] = u5[:, hh * LANES:(hh + 1) * LANES]


def _inproj(x2, g1, w, wdt, layer, qkg, cos_t, sin_t, gm, mix_params, bsz, seq, tm, mix_l):
    t = x2.shape[0]
    nt = seq // tm
    row = lambda i: (i, 0)
    tab = lambda i: (i % nt, 0)
    kern = functools.partial(_inproj_kernel, tiles_per_seq=nt, mix_l=mix_l)
    return pl.pallas_call(
        kern,
        grid=(t // tm,),
        in_specs=[
            pl.BlockSpec((tm, D_MODEL), row),
            _const_spec((1, D_MODEL)),
            _layer_spec(w.shape, layer),
            _layer_spec(wdt.shape, layer),
            _const_spec((2, LANES)),
            pl.BlockSpec((tm, LANES), tab),
            pl.BlockSpec((tm, LANES), tab),
            _const_spec((QK_COLS, QK_COLS)),
        ] + [_const_spec(a.shape) for a in mix_params],
        out_specs=[
            pl.BlockSpec((None, DA_HEADS, DA_V, tm), lambda i: (i // nt, 0, 0, i % nt)),
            pl.BlockSpec((None, DA_HEADS, None, tm, DA_V), lambda i: (i // nt, 0, i % nt, 0, 0)),
            pl.BlockSpec((None, DA_HEADS, None, DA_VX, tm), lambda i: (i // nt, 0, i % nt, 0, 0)),
            pl.BlockSpec((tm, SC_WIDTH + M2_WIDTH), row),
            pl.BlockSpec((S5_WIDTH // LANES, tm, LANES), lambda i: (0, i, 0)),
        ],
        out_shape=[
            jax.ShapeDtypeStruct((bsz, DA_HEADS, DA_V, seq), BF16),
            jax.ShapeDtypeStruct((bsz, DA_HEADS, nt, tm, DA_V), BF16),
            jax.ShapeDtypeStruct((bsz, DA_HEADS, nt, DA_VX, tm), BF16),
            jax.ShapeDtypeStruct((t, SC_WIDTH + M2_WIDTH), BF16),
            jax.ShapeDtypeStruct((S5_WIDTH // LANES, t, LANES), F32),
        ],
        scratch_shapes=[
            pltpu.VMEM((tm, D_REST), F32),
            pltpu.VMEM((mix_l + HIST, SC_WIDTH), F32),
            pltpu.VMEM((mix_l + HIST, M2_XBC), F32),
            pltpu.VMEM((M2_GROUPS, M2_STATE, 2 * M2_HEAD_DIM), F32),
            pltpu.VMEM((mix_l, M2_WIDTH), F32),
        ],
        compiler_params=pltpu.CompilerParams(
            dimension_semantics=("arbitrary",), vmem_limit_bytes=VMEM_LIMIT),
        name="inproj_mix",
    )(x2, g1, w, wdt, qkg, cos_t, sin_t, gm, *mix_params)


def _attn_kernel(qt_ref, k_ref, vt_ref, lam_ref, sg_ref, o_ref, qz_sc, s_sc, m_sc, acc_sc,
                 *, lambda_init):
    qi = pl.program_id(2)
    heads = qt_ref.shape[0]
    streams = [(hd, c) for hd in range(heads) for c in range(2)]
    for n, (hd, c) in enumerate(streams):
        qt = qt_ref[hd]
        comp = lax.broadcasted_iota(jnp.int32, qt.shape, 0) // DA_QK
        qz_sc[n] = jnp.where(comp == c, qt, jnp.zeros_like(qt))
    m_sc[...] = jnp.full(m_sc.shape, -1e30, F32)
    acc_sc[...] = jnp.zeros(acc_sc.shape, F32)

    def scores(j, buf):
        for n, (hd, c) in enumerate(streams):
            s_sc[buf, n] = _dot(k_ref[hd, j], qz_sc[n])

    def absorb(j, buf, masked):
        for n, (hd, c) in enumerate(streams):
            s = s_sc[buf, n]
            if masked:
                kpos = lax.broadcasted_iota(jnp.int32, s.shape, 0)
                qpos = lax.broadcasted_iota(jnp.int32, s.shape, 1)
                s = jnp.where(kpos <= qpos, s, -jnp.inf)
            m_prev = m_sc[n]
            m_new = jnp.maximum(m_prev, jnp.max(s, axis=0, keepdims=True))
            alpha = jnp.exp2(m_prev - m_new)
            p = jnp.exp2(s - m_new).astype(BF16)
            acc_sc[n] = alpha * acc_sc[n] + _dot(vt_ref[hd, j], p)
            m_sc[n] = m_new

    scores(0, 0)
    pairs = qi // 2

    def body(t, carry):
        j = 2 * t
        scores(j + 1, 1)
        absorb(j, 0, False)
        scores(j + 2, 0)
        absorb(j + 1, 1, False)
        return carry

    lax.fori_loop(0, pairs, body, 0)
    j0 = 2 * pairs

    @pl.when(j0 == qi)
    def _():
        absorb(qi, 0, True)

    @pl.when(j0 != qi)
    def _():
        scores(qi, 1)
        absorb(j0, 0, False)
        absorb(qi, 1, True)

    lp = lam_ref[...]
    lam = (jnp.exp(jnp.sum(lp[0:1] * lp[1:2], axis=1, keepdims=True))
           - jnp.exp(jnp.sum(lp[2:3] * lp[3:4], axis=1, keepdims=True)) + lambda_init)
    for hd in range(heads):
        a0, a1 = acc_sc[2 * hd], acc_sc[2 * hd + 1]
        ot = (a0[0:DA_V] / a0[DA_V:DA_V + 1] - lam * (a1[0:DA_V] / a1[DA_V:DA_V + 1]))
        ms = jnp.mean(ot * ot, axis=0, keepdims=True)
        o = (ot * lax.rsqrt(ms + EPS)).T * (sg_ref[...] * (1.0 - lambda_init))
        o_ref[:, hd * DA_V:(hd + 1) * DA_V] = o.astype(o_ref.dtype)


def _attention(qt, kb, vtb, lam_p, sub_g, lambda_init):
    bsz, _, _, seq = qt.shape
    nb, blk = kb.shape[2], kb.shape[3]
    hps = ATT_HEADS_PER_STEP
    kern = functools.partial(_attn_kernel, lambda_init=lambda_init)
    return pl.pallas_call(
        kern,
        grid=(bsz, DA_HEADS // hps, nb),
        in_specs=[
            pl.BlockSpec((None, hps, DA_V, blk), lambda b, h, i: (b, h, 0, i)),
            pl.BlockSpec((None, hps, nb, blk, DA_V), lambda b, h, i: (b, h, 0, 0, 0)),
            pl.BlockSpec((None, hps, nb, DA_VX, blk), lambda b, h, i: (b, h, 0, 0, 0)),
            pl.BlockSpec((4, DA_QK), lambda b, h, i: (0, 0)),
            pl.BlockSpec((1, DA_V), lambda b, h, i: (0, 0)),
        ],
        out_specs=pl.BlockSpec((None, blk, hps * DA_V), lambda b, h, i: (b, i, h)),
        out_shape=jax.ShapeDtypeStruct((bsz, seq, DA_WIDTH), BF16),
        scratch_shapes=[
            pltpu.VMEM((2 * hps, DA_V, blk), BF16),
            pltpu.VMEM((2, 2 * hps, blk, blk), F32),
            pltpu.VMEM((2 * hps, 1, blk), F32),
            pltpu.VMEM((2 * hps, DA_VX, blk), F32),
        ],
        compiler_params=pltpu.CompilerParams(
            dimension_semantics=("arbitrary", "arbitrary", "arbitrary"),
            vmem_limit_bytes=VMEM_LIMIT),
        name="diff_attn",
    )(qt, kb, vtb, lam_p, sub_g)


def _expand_blockdiag(src_ref, dst_ref, row_group, col_div, col_inner):
    rows, wide = dst_ref.shape[1], dst_ref.shape[2]
    r_s = lax.broadcasted_iota(jnp.int32, (LANES, wide), 0)
    c_s = lax.broadcasted_iota(jnp.int32, (LANES, wide), 1)
    sel = jnp.where(r_s == (c_s // col_div) * col_inner + c_s % col_inner, 1.0, 0.0).astype(BF16)
    step = 256
    r_i = lax.broadcasted_iota(jnp.int32, (step, wide), 0)
    c_i = lax.broadcasted_iota(jnp.int32, (step, wide), 1)
    col_group = (c_i // (col_div // S5_HALF)) % S5_HALF
    for hh in range(dst_ref.shape[0]):
        for r0 in range(0, rows, step):
            same = ((r_i + r0) // row_group) % S5_HALF == col_group
            wide_blk = _dot(src_ref[hh, r0:r0 + step, :], sel)
            dst_ref[hh, r0:r0 + step, :] = jnp.where(same, wide_blk, 0.0).astype(BF16)


def _s5_kernel(u_ref, tc_ref, bc_ref, cc_ref, lam_ref, s5d_ref, wglu_ref, bglu_ref, o_ref,
               toep_ref, bpow_ref, cpow_ref, st, hc, y_ref, *, ls, tiles_per_seq):
    tr = u_ref.shape[1] // ls
    hs = S5_HSTATE

    @pl.when(pl.program_id(0) == 0)
    def _():
        _expand_blockdiag(tc_ref, toep_ref, S5_GROUP, LANES, S5_GROUP)
        _expand_blockdiag(bc_ref, bpow_ref, S5_GROUP, S5_HSTATE, S5_STATE)
        _expand_blockdiag(cc_ref, cpow_ref, S5_STATE, LANES, S5_GROUP)

    @pl.when(pl.program_id(0) % tiles_per_seq == 0)
    def _():
        hc[...] = jnp.zeros(hc.shape, F32)

    def half_input(hh):
        cols = [u_ref[hh, pl.ds(i, tr, stride=ls), :] for i in range(ls)]
        return jnp.concatenate(cols, axis=1).astype(BF16)

    for hh in range(2):
        st[hh] = _dot(half_input(hh), bpow_ref[hh])

    lam = [(lam_ref[hh, 0:1, :], lam_ref[hh, 1:2, :]) for hh in range(2)]

    def body(c, carry):
        new = []
        for hh in range(2):
            hr, hi = carry[2 * hh], carry[2 * hh + 1]
            lr, li = lam[hh]
            s_r = st[hh, pl.ds(c, 1), 0:hs]
            s_i = st[hh, pl.ds(c, 1), hs:2 * hs]
            st[hh, pl.ds(c, 1), 0:hs] = hr
            st[hh, pl.ds(c, 1), hs:2 * hs] = hi
            new.append(lr * hr - li * hi + s_r)
            new.append(lr * hi + li * hr + s_i)
        return tuple(new)

    init = tuple(hc[hh, :, ri * hs:(ri + 1) * hs] for hh in range(2) for ri in range(2))
    fin = lax.fori_loop(0, tr, body, init)
    for hh in range(2):
        for ri in range(2):
            hc[hh, :, ri * hs:(ri + 1) * hs] = fin[2 * hh + ri]

    for hh in range(2):
        y = _dot(half_input(hh), toep_ref[hh]) + _dot(st[hh].astype(BF16), cpow_ref[hh])
        for i in range(ls):
            y_ref[hh, pl.ds(i, tr, stride=ls), :] = y[:, i * LANES:(i + 1) * LANES]

    halves = lambda ref: jnp.concatenate([ref[hh] for hh in range(S5_WIDTH // LANES)], axis=1)
    y5 = halves(y_ref) + s5d_ref[...] * halves(u_ref)
    gl = jax.nn.gelu(y5, approximate=True)
    gate = _dot(gl.astype(BF16), wglu_ref[...]) + bglu_ref[...]
    o_ref[...] = (gl * _sigmoid(gate)).astype(o_ref.dtype)


def _s5_scan(u5, tc, bc, cc, lam_l, layer, s5d, wglu, bglu, ls, nc, tr):
    nh, t, w = u5.shape
    rows = tr * ls
    wide = ls * LANES
    kern = functools.partial(_s5_kernel, ls=ls, tiles_per_seq=nc // tr)
    row = lambda i: (0, i, 0)
    return pl.pallas_call(
        kern,
        grid=(t // rows,),
        in_specs=[
            pl.BlockSpec((nh, rows, w), row),
            _layer_spec(tc.shape, layer),
            _layer_spec(bc.shape, layer),
            _layer_spec(cc.shape, layer),
            _layer_spec(lam_l.shape, layer),
            _const_spec(s5d.shape),
            _const_spec(wglu.shape),
            _const_spec(bglu.shape),
        ],
        out_specs=pl.BlockSpec((rows, S5_WIDTH), lambda i: (i, 0)),
        out_shape=jax.ShapeDtypeStruct((t, S5_WIDTH), BF16),
        scratch_shapes=[
            pltpu.VMEM((nh, wide, wide), BF16),
            pltpu.VMEM((nh, wide, 2 * S5_HSTATE), BF16),
            pltpu.VMEM((nh, 2 * S5_HSTATE, wide), BF16),
            pltpu.VMEM((nh, tr, 2 * S5_HSTATE), F32),
            pltpu.VMEM((nh, 1, 2 * S5_HSTATE), F32),
            pltpu.VMEM((nh, rows, w), F32),
        ],
        compiler_params=pltpu.CompilerParams(
            dimension_semantics=("arbitrary",), vmem_limit_bytes=VMEM_LIMIT),
        name="s5_mixer",
    )(u5, tc, bc, cc, lam_l, s5d, wglu, bglu)


def _s5_operators(lam_re, lam_im, log_dt, b_re, b_im, c_re, c_im, ls):
    hp = lax.Precision.HIGHEST
    dt = jnp.exp(log_dt)[:, None]
    zr, zi = lam_re * dt, lam_im * dt
    er = jnp.exp(zr)
    lbr, lbi = er * jnp.cos(zi), er * jnp.sin(zi)
    den = lam_re * lam_re + lam_im * lam_im
    nr, ni = lbr - 1.0, lbi
    fr = (nr * lam_re + ni * lam_im) / den
    fi = (ni * lam_re - nr * lam_im) / den
    bbr = fr[..., None] * b_re - fi[..., None] * b_im
    bbi = fr[..., None] * b_im + fi[..., None] * b_re
    tau = jnp.arange(ls + 1, dtype=F32)
    pe = jnp.exp(zr[..., None] * tau)
    pr, pi_ = pe * jnp.cos(zi[..., None] * tau), pe * jnp.sin(zi[..., None] * tau)
    cpr = c_re[:, :, :, None] * pr[:, None, :, :ls] - c_im[:, :, :, None] * pi_[:, None, :, :ls]
    cpi = c_re[:, :, :, None] * pi_[:, None, :, :ls] + c_im[:, :, :, None] * pr[:, None, :, :ls]
    kk = (jnp.einsum('ghpt,gpk->gthk', cpr, bbr, precision=hp)
          - jnp.einsum('ghpt,gpk->gthk', cpi, bbi, precision=hp))
    lag = jnp.arange(ls)[None, :] - jnp.arange(ls)[:, None]
    kt = kk[:, jnp.clip(lag, 0, ls - 1)]
    kt = jnp.where((lag >= 0)[None, :, :, None, None], kt, 0.0)
    nh, gh = S5_GROUPS // S5_HALF, S5_HALF
    w = ls * LANES
    kt = kt.reshape(nh, gh, ls, ls, S5_GROUP, S5_GROUP)
    tc = kt.transpose(0, 2, 1, 5, 3, 4).reshape(nh, w, ls * S5_GROUP)
    tau_rev = (ls - 1) - jnp.arange(ls, dtype=F32)
    pe_rev = jnp.exp(zr[..., None] * tau_rev)
    rev_r, rev_i = pe_rev * jnp.cos(zi[..., None] * tau_rev), pe_rev * jnp.sin(zi[..., None] * tau_rev)
    bre = rev_r[..., None] * bbr[:, :, None, :] - rev_i[..., None] * bbi[:, :, None, :]
    bim = rev_r[..., None] * bbi[:, :, None, :] + rev_i[..., None] * bbr[:, :, None, :]
    bb = jnp.stack([bre, bim], axis=0).reshape(2, nh, gh, S5_STATE, ls, S5_GROUP)
    bc = bb.transpose(1, 4, 2, 5, 0, 3).reshape(nh, w, 2 * S5_STATE)
    qr = c_re[:, :, :, None] * pr[:, None, :, 1:] - c_im[:, :, :, None] * pi_[:, None, :, 1:]
    qi = c_re[:, :, :, None] * pi_[:, None, :, 1:] + c_im[:, :, :, None] * pr[:, None, :, 1:]
    cc = jnp.stack([qr, -qi], axis=0).reshape(2, nh, gh, S5_GROUP, S5_STATE, ls)
    cc = cc.transpose(1, 0, 2, 4, 5, 3).reshape(nh, 2 * S5_HSTATE, ls * S5_GROUP)
    lam_l = jnp.stack([pr[:, :, ls].reshape(nh, S5_HSTATE), pi_[:, :, ls].reshape(nh, S5_HSTATE)], axis=1)
    return tc.astype(BF16), bc.astype(BF16), cc.astype(BF16), lam_l.astype(F32)


def _mix_reset(cbuf, mbuf, hst):
    cbuf[0:HIST, :] = jnp.zeros((HIST, SC_WIDTH), F32)
    mbuf[0:HIST, :] = jnp.zeros((HIST, M2_XBC), F32)
    hst[...] = jnp.zeros(hst.shape, F32)


def _mix_chunk(rest_ref, r0, L, scw_ref, mcw_ref, mcb_ref, dtb_ref, alog_ref, dvec_ref, ng_ref,
               o_ref, cbuf, mbuf, hst, ybuf):
    rows = slice(r0, r0 + L)
    u = rest_ref[rows, R_SC + SC_WIDTH:R_SC + 2 * SC_WIDTH] * rest_ref[rows, R_SC + 2 * SC_WIDTH:R_SC + 3 * SC_WIDTH]
    cbuf[HIST:HIST + L, :] = u
    conv = (scw_ref[2:3, :] * u + scw_ref[1:2, :] * cbuf[HIST - 1:HIST - 1 + L, :]
            + scw_ref[0:1, :] * cbuf[HIST - 2:HIST - 2 + L, :])
    cbuf[0:HIST, :] = cbuf[L:L + HIST, :]
    o_ref[rows, 0:SC_WIDTH] = (rest_ref[rows, R_SC:R_SC + SC_WIDTH] * conv).astype(o_ref.dtype)

    xr = rest_ref[rows, R_XBC:R_XBC + M2_XBC]
    mbuf[HIST:HIST + L, :] = xr
    conv = (mcw_ref[3:4, :] * xr + mcw_ref[2:3, :] * mbuf[HIST - 1:HIST - 1 + L, :]
            + mcw_ref[1:2, :] * mbuf[HIST - 2:HIST - 2 + L, :]
            + mcw_ref[0:1, :] * mbuf[HIST - 3:HIST - 3 + L, :] + mcb_ref[...])
    mbuf[0:HIST, :] = mbuf[L:L + HIST, :]
    xbc = _silu(conv)
    xs = xbc[:, 0:M2_WIDTH]
    dtr = rest_ref[rows, R_DT:R_DT + LANES] + dtb_ref[...]
    dtv = jnp.maximum(dtr, 0.0) + jnp.log(1.0 + jnp.exp(-jnp.abs(dtr)))
    a = -jnp.exp(alog_ref[...]) * dtv
    r_i = lax.broadcasted_iota(jnp.int32, (L, L), 0)
    c_i = lax.broadcasted_iota(jnp.int32, (L, L), 1)
    tril = c_i <= r_i
    tri = jnp.where(tril, 1.0, 0.0).astype(BF16)
    cs = sum(_dot(tri, part) for part in _split_bf16(a, 3))
    cs_t = cs.T
    cs_last = cs[L - 1:L, :]
    ecs = jnp.exp(cs)
    dec = jnp.exp(cs_last - cs)
    ecl = jnp.exp(cs_last)
    lane = lax.broadcasted_iota(jnp.int32, (1, LANES), 1)
    for g in range(M2_GROUPS):
        bg = xbc[:, M2_WIDTH + g * M2_STATE:M2_WIDTH + (g + 1) * M2_STATE].astype(BF16)
        cg = xbc[:, M2_WIDTH + (M2_GROUPS + g) * M2_STATE:M2_WIDTH + (M2_GROUPS + g + 1) * M2_STATE].astype(BF16)
        gram = _dot_nt(cg, bg)
        hs = hst[g]
        yoff = _dot(cg, hs.astype(BF16))
        xdec = []
        for hh in range(2):
            h = 2 * g + hh
            seg = cs[:, h:h + 1] - cs_t[h:h + 1, :]
            dm = jnp.exp(jnp.where(tril, seg, -jnp.inf))
            xdt = xs[:, h * M2_HEAD_DIM:(h + 1) * M2_HEAD_DIM] * dtv[:, h:h + 1]
            yd = _dot((gram * dm).astype(BF16), xdt.astype(BF16))
            yo = yoff[:, hh * M2_HEAD_DIM:(hh + 1) * M2_HEAD_DIM] * ecs[:, h:h + 1]
            ybuf[:, h * M2_HEAD_DIM:(h + 1) * M2_HEAD_DIM] = yd + yo
            xdec.append(xdt * dec[:, h:h + 1])
        upd = _dot_tn(bg, jnp.concatenate(xdec, axis=1).astype(BF16))
        keep = jnp.where(lane < M2_HEAD_DIM, ecl[:, 2 * g:2 * g + 1], ecl[:, 2 * g + 1:2 * g + 2])
        hst[g] = hs * keep + upd
    y = ybuf[...] + dvec_ref[...] * xs
    yg = y * _silu(rest_ref[rows, R_Z:R_Z + M2_WIDTH])
    ms = jnp.mean(yg * yg, axis=-1, keepdims=True)
    o_ref[rows, SC_WIDTH:SC_WIDTH + M2_WIDTH] = (yg * lax.rsqrt(ms + EPS) * ng_ref[...]).astype(o_ref.dtype)


def _ffn_kernel(x_ref, oa_ref, obd_ref, oc_ref, wo_ref, g2_ref, wg_ref, wu_ref, cw_ref, wd_ref,
                o_ref, hist, cb, *, tiles_per_seq, tf):
    tm = x_ref.shape[0]

    @pl.when(pl.program_id(0) % tiles_per_seq == 0)
    def _():
        hist[...] = jnp.zeros(hist.shape, F32)

    r_b = DA_WIDTH
    r_c = r_b + SC_WIDTH
    r_d = r_c + S5_WIDTH
    x1 = (x_ref[...] + _dot(oa_ref[...], wo_ref[0:r_b, :])
          + _dot(obd_ref[:, 0:SC_WIDTH], wo_ref[r_b:r_c, :])
          + _dot(oc_ref[...], wo_ref[r_c:r_d, :])
          + _dot(obd_ref[:, SC_WIDTH:SC_WIDTH + M2_WIDTH], wo_ref[r_d:D_MIX, :]))
    ms = jnp.mean(x1 * x1, axis=-1, keepdims=True)
    h2 = (x1 * lax.rsqrt(ms + EPS) * g2_ref[...]).astype(BF16)
    o_ref[...] = x1
    for c in range(D_FF // tf):
        sl = slice(c * tf, (c + 1) * tf)
        gpre = _dot(h2, wg_ref[:, sl])
        cb[0:HIST, :] = hist[:, sl]
        cb[HIST:HIST + tm, :] = gpre
        gc = (cw_ref[2:3, sl] * gpre + cw_ref[1:2, sl] * cb[HIST - 1:HIST - 1 + tm, :]
              + cw_ref[0:1, sl] * cb[HIST - 2:HIST - 2 + tm, :])
        hist[:, sl] = cb[tm:tm + HIST, :]
        act = (_silu(gc) * _dot(h2, wu_ref[:, sl])).astype(BF16)
        o_ref[...] += _dot(act, wd_ref[sl, :])


def _outproj_ffn(x2, oa, obd, oc, wo, g2, wg, wu, cw, wd, layer, seq, tm, tf):
    t = x2.shape[0]
    kern = functools.partial(_ffn_kernel, tiles_per_seq=seq // tm, tf=tf)
    row = lambda i: (i, 0)
    return pl.pallas_call(
        kern,
        grid=(t // tm,),
        in_specs=[
            pl.BlockSpec((tm, D_MODEL), row),
            pl.BlockSpec((tm, DA_WIDTH), row),
            pl.BlockSpec((tm, SC_WIDTH + M2_WIDTH), row),
            pl.BlockSpec((tm, S5_WIDTH), row),
            _layer_spec(wo.shape, layer),
            _const_spec((1, D_MODEL)),
            _layer_spec(wg.shape, layer),
            _layer_spec(wu.shape, layer),
            _const_spec((3, D_FF)),
            _layer_spec(wd.shape, layer),
        ],
        out_specs=pl.BlockSpec((tm, D_MODEL), row),
        out_shape=jax.ShapeDtypeStruct((t, D_MODEL), F32),
        scratch_shapes=[
            pltpu.VMEM((HIST, D_FF), F32),
            pltpu.VMEM((tm + HIST, tf), F32),
        ],
        compiler_params=pltpu.CompilerParams(
            dimension_semantics=("arbitrary",), vmem_limit_bytes=VMEM_LIMIT),
        name="outproj_ffn",
    )(x2, oa, obd, oc, wo, g2, wg, wu, cw, wd)


def _rope_lane_tables(seq):
    inv = 1.0 / (ROPE_THETA ** (jnp.arange(0, ROPE_DIM, 2, dtype=F32) / ROPE_DIM))
    ang = jnp.arange(seq, dtype=F32)[:, None] * inv[None, :]
    cos, sin = jnp.cos(ang), jnp.sin(ang)
    ones = jnp.ones((seq, DA_QK - ROPE_DIM), F32)
    cos64 = jnp.concatenate([cos, cos, ones], axis=1)
    sin64 = jnp.concatenate([-sin, sin, 0.0 * ones], axis=1)
    return jnp.tile(cos64, (1, LANES // DA_QK)), jnp.tile(sin64, (1, LANES // DA_QK))


def _forward(x, p, blk, mix_l, s5_tr):
    bsz, seq, _ = x.shape
    depth = p["w_in"].shape[0]
    t = bsz * seq
    ls = S5_CHUNK
    nc = seq // ls
    assert seq % blk == 0 and seq % mix_l == 0 and nc % s5_tr == 0

    cos_t, sin_t = _rope_lane_tables(seq)
    comp = lax.broadcasted_iota(jnp.int32, (QK_COLS, QK_COLS), 0) // DA_QK
    gm = (comp == comp.T).astype(BF16)
    w_in = p["w_in"].astype(BF16)
    w_dt = jnp.pad(p["w_in"][:, :, W_DT:], ((0, 0), (0, 0), (0, LANES - M2_HEADS))).astype(BF16)
    w_out = p["w_out"].astype(BF16)
    w_gate, w_up, w_down = (p[k].astype(BF16) for k in ("ffn_w_gate", "ffn_w_up", "ffn_w_down"))
    s5_ops = jax.vmap(functools.partial(_s5_operators, ls=ls))(
        p["s5_lam_re"], p["s5_lam_im"], p["s5_log_dt"], p["s5_b_re"], p["s5_b_im"],
        p["s5_c_re"], p["s5_c_im"])
    x2 = x.reshape(t, D_MODEL)
    for l in range(depth):
        lambda_init = 0.8 - 0.6 * math.exp(-0.3 * l)
        qkg = jnp.tile(p["qk_norm_g"][l], (1, LANES // DA_QK))
        pad4 = lambda a: jnp.pad(a, (0, LANES - M2_HEADS))[None, :]
        mix_params = (p["sc_conv_w"][l], p["m2_conv_w"][l], p["m2_conv_b"][l][None, :],
                      pad4(p["m2_dt_bias"][l]), pad4(p["m2_a_log"][l]),
                      jnp.repeat(p["m2_d"][l], M2_HEAD_DIM)[None, :], p["m2_norm_g"][l][None, :])
        qt, kb, vtb, obd, u5 = _inproj(x2, p["ln1_g"][l][None, :], w_in, w_dt, l, qkg, cos_t, sin_t,
                                       gm, mix_params, bsz, seq, blk, mix_l)
        oa = _attention(qt, kb, vtb, p["da_lambda"][l], p["subln_g"][l][None, :], lambda_init)
        oc = _s5_scan(u5, *s5_ops, l, p["s5_d"][l][None, :], p["s5_w_glu"][l].astype(BF16),
                      p["s5_b_glu"][l][None, :], ls, nc, s5_tr)
        x2 = _outproj_ffn(x2, oa.reshape(t, DA_WIDTH), obd, oc, w_out,
                          p["ln2_g"][l][None, :], w_gate, w_up, p["ffn_conv_w"][l], w_down,
                          l, seq, min(FFN_ROWS, seq), FFN_COLS)
    return x2.reshape(bsz, seq, D_MODEL)


def kernel(x, ln1_g, w_in, qk_norm_g, da_lambda, subln_g, sc_conv_w, s5_lam_re, s5_lam_im, s5_log_dt, s5_b_re, s5_b_im, s5_c_re, s5_c_im, s5_d, s5_w_glu, s5_b_glu, m2_conv_w, m2_conv_b, m2_dt_bias, m2_a_log, m2_d, m2_norm_g, w_out, ln2_g, ffn_w_gate, ffn_w_up, ffn_conv_w, ffn_w_down):
    params = dict(ln1_g=ln1_g, w_in=w_in, qk_norm_g=qk_norm_g, da_lambda=da_lambda, subln_g=subln_g,
                  sc_conv_w=sc_conv_w, s5_lam_re=s5_lam_re, s5_lam_im=s5_lam_im, s5_log_dt=s5_log_dt,
                  s5_b_re=s5_b_re, s5_b_im=s5_b_im, s5_c_re=s5_c_re, s5_c_im=s5_c_im, s5_d=s5_d,
                  s5_w_glu=s5_w_glu, s5_b_glu=s5_b_glu, m2_conv_w=m2_conv_w, m2_conv_b=m2_conv_b,
                  m2_dt_bias=m2_dt_bias, m2_a_log=m2_a_log, m2_d=m2_d, m2_norm_g=m2_norm_g,
                  w_out=w_out, ln2_g=ln2_g, ffn_w_gate=ffn_w_gate, ffn_w_up=ffn_w_up,
                  ffn_conv_w=ffn_conv_w, ffn_w_down=ffn_w_down)
    seq = x.shape[1]
    return _forward(x, params, blk=min(512, seq), mix_l=min(256, seq),
                    s5_tr=min(256, seq // S5_CHUNK))
```

```python
import functools
import math

import jax
import jax.numpy as jnp
from jax import lax
from jax.experimental import pallas as pl
from jax.experimental.pallas import tpu as pltpu

F32 = jnp.float32
BF16 = jnp.bfloat16

D_MODEL = 1024
DA_HEADS = 4
DA_QK = 64
DA_V = 2 * DA_QK
DA_VX = DA_V + 16
DA_WIDTH = DA_HEADS * DA_V
ROPE_DIM = DA_QK // 4
ROPE_THETA = 500000.0
SC_WIDTH = 256
S5_WIDTH = 256
S5_GROUP = 16
S5_GROUPS = S5_WIDTH // S5_GROUP
S5_STATE = 64
M2_HEADS = 4
M2_HEAD_DIM = 64
M2_WIDTH = M2_HEADS * M2_HEAD_DIM
M2_GROUPS = 2
M2_STATE = 128
M2_XBC = M2_WIDTH + 2 * M2_GROUPS * M2_STATE
D_MIX = DA_WIDTH + SC_WIDTH + S5_WIDTH + M2_WIDTH
D_FF = 2816
EPS = 1e-6
LOG2E = 1.4426950408889634

LANES = 128
HIST = 8
VMEM_LIMIT = 56 * 1024 * 1024

W_Q = 0
W_K = W_Q + DA_WIDTH
W_V = W_K + DA_WIDTH
W_SC = W_V + DA_WIDTH
W_S5 = W_SC + 3 * SC_WIDTH
W_Z = W_S5 + S5_WIDTH
W_DT = W_Z + M2_WIDTH + M2_XBC
D_PROJ = W_DT + M2_HEADS
R_SC = 0
R_Z = R_SC + 3 * SC_WIDTH
R_XBC = R_Z + M2_WIDTH
R_DT = R_XBC + M2_XBC
D_REST = R_DT + LANES
QK_COLS = 2 * LANES
ATT_HEADS_PER_STEP = 2
FFN_ROWS = 512
FFN_COLS = 2816

S5_CHUNK = 8
S5_HALF = LANES // S5_GROUP
S5_HSTATE = S5_HALF * S5_STATE


def _dot(a, b):
    return jnp.dot(a, b, preferred_element_type=F32)


def _dot_nt(a, b):
    return lax.dot_general(a, b, (((1,), (1,)), ((), ())), preferred_element_type=F32)


def _dot_tn(a, b):
    return lax.dot_general(a, b, (((0,), (0,)), ((), ())), preferred_element_type=F32)


def _split_bf16(x, parts):
    out = []
    r = x
    for _ in range(parts):
        p = r.astype(BF16)
        out.append(p)
        r = r - p.astype(F32)
    return out


def _sigmoid(x):
    return 0.5 + 0.5 * jnp.tanh(0.5 * x)


def _silu(x):
    return x * _sigmoid(x)


def _const_spec(shape):
    nd = len(shape)
    return pl.BlockSpec(shape, lambda *_: (0,) * nd, pipeline_mode=pl.Buffered(1))


def _layer_spec(shape, layer):
    nd = len(shape)
    return pl.BlockSpec((None,) + tuple(shape[1:]), lambda *_: (layer,) + (0,) * (nd - 1),
                        pipeline_mode=pl.Buffered(1))


def _inproj_kernel(x_ref, g1_ref, wf_ref, qkg_ref, cos_ref, sin_ref, gm_ref,
                   scw_ref, mcw_ref, mcb_ref, dtb_ref, alog_ref, dvec_ref, ng_ref,
                   qt_ref, kb_ref, vt_ref, obd_ref, u5_ref,
                   w_ref, wdt_ref, rest_ref, cbuf, mbuf, hst, ybuf, *, tiles_per_seq, mix_l):
    tm = x_ref.shape[0]

    @pl.when(pl.program_id(0) == 0)
    def _():
        step = 4 * LANES
        for c0 in range(0, W_DT, step):
            w_ref[:, c0:c0 + step] = wf_ref[:, c0:c0 + step].astype(BF16)
        wdt_ref[...] = jnp.zeros(wdt_ref.shape, BF16)
        wdt_ref[:, 0:M2_HEADS] = wf_ref[:, W_DT:D_PROJ].astype(BF16)

    @pl.when(pl.program_id(0) % tiles_per_seq == 0)
    def _():
        _mix_reset(cbuf, mbuf, hst)

    x = x_ref[...]
    ms = jnp.mean(x * x, axis=-1, keepdims=True)
    hn = (x * lax.rsqrt(ms + EPS) * g1_ref[...]).astype(BF16)
    rest_ref[:, R_SC:R_Z] = _dot(hn, w_ref[:, W_SC:W_S5])
    rest_ref[:, R_Z:R_DT] = _dot(hn, w_ref[:, W_Z:W_DT])
    rest_ref[:, R_DT:D_REST] = _dot(hn, wdt_ref[...])
    for r0 in range(0, tm, mix_l):
        _mix_chunk(rest_ref, r0, mix_l, scw_ref, mcw_ref, mcb_ref, dtb_ref, alog_ref, dvec_ref, ng_ref,
                   obd_ref, cbuf, mbuf, hst, ybuf)
    cosv = cos_ref[...]
    sinv = sin_ref[...]
    gm = gm_ref[...]
    lane = lax.broadcasted_iota(jnp.int32, (tm, LANES), 1)
    pair_up = (lane % DA_QK) < (ROPE_DIM // 2)
    heads_per_dot = QK_COLS // DA_V

    def qk_heads(col0, gvec, scale):
        y = _dot(hn, w_ref[:, col0:col0 + QK_COLS])
        hi, lo = _split_bf16(y * y, 2)
        ss = _dot(hi, gm) + _dot(lo, gm)
        yn = y * lax.rsqrt(ss * (1.0 / DA_QK) + EPS)
        out = []
        for j in range(heads_per_dot):
            yh = yn[:, j * DA_V:(j + 1) * DA_V] * gvec
            partner = jnp.where(pair_up,
                                pltpu.roll(yh, LANES - ROPE_DIM // 2, 1),
                                pltpu.roll(yh, ROPE_DIM // 2, 1))
            out.append((yh * cosv + partner * sinv) * scale)
        return out

    for h0 in range(0, DA_HEADS, heads_per_dot):
        qs = qk_heads(W_Q + h0 * DA_V, qkg_ref[0:1, :], DA_QK ** -0.5 * LOG2E)
        ks = qk_heads(W_K + h0 * DA_V, qkg_ref[1:2, :], 1.0)
        v = _dot(hn, w_ref[:, W_V + h0 * DA_V:W_V + h0 * DA_V + QK_COLS])
        for j in range(heads_per_dot):
            qt_ref[h0 + j] = qs[j].T.astype(BF16)
            kb_ref[h0 + j] = ks[j].astype(BF16)
            vt_ref[h0 + j, 0:DA_V, :] = v[:, j * DA_V:(j + 1) * DA_V].T.astype(BF16)
            vt_ref[h0 + j, DA_V:DA_VX, :] = jnp.ones((DA_VX - DA_V, tm), BF16)
    u5 = _dot(hn, w_ref[:, W_S5:W_Z])
    for hh in range(S5_WIDTH // LANES):
        u5_ref[hh] = u5[:, hh * LANES:(hh + 1) * LANES]


def _inproj(x2, g1, w, layer, qkg, cos_t, sin_t, gm, mix_params, bsz, seq, tm, mix_l):
    t = x2.shape[0]
    nt = seq // tm
    row = lambda i: (i, 0)
    tab = lambda i: (i % nt, 0)
    kern = functools.partial(_inproj_kernel, tiles_per_seq=nt, mix_l=mix_l)
    return pl.pallas_call(
        kern,
        grid=(t // tm,),
        in_specs=[
            pl.BlockSpec((tm, D_MODEL), row),
            _const_spec((1, D_MODEL)),
            _layer_spec(w.shape, layer),
            _const_spec((2, LANES)),
            pl.BlockSpec((tm, LANES), tab),
            pl.BlockSpec((tm, LANES), tab),
            _const_spec((QK_COLS, QK_COLS)),
        ] + [_const_spec(a.shape) for a in mix_params],
        out_specs=[
            pl.BlockSpec((None, DA_HEADS, DA_V, tm), lambda i: (i // nt, 0, 0, i % nt)),
            pl.BlockSpec((None, DA_HEADS, None, tm, DA_V), lambda i: (i // nt, 0, i % nt, 0, 0)),
            pl.BlockSpec((None, DA_HEADS, None, DA_VX, tm), lambda i: (i // nt, 0, i % nt, 0, 0)),
            pl.BlockSpec((tm, SC_WIDTH + M2_WIDTH), row),
            pl.BlockSpec((S5_WIDTH // LANES, tm, LANES), lambda i: (0, i, 0)),
        ],
        out_shape=[
            jax.ShapeDtypeStruct((bsz, DA_HEADS, DA_V, seq), BF16),
            jax.ShapeDtypeStruct((bsz, DA_HEADS, nt, tm, DA_V), BF16),
            jax.ShapeDtypeStruct((bsz, DA_HEADS, nt, DA_VX, tm), BF16),
            jax.ShapeDtypeStruct((t, SC_WIDTH + M2_WIDTH), BF16),
            jax.ShapeDtypeStruct((S5_WIDTH // LANES, t, LANES), F32),
        ],
        scratch_shapes=[
            pltpu.VMEM((D_MODEL, W_DT), BF16),
            pltpu.VMEM((D_MODEL, LANES), BF16),
            pltpu.VMEM((tm, D_REST), F32),
            pltpu.VMEM((mix_l + HIST, SC_WIDTH), F32),
            pltpu.VMEM((mix_l + HIST, M2_XBC), F32),
            pltpu.VMEM((M2_GROUPS, M2_STATE, 2 * M2_HEAD_DIM), F32),
            pltpu.VMEM((mix_l, M2_WIDTH), F32),
        ],
        compiler_params=pltpu.CompilerParams(
            dimension_semantics=("arbitrary",), vmem_limit_bytes=VMEM_LIMIT),
        name="inproj_mix",
    )(x2, g1, w, qkg, cos_t, sin_t, gm, *mix_params)


def _attn_kernel(qt_ref, k_ref, vt_ref, lam_ref, sg_ref, o_ref, qz_sc, s_sc, m_sc, acc_sc,
                 *, lambda_init):
    qi = pl.program_id(2)
    heads = qt_ref.shape[0]
    streams = [(hd, c) for hd in range(heads) for c in range(2)]
    for n, (hd, c) in enumerate(streams):
        qt = qt_ref[hd]
        comp = lax.broadcasted_iota(jnp.int32, qt.shape, 0) // DA_QK
        qz_sc[n] = jnp.where(comp == c, qt, jnp.zeros_like(qt))
    m_sc[...] = jnp.full(m_sc.shape, -1e30, F32)
    acc_sc[...] = jnp.zeros(acc_sc.shape, F32)

    def scores(j, buf):
        for n, (hd, c) in enumerate(streams):
            s_sc[buf, n] = _dot(k_ref[hd, j], qz_sc[n])

    def absorb(j, buf, masked):
        for n, (hd, c) in enumerate(streams):
            s = s_sc[buf, n]
            if masked:
                kpos = lax.broadcasted_iota(jnp.int32, s.shape, 0)
                qpos = lax.broadcasted_iota(jnp.int32, s.shape, 1)
                s = jnp.where(kpos <= qpos, s, -jnp.inf)
            m_prev = m_sc[n]
            m_new = jnp.maximum(m_prev, jnp.max(s, axis=0, keepdims=True))
            alpha = jnp.exp2(m_prev - m_new)
            p = jnp.exp2(s - m_new).astype(BF16)
            acc_sc[n] = alpha * acc_sc[n] + _dot(vt_ref[hd, j], p)
            m_sc[n] = m_new

    scores(0, 0)
    pairs = qi // 2

    def body(t, carry):
        j = 2 * t
        scores(j + 1, 1)
        absorb(j, 0, False)
        scores(j + 2, 0)
        absorb(j + 1, 1, False)
        return carry

    lax.fori_loop(0, pairs, body, 0)
    j0 = 2 * pairs

    @pl.when(j0 == qi)
    def _():
        absorb(qi, 0, True)

    @pl.when(j0 != qi)
    def _():
        scores(qi, 1)
        absorb(j0, 0, False)
        absorb(qi, 1, True)

    lp = lam_ref[...]
    lam = (jnp.exp(jnp.sum(lp[0:1] * lp[1:2], axis=1, keepdims=True))
           - jnp.exp(jnp.sum(lp[2:3] * lp[3:4], axis=1, keepdims=True)) + lambda_init)
    for hd in range(heads):
        a0, a1 = acc_sc[2 * hd], acc_sc[2 * hd + 1]
        ot = (a0[0:DA_V] / a0[DA_V:DA_V + 1] - lam * (a1[0:DA_V] / a1[DA_V:DA_V + 1]))
        ms = jnp.mean(ot * ot, axis=0, keepdims=True)
        o = (ot * lax.rsqrt(ms + EPS)).T * (sg_ref[...] * (1.0 - lambda_init))
        o_ref[:, hd * DA_V:(hd + 1) * DA_V] = o.astype(o_ref.dtype)


def _attention(qt, kb, vtb, lam_p, sub_g, lambda_init):
    bsz, _, _, seq = qt.shape
    nb, blk = kb.shape[2], kb.shape[3]
    hps = ATT_HEADS_PER_STEP
    kern = functools.partial(_attn_kernel, lambda_init=lambda_init)
    return pl.pallas_call(
        kern,
        grid=(bsz, DA_HEADS // hps, nb),
        in_specs=[
            pl.BlockSpec((None, hps, DA_V, blk), lambda b, h, i: (b, h, 0, i)),
            pl.BlockSpec((None, hps, nb, blk, DA_V), lambda b, h, i: (b, h, 0, 0, 0)),
            pl.BlockSpec((None, hps, nb, DA_VX, blk), lambda b, h, i: (b, h, 0, 0, 0)),
            pl.BlockSpec((4, DA_QK), lambda b, h, i: (0, 0)),
            pl.BlockSpec((1, DA_V), lambda b, h, i: (0, 0)),
        ],
        out_specs=pl.BlockSpec((None, blk, hps * DA_V), lambda b, h, i: (b, i, h)),
        out_shape=jax.ShapeDtypeStruct((bsz, seq, DA_WIDTH), BF16),
        scratch_shapes=[
            pltpu.VMEM((2 * hps, DA_V, blk), BF16),
            pltpu.VMEM((2, 2 * hps, blk, blk), F32),
            pltpu.VMEM((2 * hps, 1, blk), F32),
            pltpu.VMEM((2 * hps, DA_VX, blk), F32),
        ],
        compiler_params=pltpu.CompilerParams(
            dimension_semantics=("arbitrary", "arbitrary", "arbitrary"),
            vmem_limit_bytes=VMEM_LIMIT),
        name="diff_attn",
    )(qt, kb, vtb, lam_p, sub_g)


def _expand_blockdiag(src_ref, dst_ref, row_group, col_div, col_inner):
    rows, wide = dst_ref.shape[1], dst_ref.shape[2]
    r_s = lax.broadcasted_iota(jnp.int32, (LANES, wide), 0)
    c_s = lax.broadcasted_iota(jnp.int32, (LANES, wide), 1)
    sel = jnp.where(r_s == (c_s // col_div) * col_inner + c_s % col_inner, 1.0, 0.0).astype(BF16)
    step = 256
    r_i = lax.broadcasted_iota(jnp.int32, (step, wide), 0)
    c_i = lax.broadcasted_iota(jnp.int32, (step, wide), 1)
    col_group = (c_i // (col_div // S5_HALF)) % S5_HALF
    for hh in range(dst_ref.shape[0]):
        for r0 in range(0, rows, step):
            same = ((r_i + r0) // row_group) % S5_HALF == col_group
            wide_blk = _dot(src_ref[hh, r0:r0 + step, :], sel)
            dst_ref[hh, r0:r0 + step, :] = jnp.where(same, wide_blk, 0.0).astype(BF16)


def _s5_kernel(u_ref, tc_ref, bc_ref, cc_ref, lam_ref, s5d_ref, wglu_ref, bglu_ref, o_ref,
               toep_ref, bpow_ref, cpow_ref, st, hc, y_ref, *, ls, tiles_per_seq):
    tr = u_ref.shape[1] // ls
    hs = S5_HSTATE

    @pl.when(pl.program_id(0) == 0)
    def _():
        _expand_blockdiag(tc_ref, toep_ref, S5_GROUP, LANES, S5_GROUP)
        _expand_blockdiag(bc_ref, bpow_ref, S5_GROUP, S5_HSTATE, S5_STATE)
        _expand_blockdiag(cc_ref, cpow_ref, S5_STATE, LANES, S5_GROUP)

    @pl.when(pl.program_id(0) % tiles_per_seq == 0)
    def _():
        hc[...] = jnp.zeros(hc.shape, F32)

    def half_input(hh):
        cols = [u_ref[hh, pl.ds(i, tr, stride=ls), :] for i in range(ls)]
        return jnp.concatenate(cols, axis=1).astype(BF16)

    for hh in range(2):
        st[hh] = _dot(half_input(hh), bpow_ref[hh])

    lam = [(lam_ref[hh, 0:1, :], lam_ref[hh, 1:2, :]) for hh in range(2)]

    def body(c, carry):
        new = []
        for hh in range(2):
            hr, hi = carry[2 * hh], carry[2 * hh + 1]
            lr, li = lam[hh]
            s_r = st[hh, pl.ds(c, 1), 0:hs]
            s_i = st[hh, pl.ds(c, 1), hs:2 * hs]
            st[hh, pl.ds(c, 1), 0:hs] = hr
            st[hh, pl.ds(c, 1), hs:2 * hs] = hi
            new.append(lr * hr - li * hi + s_r)
            new.append(lr * hi + li * hr + s_i)
        return tuple(new)

    init = tuple(hc[hh, :, ri * hs:(ri + 1) * hs] for hh in range(2) for ri in range(2))
    fin = lax.fori_loop(0, tr, body, init)
    for hh in range(2):
        for ri in range(2):
            hc[hh, :, ri * hs:(ri + 1) * hs] = fin[2 * hh + ri]

    for hh in range(2):
        y = _dot(half_input(hh), toep_ref[hh]) + _dot(st[hh].astype(BF16), cpow_ref[hh])
        for i in range(ls):
            y_ref[hh, pl.ds(i, tr, stride=ls), :] = y[:, i * LANES:(i + 1) * LANES]

    halves = lambda ref: jnp.concatenate([ref[hh] for hh in range(S5_WIDTH // LANES)], axis=1)
    y5 = halves(y_ref) + s5d_ref[...] * halves(u_ref)
    gl = jax.nn.gelu(y5, approximate=True)
    gate = _dot(gl.astype(BF16), wglu_ref[...]) + bglu_ref[...]
    o_ref[...] = (gl * _sigmoid(gate)).astype(o_ref.dtype)


def _s5_scan(u5, tc, bc, cc, lam_l, layer, s5d, wglu, bglu, ls, nc, tr):
    nh, t, w = u5.shape
    rows = tr * ls
    wide = ls * LANES
    kern = functools.partial(_s5_kernel, ls=ls, tiles_per_seq=nc // tr)
    row = lambda i: (0, i, 0)
    return pl.pallas_call(
        kern,
        grid=(t // rows,),
        in_specs=[
            pl.BlockSpec((nh, rows, w), row),
            _layer_spec(tc.shape, layer),
            _layer_spec(bc.shape, layer),
            _layer_spec(cc.shape, layer),
            _layer_spec(lam_l.shape, layer),
            _const_spec(s5d.shape),
            _const_spec(wglu.shape),
            _const_spec(bglu.shape),
        ],
        out_specs=pl.BlockSpec((rows, S5_WIDTH), lambda i: (i, 0)),
        out_shape=jax.ShapeDtypeStruct((t, S5_WIDTH), BF16),
        scratch_shapes=[
            pltpu.VMEM((nh, wide, wide), BF16),
            pltpu.VMEM((nh, wide, 2 * S5_HSTATE), BF16),
            pltpu.VMEM((nh, 2 * S5_HSTATE, wide), BF16),
            pltpu.VMEM((nh, tr, 2 * S5_HSTATE), F32),
            pltpu.VMEM((nh, 1, 2 * S5_HSTATE), F32),
            pltpu.VMEM((nh, rows, w), F32),
        ],
        compiler_params=pltpu.CompilerParams(
            dimension_semantics=("arbitrary",), vmem_limit_bytes=VMEM_LIMIT),
        name="s5_mixer",
    )(u5, tc, bc, cc, lam_l, s5d, wglu, bglu)


def _s5_operators(lam_re, lam_im, log_dt, b_re, b_im, c_re, c_im, ls):
    hp = lax.Precision.HIGHEST
    dt = jnp.exp(log_dt)[:, None]
    zr, zi = lam_re * dt, lam_im * dt
    er = jnp.exp(zr)
    lbr, lbi = er * jnp.cos(zi), er * jnp.sin(zi)
    den = lam_re * lam_re + lam_im * lam_im
    nr, ni = lbr - 1.0, lbi
    fr = (nr * lam_re + ni * lam_im) / den
    fi = (ni * lam_re - nr * lam_im) / den
    bbr = fr[..., None] * b_re - fi[..., None] * b_im
    bbi = fr[..., None] * b_im + fi[..., None] * b_re
    tau = jnp.arange(ls + 1, dtype=F32)
    pe = jnp.exp(zr[..., None] * tau)
    pr, pi_ = pe * jnp.cos(zi[..., None] * tau), pe * jnp.sin(zi[..., None] * tau)
    pr_t, pi_t = jnp.swapaxes(pr, 1, 2)[:, :ls, None, :], jnp.swapaxes(pi_, 1, 2)[:, :ls, None, :]
    cpr = c_re[:, None] * pr_t - c_im[:, None] * pi_t
    cpi = c_re[:, None] * pi_t + c_im[:, None] * pr_t
    kk = (jnp.einsum('gthp,gpk->gthk', cpr, bbr, precision=hp)
          - jnp.einsum('gthp,gpk->gthk', cpi, bbi, precision=hp))
    lag = jnp.arange(ls)[None, :] - jnp.arange(ls)[:, None]
    kt = kk[:, jnp.clip(lag, 0, ls - 1)]
    kt = jnp.where((lag >= 0)[None, :, :, None, None], kt, 0.0)
    nh, gh = S5_GROUPS // S5_HALF, S5_HALF
    w = ls * LANES
    kt = kt.reshape(nh, gh, ls, ls, S5_GROUP, S5_GROUP)
    tc = kt.transpose(0, 2, 1, 5, 3, 4).reshape(nh, w, ls * S5_GROUP)
    tau_rev = (ls - 1) - jnp.arange(ls, dtype=F32)
    pe_rev = jnp.exp(zr[..., None] * tau_rev)
    rev_r, rev_i = pe_rev * jnp.cos(zi[..., None] * tau_rev), pe_rev * jnp.sin(zi[..., None] * tau_rev)
    bre = rev_r[..., None] * bbr[:, :, None, :] - rev_i[..., None] * bbi[:, :, None, :]
    bim = rev_r[..., None] * bbi[:, :, None, :] + rev_i[..., None] * bbr[:, :, None, :]
    bb = jnp.stack([bre, bim], axis=0).reshape(2, nh, gh, S5_STATE, ls, S5_GROUP)
    bc = bb.transpose(1, 4, 2, 5, 0, 3).reshape(nh, w, 2 * S5_STATE)
    qr = c_re[:, :, :, None] * pr[:, None, :, 1:] - c_im[:, :, :, None] * pi_[:, None, :, 1:]
    qi = c_re[:, :, :, None] * pi_[:, None, :, 1:] + c_im[:, :, :, None] * pr[:, None, :, 1:]
    cc = jnp.stack([qr, -qi], axis=0).reshape(2, nh, gh, S5_GROUP, S5_STATE, ls)
    cc = cc.transpose(1, 0, 2, 4, 5, 3).reshape(nh, 2 * S5_HSTATE, ls * S5_GROUP)
    lam_l = jnp.stack([pr[:, :, ls].reshape(nh, S5_HSTATE), pi_[:, :, ls].reshape(nh, S5_HSTATE)], axis=1)
    return tc.astype(BF16), bc.astype(BF16), cc.astype(BF16), lam_l.astype(F32)


def _mix_reset(cbuf, mbuf, hst):
    cbuf[0:HIST, :] = jnp.zeros((HIST, SC_WIDTH), F32)
    mbuf[0:HIST, :] = jnp.zeros((HIST, M2_XBC), F32)
    hst[...] = jnp.zeros(hst.shape, F32)


def _mix_chunk(rest_ref, r0, L, scw_ref, mcw_ref, mcb_ref, dtb_ref, alog_ref, dvec_ref, ng_ref,
               o_ref, cbuf, mbuf, hst, ybuf):
    rows = slice(r0, r0 + L)
    u = rest_ref[rows, R_SC + SC_WIDTH:R_SC + 2 * SC_WIDTH] * rest_ref[rows, R_SC + 2 * SC_WIDTH:R_SC + 3 * SC_WIDTH]
    cbuf[HIST:HIST + L, :] = u
    conv = (scw_ref[2:3, :] * u + scw_ref[1:2, :] * cbuf[HIST - 1:HIST - 1 + L, :]
            + scw_ref[0:1, :] * cbuf[HIST - 2:HIST - 2 + L, :])
    cbuf[0:HIST, :] = cbuf[L:L + HIST, :]
    o_ref[rows, 0:SC_WIDTH] = (rest_ref[rows, R_SC:R_SC + SC_WIDTH] * conv).astype(o_ref.dtype)

    xr = rest_ref[rows, R_XBC:R_XBC + M2_XBC]
    mbuf[HIST:HIST + L, :] = xr
    conv = (mcw_ref[3:4, :] * xr + mcw_ref[2:3, :] * mbuf[HIST - 1:HIST - 1 + L, :]
            + mcw_ref[1:2, :] * mbuf[HIST - 2:HIST - 2 + L, :]
            + mcw_ref[0:1, :] * mbuf[HIST - 3:HIST - 3 + L, :] + mcb_ref[...])
    mbuf[0:HIST, :] = mbuf[L:L + HIST, :]
    xbc = _silu(conv)
    xs = xbc[:, 0:M2_WIDTH]
    dtr = rest_ref[rows, R_DT:R_DT + LANES] + dtb_ref[...]
    dtv = jnp.maximum(dtr, 0.0) + jnp.log(1.0 + jnp.exp(-jnp.abs(dtr)))
    a = -jnp.exp(alog_ref[...]) * dtv
    r_i = lax.broadcasted_iota(jnp.int32, (L, L), 0)
    c_i = lax.broadcasted_iota(jnp.int32, (L, L), 1)
    tril = c_i <= r_i
    tri = jnp.where(tril, 1.0, 0.0).astype(BF16)
    cs = sum(_dot(tri, part) for part in _split_bf16(a, 3))
    cs_t = cs.T
    cs_last = cs[L - 1:L, :]
    ecs = jnp.exp(cs)
    dec = jnp.exp(cs_last - cs)
    ecl = jnp.exp(cs_last)
    lane = lax.broadcasted_iota(jnp.int32, (1, LANES), 1)
    for g in range(M2_GROUPS):
        bg = xbc[:, M2_WIDTH + g * M2_STATE:M2_WIDTH + (g + 1) * M2_STATE].astype(BF16)
        cg = xbc[:, M2_WIDTH + (M2_GROUPS + g) * M2_STATE:M2_WIDTH + (M2_GROUPS + g + 1) * M2_STATE].astype(BF16)
        gram = _dot_nt(cg, bg)
        hs = hst[g]
        yoff = _dot(cg, hs.astype(BF16))
        xdec = []
        for hh in range(2):
            h = 2 * g + hh
            seg = cs[:, h:h + 1] - cs_t[h:h + 1, :]
            dm = jnp.exp(jnp.where(tril, seg, -jnp.inf))
            xdt = xs[:, h * M2_HEAD_DIM:(h + 1) * M2_HEAD_DIM] * dtv[:, h:h + 1]
            yd = _dot((gram * dm).astype(BF16), xdt.astype(BF16))
            yo = yoff[:, hh * M2_HEAD_DIM:(hh + 1) * M2_HEAD_DIM] * ecs[:, h:h + 1]
            ybuf[:, h * M2_HEAD_DIM:(h + 1) * M2_HEAD_DIM] = yd + yo
            xdec.append(xdt * dec[:, h:h + 1])
        upd = _dot_tn(bg, jnp.concatenate(xdec, axis=1).astype(BF16))
        keep = jnp.where(lane < M2_HEAD_DIM, ecl[:, 2 * g:2 * g + 1], ecl[:, 2 * g + 1:2 * g + 2])
        hst[g] = hs * keep + upd
    y = ybuf[...] + dvec_ref[...] * xs
    yg = y * _silu(rest_ref[rows, R_Z:R_Z + M2_WIDTH])
    ms = jnp.mean(yg * yg, axis=-1, keepdims=True)
    o_ref[rows, SC_WIDTH:SC_WIDTH + M2_WIDTH] = (yg * lax.rsqrt(ms + EPS) * ng_ref[...]).astype(o_ref.dtype)


def _ffn_kernel(x_ref, oa_ref, obd_ref, oc_ref, wo_ref, g2_ref, wg_ref, wu_ref, cw_ref, wd_ref,
                o_ref, hist, cb, *, tiles_per_seq, tf):
    tm = x_ref.shape[0]

    @pl.when(pl.program_id(0) % tiles_per_seq == 0)
    def _():
        hist[...] = jnp.zeros(hist.shape, F32)

    r_b = DA_WIDTH
    r_c = r_b + SC_WIDTH
    r_d = r_c + S5_WIDTH
    x1 = (x_ref[...] + _dot(oa_ref[...], wo_ref[0:r_b, :])
          + _dot(obd_ref[:, 0:SC_WIDTH], wo_ref[r_b:r_c, :])
          + _dot(oc_ref[...], wo_ref[r_c:r_d, :])
          + _dot(obd_ref[:, SC_WIDTH:SC_WIDTH + M2_WIDTH], wo_ref[r_d:D_MIX, :]))
    ms = jnp.mean(x1 * x1, axis=-1, keepdims=True)
    h2 = (x1 * lax.rsqrt(ms + EPS) * g2_ref[...]).astype(BF16)
    o_ref[...] = x1
    for c in range(D_FF // tf):
        sl = slice(c * tf, (c + 1) * tf)
        gpre = _dot(h2, wg_ref[:, sl])
        cb[0:HIST, :] = hist[:, sl]
        cb[HIST:HIST + tm, :] = gpre
        gc = (cw_ref[2:3, sl] * gpre + cw_ref[1:2, sl] * cb[HIST - 1:HIST - 1 + tm, :]
              + cw_ref[0:1, sl] * cb[HIST - 2:HIST - 2 + tm, :])
        hist[:, sl] = cb[tm:tm + HIST, :]
        act = (_silu(gc) * _dot(h2, wu_ref[:, sl])).astype(BF16)
        o_ref[...] += _dot(act, wd_ref[sl, :])


def _outproj_ffn(x2, oa, obd, oc, wo, g2, wg, wu, cw, wd, layer, seq, tm, tf):
    t = x2.shape[0]
    kern = functools.partial(_ffn_kernel, tiles_per_seq=seq // tm, tf=tf)
    row = lambda i: (i, 0)
    return pl.pallas_call(
        kern,
        grid=(t // tm,),
        in_specs=[
            pl.BlockSpec((tm, D_MODEL), row),
            pl.BlockSpec((tm, DA_WIDTH), row),
            pl.BlockSpec((tm, SC_WIDTH + M2_WIDTH), row),
            pl.BlockSpec((tm, S5_WIDTH), row),
            _layer_spec(wo.shape, layer),
            _const_spec((1, D_MODEL)),
            _layer_spec(wg.shape, layer),
            _layer_spec(wu.shape, layer),
            _const_spec((3, D_FF)),
            _layer_spec(wd.shape, layer),
        ],
        out_specs=pl.BlockSpec((tm, D_MODEL), row),
        out_shape=jax.ShapeDtypeStruct((t, D_MODEL), F32),
        scratch_shapes=[
            pltpu.VMEM((HIST, D_FF), F32),
            pltpu.VMEM((tm + HIST, tf), F32),
        ],
        compiler_params=pltpu.CompilerParams(
            dimension_semantics=("arbitrary",), vmem_limit_bytes=VMEM_LIMIT),
        name="outproj_ffn",
    )(x2, oa, obd, oc, wo, g2, wg, wu, cw, wd)


def _rope_lane_tables(seq):
    inv = 1.0 / (ROPE_THETA ** (jnp.arange(0, ROPE_DIM, 2, dtype=F32) / ROPE_DIM))
    ang = jnp.arange(seq, dtype=F32)[:, None] * inv[None, :]
    cos, sin = jnp.cos(ang), jnp.sin(ang)
    ones = jnp.ones((seq, DA_QK - ROPE_DIM), F32)
    cos64 = jnp.concatenate([cos, cos, ones], axis=1)
    sin64 = jnp.concatenate([-sin, sin, 0.0 * ones], axis=1)
    return jnp.tile(cos64, (1, LANES // DA_QK)), jnp.tile(sin64, (1, LANES // DA_QK))


def _forward(x, p, blk, mix_l, s5_tr):
    bsz, seq, _ = x.shape
    depth = p["w_in"].shape[0]
    t = bsz * seq
    ls = S5_CHUNK
    nc = seq // ls
    assert seq % blk == 0 and seq % mix_l == 0 and nc % s5_tr == 0

    cos_t, sin_t = _rope_lane_tables(seq)
    comp = lax.broadcasted_iota(jnp.int32, (QK_COLS, QK_COLS), 0) // DA_QK
    gm = (comp == comp.T).astype(BF16)
    w_out = p["w_out"].astype(BF16)
    w_gate, w_up, w_down = (p[k].astype(BF16) for k in ("ffn_w_gate", "ffn_w_up", "ffn_w_down"))
    s5_ops = jax.vmap(functools.partial(_s5_operators, ls=ls))(
        p["s5_lam_re"], p["s5_lam_im"], p["s5_log_dt"], p["s5_b_re"], p["s5_b_im"],
        p["s5_c_re"], p["s5_c_im"])
    x2 = x.reshape(t, D_MODEL)
    for l in range(depth):
        lambda_init = 0.8 - 0.6 * math.exp(-0.3 * l)
        qkg = jnp.tile(p["qk_norm_g"][l], (1, LANES // DA_QK))
        pad4 = lambda a: jnp.pad(a, (0, LANES - M2_HEADS))[None, :]
        mix_params = (p["sc_conv_w"][l], p["m2_conv_w"][l], p["m2_conv_b"][l][None, :],
                      pad4(p["m2_dt_bias"][l]), pad4(p["m2_a_log"][l]),
                      jnp.repeat(p["m2_d"][l], M2_HEAD_DIM)[None, :], p["m2_norm_g"][l][None, :])
        qt, kb, vtb, obd, u5 = _inproj(x2, p["ln1_g"][l][None, :], p["w_in"], l, qkg, cos_t, sin_t,
                                       gm, mix_params, bsz, seq, blk, mix_l)
        oa = _attention(qt, kb, vtb, p["da_lambda"][l], p["subln_g"][l][None, :], lambda_init)
        oc = _s5_scan(u5, *s5_ops, l, p["s5_d"][l][None, :], p["s5_w_glu"][l].astype(BF16),
                      p["s5_b_glu"][l][None, :], ls, nc, s5_tr)
        x2 = _outproj_ffn(x2, oa.reshape(t, DA_WIDTH), obd, oc, w_out,
                          p["ln2_g"][l][None, :], w_gate, w_up, p["ffn_conv_w"][l], w_down,
                          l, seq, min(FFN_ROWS, seq), FFN_COLS)
    return x2.reshape(bsz, seq, D_MODEL)


def kernel(x, ln1_g, w_in, qk_norm_g, da_lambda, subln_g, sc_conv_w, s5_lam_re, s5_lam_im, s5_log_dt, s5_b_re, s5_b_im, s5_c_re, s5_c_im, s5_d, s5_w_glu, s5_b_glu, m2_conv_w, m2_conv_b, m2_dt_bias, m2_a_log, m2_d, m2_norm_g, w_out, ln2_g, ffn_w_gate, ffn_w_up, ffn_conv_w, ffn_w_down):
    params = dict(ln1_g=ln1_g, w_in=w_in, qk_norm_g=qk_norm_g, da_lambda=da_lambda, subln_g=subln_g,
                  sc_conv_w=sc_conv_w, s5_lam_re=s5_lam_re, s5_lam_im=s5_lam_im, s5_log_dt=s5_log_dt,
                  s5_b_re=s5_b_re, s5_b_im=s5_b_im, s5_c_re=s5_c_re, s5_c_im=s5_c_im, s5_d=s5_d,
                  s5_w_glu=s5_w_glu, s5_b_glu=s5_b_glu, m2_conv_w=m2_conv_w, m2_conv_b=m2_conv_b,
                  m2_dt_bias=m2_dt_bias, m2_a_log=m2_a_log, m2_d=m2_d, m2_norm_g=m2_norm_g,
                  w_out=w_out, ln2_g=ln2_g, ffn_w_gate=ffn_w_gate, ffn_w_up=ffn_w_up,
                  ffn_conv_w=ffn_conv_w, ffn_w_down=ffn_w_down)
    seq = x.shape[1]
    return _forward(x, params, blk=min(512, seq), mix_l=min(256, seq),
                    s5_tr=min(256, seq // S5_CHUNK))
```

```python
import functools
import math

import jax
import jax.numpy as jnp
from jax import lax
from jax.experimental import pallas as pl
from jax.experimental.pallas import tpu as pltpu

F32 = jnp.float32
BF16 = jnp.bfloat16

D_MODEL = 1024
DA_HEADS = 4
DA_QK = 64
DA_V = 2 * DA_QK
DA_VX = DA_V + 16
DA_WIDTH = DA_HEADS * DA_V
ROPE_DIM = DA_QK // 4
ROPE_THETA = 500000.0
SC_WIDTH = 256
S5_WIDTH = 256
S5_GROUP = 16
S5_GROUPS = S5_WIDTH // S5_GROUP
S5_STATE = 64
M2_HEADS = 4
M2_HEAD_DIM = 64
M2_WIDTH = M2_HEADS * M2_HEAD_DIM
M2_GROUPS = 2
M2_STATE = 128
M2_XBC = M2_WIDTH + 2 * M2_GROUPS * M2_STATE
D_MIX = DA_WIDTH + SC_WIDTH + S5_WIDTH + M2_WIDTH
D_FF = 2816
EPS = 1e-6
LOG2E = 1.4426950408889634

LANES = 128
HIST = 8
VMEM_LIMIT = 56 * 1024 * 1024

W_Q = 0
W_K = W_Q + DA_WIDTH
W_V = W_K + DA_WIDTH
W_SC = W_V + DA_WIDTH
W_S5 = W_SC + 3 * SC_WIDTH
W_Z = W_S5 + S5_WIDTH
W_DT = W_Z + M2_WIDTH + M2_XBC
D_PROJ = W_DT + M2_HEADS
R_SC = 0
R_Z = R_SC + 3 * SC_WIDTH
R_XBC = R_Z + M2_WIDTH
R_DT = R_XBC + M2_XBC
D_REST = R_DT + LANES
QK_COLS = 2 * LANES
ATT_HEADS_PER_STEP = 2
FFN_ROWS = 512
FFN_COLS = 2816

S5_CHUNK = 8
S5_HALF = LANES // S5_GROUP
S5_HSTATE = S5_HALF * S5_STATE


def _dot(a, b):
    return jnp.dot(a, b, preferred_element_type=F32)


def _dot_nt(a, b):
    return lax.dot_general(a, b, (((1,), (1,)), ((), ())), preferred_element_type=F32)


def _dot_tn(a, b):
    return lax.dot_general(a, b, (((0,), (0,)), ((), ())), preferred_element_type=F32)


def _split_bf16(x, parts):
    out = []
    r = x
    for _ in range(parts):
        p = r.astype(BF16)
        out.append(p)
        r = r - p.astype(F32)
    return out


def _sigmoid(x):
    return 0.5 + 0.5 * jnp.tanh(0.5 * x)


def _silu(x):
    return x * _sigmoid(x)


def _const_spec(shape):
    nd = len(shape)
    return pl.BlockSpec(shape, lambda *_: (0,) * nd, pipeline_mode=pl.Buffered(1))


def _layer_spec(shape, layer):
    nd = len(shape)
    return pl.BlockSpec((None,) + tuple(shape[1:]), lambda *_: (layer,) + (0,) * (nd - 1),
                        pipeline_mode=pl.Buffered(1))


def _inproj_kernel(x_ref, g1_ref, wf_ref, qkg_ref, cos_ref, sin_ref, gm_ref,
                   scw_ref, mcw_ref, mcb_ref, dtb_ref, alog_ref, dvec_ref, ng_ref,
                   qt_ref, kb_ref, vt_ref, obd_ref, u5_ref,
                   w_ref, wdt_ref, rest_ref, cbuf, mbuf, hst, ybuf, *, tiles_per_seq, mix_l):
    tm = x_ref.shape[0]

    @pl.when(pl.program_id(0) == 0)
    def _():
        step = 4 * LANES
        for c0 in range(0, W_DT, step):
            w_ref[:, c0:c0 + step] = wf_ref[:, c0:c0 + step].astype(BF16)
        wdt_ref[...] = jnp.zeros(wdt_ref.shape, BF16)
        wdt_ref[:, 0:M2_HEADS] = wf_ref[:, W_DT:D_PROJ].astype(BF16)

    @pl.when(pl.program_id(0) % tiles_per_seq == 0)
    def _():
        _mix_reset(cbuf, mbuf, hst)

    x = x_ref[...]
    ms = jnp.mean(x * x, axis=-1, keepdims=True)
    hn = (x * lax.rsqrt(ms + EPS) * g1_ref[...]).astype(BF16)
    rest_ref[:, R_SC:R_Z] = _dot(hn, w_ref[:, W_SC:W_S5])
    rest_ref[:, R_Z:R_DT] = _dot(hn, w_ref[:, W_Z:W_DT])
    rest_ref[:, R_DT:D_REST] = _dot(hn, wdt_ref[...])
    for r0 in range(0, tm, mix_l):
        _mix_chunk(rest_ref, r0, mix_l, scw_ref, mcw_ref, mcb_ref, dtb_ref, alog_ref, dvec_ref, ng_ref,
                   obd_ref, cbuf, mbuf, hst, ybuf)
    cosv = cos_ref[...]
    sinv = sin_ref[...]
    gm = gm_ref[...]
    lane = lax.broadcasted_iota(jnp.int32, (tm, LANES), 1)
    pair_up = (lane % DA_QK) < (ROPE_DIM // 2)
    heads_per_dot = QK_COLS // DA_V

    def qk_heads(col0, gvec, scale):
        y = _dot(hn, w_ref[:, col0:col0 + QK_COLS])
        hi, lo = _split_bf16(y * y, 2)
        ss = _dot(hi, gm) + _dot(lo, gm)
        yn = y * lax.rsqrt(ss * (1.0 / DA_QK) + EPS)
        out = []
        for j in range(heads_per_dot):
            yh = yn[:, j * DA_V:(j + 1) * DA_V] * gvec
            partner = jnp.where(pair_up,
                                pltpu.roll(yh, LANES - ROPE_DIM // 2, 1),
                                pltpu.roll(yh, ROPE_DIM // 2, 1))
            out.append((yh * cosv + partner * sinv) * scale)
        return out

    for h0 in range(0, DA_HEADS, heads_per_dot):
        qs = qk_heads(W_Q + h0 * DA_V, qkg_ref[0:1, :], DA_QK ** -0.5 * LOG2E)
        ks = qk_heads(W_K + h0 * DA_V, qkg_ref[1:2, :], 1.0)
        v = _dot(hn, w_ref[:, W_V + h0 * DA_V:W_V + h0 * DA_V + QK_COLS])
        for j in range(heads_per_dot):
            qt_ref[h0 + j] = qs[j].T.astype(BF16)
            kb_ref[h0 + j] = ks[j].astype(BF16)
            vt_ref[h0 + j, 0:DA_V, :] = v[:, j * DA_V:(j + 1) * DA_V].T.astype(BF16)
            vt_ref[h0 + j, DA_V:DA_VX, :] = jnp.ones((DA_VX - DA_V, tm), BF16)
    u5 = _dot(hn, w_ref[:, W_S5:W_Z])
    for hh in range(S5_WIDTH // LANES):
        u5_ref[hh] = u5[:, hh * LANES:(hh + 1) * LANES]


def _inproj(x2, g1, w, layer, qkg, cos_t, sin_t, gm, mix_params, bsz, seq, tm, mix_l):
    t = x2.shape[0]
    nt = seq // tm
    row = lambda i: (i, 0)
    tab = lambda i: (i % nt, 0)
    kern = functools.partial(_inproj_kernel, tiles_per_seq=nt, mix_l=mix_l)
    return pl.pallas_call(
        kern,
        grid=(t // tm,),
        in_specs=[
            pl.BlockSpec((tm, D_MODEL), row),
            _const_spec((1, D_MODEL)),
            _layer_spec(w.shape, layer),
            _const_spec((2, LANES)),
            pl.BlockSpec((tm, LANES), tab),
            pl.BlockSpec((tm, LANES), tab),
            _const_spec((QK_COLS, QK_COLS)),
        ] + [_const_spec(a.shape) for a in mix_params],
        out_specs=[
            pl.BlockSpec((None, DA_HEADS, DA_V, tm), lambda i: (i // nt, 0, 0, i % nt)),
            pl.BlockSpec((None, DA_HEADS, None, tm, DA_V), lambda i: (i // nt, 0, i % nt, 0, 0)),
            pl.BlockSpec((None, DA_HEADS, None, DA_VX, tm), lambda i: (i // nt, 0, i % nt, 0, 0)),
            pl.BlockSpec((tm, SC_WIDTH + M2_WIDTH), row),
            pl.BlockSpec((S5_WIDTH // LANES, tm, LANES), lambda i: (0, i, 0)),
        ],
        out_shape=[
            jax.ShapeDtypeStruct((bsz, DA_HEADS, DA_V, seq), BF16),
            jax.ShapeDtypeStruct((bsz, DA_HEADS, nt, tm, DA_V), BF16),
            jax.ShapeDtypeStruct((bsz, DA_HEADS, nt, DA_VX, tm), BF16),
            jax.ShapeDtypeStruct((t, SC_WIDTH + M2_WIDTH), BF16),
            jax.ShapeDtypeStruct((S5_WIDTH // LANES, t, LANES), F32),
        ],
        scratch_shapes=[
            pltpu.VMEM((D_MODEL, W_DT), BF16),
            pltpu.VMEM((D_MODEL, LANES), BF16),
            pltpu.VMEM((tm, D_REST), F32),
            pltpu.VMEM((mix_l + HIST, SC_WIDTH), F32),
            pltpu.VMEM((mix_l + HIST, M2_XBC), F32),
            pltpu.VMEM((M2_GROUPS, M2_STATE, 2 * M2_HEAD_DIM), F32),
            pltpu.VMEM((mix_l, M2_WIDTH), F32),
        ],
        compiler_params=pltpu.CompilerParams(
            dimension_semantics=("arbitrary",), vmem_limit_bytes=VMEM_LIMIT),
        name="inproj_mix",
    )(x2, g1, w, qkg, cos_t, sin_t, gm, *mix_params)


def _attn_kernel(qt_ref, k_ref, vt_ref, lam_ref, sg_ref, o_ref, qz_sc, s_sc, m_sc, acc_sc,
                 *, lambda_init):
    qi = pl.program_id(2)
    heads = qt_ref.shape[0]
    streams = [(hd, c) for hd in range(heads) for c in range(2)]
    for n, (hd, c) in enumerate(streams):
        qt = qt_ref[hd]
        comp = lax.broadcasted_iota(jnp.int32, qt.shape, 0) // DA_QK
        qz_sc[n] = jnp.where(comp == c, qt, jnp.zeros_like(qt))
    m_sc[...] = jnp.full(m_sc.shape, -1e30, F32)
    acc_sc[...] = jnp.zeros(acc_sc.shape, F32)

    def scores(j, buf):
        for n, (hd, c) in enumerate(streams):
            s_sc[buf, n] = _dot(k_ref[hd, j], qz_sc[n])

    def absorb(j, buf, masked):
        for n, (hd, c) in enumerate(streams):
            s = s_sc[buf, n]
            if masked:
                kpos = lax.broadcasted_iota(jnp.int32, s.shape, 0)
                qpos = lax.broadcasted_iota(jnp.int32, s.shape, 1)
                s = jnp.where(kpos <= qpos, s, -jnp.inf)
            m_prev = m_sc[n]
            m_new = jnp.maximum(m_prev, jnp.max(s, axis=0, keepdims=True))
            alpha = jnp.exp2(m_prev - m_new)
            p = jnp.exp2(s - m_new).astype(BF16)
            acc_sc[n] = alpha * acc_sc[n] + _dot(vt_ref[hd, j], p)
            m_sc[n] = m_new

    scores(0, 0)
    pairs = qi // 2

    def body(t, carry):
        j = 2 * t
        scores(j + 1, 1)
        absorb(j, 0, False)
        scores(j + 2, 0)
        absorb(j + 1, 1, False)
        return carry

    lax.fori_loop(0, pairs, body, 0)
    j0 = 2 * pairs

    @pl.when(j0 == qi)
    def _():
        absorb(qi, 0, True)

    @pl.when(j0 != qi)
    def _():
        scores(qi, 1)
        absorb(j0, 0, False)
        absorb(qi, 1, True)

    lp = lam_ref[...]
    lam = (jnp.exp(jnp.sum(lp[0:1] * lp[1:2], axis=1, keepdims=True))
           - jnp.exp(jnp.sum(lp[2:3] * lp[3:4], axis=1, keepdims=True)) + lambda_init)
    for hd in range(heads):
        a0, a1 = acc_sc[2 * hd], acc_sc[2 * hd + 1]
        ot = (a0[0:DA_V] / a0[DA_V:DA_V + 1] - lam * (a1[0:DA_V] / a1[DA_V:DA_V + 1]))
        ms = jnp.mean(ot * ot, axis=0, keepdims=True)
        o = (ot * lax.rsqrt(ms + EPS)).T * (sg_ref[...] * (1.0 - lambda_init))
        o_ref[:, hd * DA_V:(hd + 1) * DA_V] = o.astype(o_ref.dtype)


def _attention(qt, kb, vtb, lam_p, sub_g, lambda_init):
    bsz, _, _, seq = qt.shape
    nb, blk = kb.shape[2], kb.shape[3]
    hps = ATT_HEADS_PER_STEP
    kern = functools.partial(_attn_kernel, lambda_init=lambda_init)
    return pl.pallas_call(
        kern,
        grid=(bsz, DA_HEADS // hps, nb),
        in_specs=[
            pl.BlockSpec((None, hps, DA_V, blk), lambda b, h, i: (b, h, 0, i)),
            pl.BlockSpec((None, hps, nb, blk, DA_V), lambda b, h, i: (b, h, 0, 0, 0)),
            pl.BlockSpec((None, hps, nb, DA_VX, blk), lambda b, h, i: (b, h, 0, 0, 0)),
            pl.BlockSpec((4, DA_QK), lambda b, h, i: (0, 0)),
            pl.BlockSpec((1, DA_V), lambda b, h, i: (0, 0)),
        ],
        out_specs=pl.BlockSpec((None, blk, hps * DA_V), lambda b, h, i: (b, i, h)),
        out_shape=jax.ShapeDtypeStruct((bsz, seq, DA_WIDTH), BF16),
        scratch_shapes=[
            pltpu.VMEM((2 * hps, DA_V, blk), BF16),
            pltpu.VMEM((2, 2 * hps, blk, blk), F32),
            pltpu.VMEM((2 * hps, 1, blk), F32),
            pltpu.VMEM((2 * hps, DA_VX, blk), F32),
        ],
        compiler_params=pltpu.CompilerParams(
            dimension_semantics=("arbitrary", "arbitrary", "arbitrary"),
            vmem_limit_bytes=VMEM_LIMIT),
        name="diff_attn",
    )(qt, kb, vtb, lam_p, sub_g)


def _expand_blockdiag(src_ref, dst_ref, row_group, col_div, col_inner):
    rows, wide = dst_ref.shape[1], dst_ref.shape[2]
    r_s = lax.broadcasted_iota(jnp.int32, (LANES, wide), 0)
    c_s = lax.broadcasted_iota(jnp.int32, (LANES, wide), 1)
    sel = jnp.where(r_s == (c_s // col_div) * col_inner + c_s % col_inner, 1.0, 0.0).astype(BF16)
    step = 256
    r_i = lax.broadcasted_iota(jnp.int32, (step, wide), 0)
    c_i = lax.broadcasted_iota(jnp.int32, (step, wide), 1)
    col_group = (c_i // (col_div // S5_HALF)) % S5_HALF
    for hh in range(dst_ref.shape[0]):
        for r0 in range(0, rows, step):
            same = ((r_i + r0) // row_group) % S5_HALF == col_group
            wide_blk = _dot(src_ref[hh, r0:r0 + step, :], sel)
            dst_ref[hh, r0:r0 + step, :] = jnp.where(same, wide_blk, 0.0).astype(BF16)


def _s5_kernel(u_ref, tc_ref, bc_ref, cc_ref, lam_ref, s5d_ref, wglu_ref, bglu_ref, o_ref,
               toep_ref, bpow_ref, cpow_ref, st, hc, y_ref, *, ls, tiles_per_seq):
    tr = u_ref.shape[1] // ls
    hs = S5_HSTATE

    @pl.when(pl.program_id(0) == 0)
    def _():
        _expand_blockdiag(tc_ref, toep_ref, S5_GROUP, LANES, S5_GROUP)
        _expand_blockdiag(bc_ref, bpow_ref, S5_GROUP, S5_HSTATE, S5_STATE)
        _expand_blockdiag(cc_ref, cpow_ref, S5_STATE, LANES, S5_GROUP)

    @pl.when(pl.program_id(0) % tiles_per_seq == 0)
    def _():
        hc[...] = jnp.zeros(hc.shape, F32)

    def half_input(hh):
        cols = [u_ref[hh, pl.ds(i, tr, stride=ls), :] for i in range(ls)]
        return jnp.concatenate(cols, axis=1).astype(BF16)

    for hh in range(2):
        st[hh] = _dot(half_input(hh), bpow_ref[hh])

    lam = [(lam_ref[hh, 0:1, :], lam_ref[hh, 1:2, :]) for hh in range(2)]

    def body(c, carry):
        new = []
        for hh in range(2):
            hr, hi = carry[2 * hh], carry[2 * hh + 1]
            lr, li = lam[hh]
            s_r = st[hh, pl.ds(c, 1), 0:hs]
            s_i = st[hh, pl.ds(c, 1), hs:2 * hs]
            st[hh, pl.ds(c, 1), 0:hs] = hr
            st[hh, pl.ds(c, 1), hs:2 * hs] = hi
            new.append(lr * hr - li * hi + s_r)
            new.append(lr * hi + li * hr + s_i)
        return tuple(new)

    init = tuple(hc[hh, :, ri * hs:(ri + 1) * hs] for hh in range(2) for ri in range(2))
    fin = lax.fori_loop(0, tr, body, init)
    for hh in range(2):
        for ri in range(2):
            hc[hh, :, ri * hs:(ri + 1) * hs] = fin[2 * hh + ri]

    for hh in range(2):
        y = _dot(half_input(hh), toep_ref[hh]) + _dot(st[hh].astype(BF16), cpow_ref[hh])
        for i in range(ls):
            y_ref[hh, pl.ds(i, tr, stride=ls), :] = y[:, i * LANES:(i + 1) * LANES]

    halves = lambda ref: jnp.concatenate([ref[hh] for hh in range(S5_WIDTH // LANES)], axis=1)
    y5 = halves(y_ref) + s5d_ref[...] * halves(u_ref)
    gl = jax.nn.gelu(y5, approximate=True)
    gate = _dot(gl.astype(BF16), wglu_ref[...]) + bglu_ref[...]
    o_ref[...] = (gl * _sigmoid(gate)).astype(o_ref.dtype)


def _s5_scan(u5, tc, bc, cc, lam_l, layer, s5d, wglu, bglu, ls, nc, tr):
    nh, t, w = u5.shape
    rows = tr * ls
    wide = ls * LANES
    kern = functools.partial(_s5_kernel, ls=ls, tiles_per_seq=nc // tr)
    row = lambda i: (0, i, 0)
    return pl.pallas_call(
        kern,
        grid=(t // rows,),
        in_specs=[
            pl.BlockSpec((nh, rows, w), row),
            _layer_spec(tc.shape, layer),
            _layer_spec(bc.shape, layer),
            _layer_spec(cc.shape, layer),
            _layer_spec(lam_l.shape, layer),
            _const_spec(s5d.shape),
            _const_spec(wglu.shape),
            _const_spec(bglu.shape),
        ],
        out_specs=pl.BlockSpec((rows, S5_WIDTH), lambda i: (i, 0)),
        out_shape=jax.ShapeDtypeStruct((t, S5_WIDTH), BF16),
        scratch_shapes=[
            pltpu.VMEM((nh, wide, wide), BF16),
            pltpu.VMEM((nh, wide, 2 * S5_HSTATE), BF16),
            pltpu.VMEM((nh, 2 * S5_HSTATE, wide), BF16),
            pltpu.VMEM((nh, tr, 2 * S5_HSTATE), F32),
            pltpu.VMEM((nh, 1, 2 * S5_HSTATE), F32),
            pltpu.VMEM((nh, rows, w), F32),
        ],
        compiler_params=pltpu.CompilerParams(
            dimension_semantics=("arbitrary",), vmem_limit_bytes=VMEM_LIMIT),
        name="s5_mixer",
    )(u5, tc, bc, cc, lam_l, s5d, wglu, bglu)


def _s5_operators(lam_re, lam_im, log_dt, b_re, b_im, c_re, c_im, ls):
    hp = lax.Precision.HIGHEST
    dt = jnp.exp(log_dt)[:, None]
    zr, zi = lam_re * dt, lam_im * dt
    er = jnp.exp(zr)
    lbr, lbi = er * jnp.cos(zi), er * jnp.sin(zi)
    den = lam_re * lam_re + lam_im * lam_im
    nr, ni = lbr - 1.0, lbi
    fr = (nr * lam_re + ni * lam_im) / den
    fi = (ni * lam_re - nr * lam_im) / den
    bbr = fr[..., None] * b_re - fi[..., None] * b_im
    bbi = fr[..., None] * b_im + fi[..., None] * b_re
    tau = jnp.arange(ls + 1, dtype=F32)
    pe = jnp.exp(zr[..., None] * tau)
    pr, pi_ = pe * jnp.cos(zi[..., None] * tau), pe * jnp.sin(zi[..., None] * tau)
    pr_t, pi_t = jnp.swapaxes(pr, 1, 2)[:, :ls, None, :], jnp.swapaxes(pi_, 1, 2)[:, :ls, None, :]
    cpr = c_re[:, None] * pr_t - c_im[:, None] * pi_t
    cpi = c_re[:, None] * pi_t + c_im[:, None] * pr_t
    kk = (jnp.einsum('gthp,gpk->gkth', cpr, bbr, precision=hp)
          - jnp.einsum('gthp,gpk->gkth', cpi, bbi, precision=hp))
    nh, gh = S5_GROUPS // S5_HALF, S5_HALF
    w = ls * LANES
    lagw = ls * S5_GROUP
    kf = kk.reshape(S5_GROUPS, S5_GROUP, lagw)
    kt = jnp.stack([jnp.pad(kf[..., :lagw - s * S5_GROUP], ((0, 0), (0, 0), (s * S5_GROUP, 0)))
                    for s in range(ls)], axis=0)
    tc = kt.reshape(ls, nh, gh, S5_GROUP, lagw).transpose(1, 0, 2, 3, 4).reshape(nh, w, lagw)
    tau_rev = (ls - 1) - jnp.arange(ls, dtype=F32)
    pe_rev = jnp.exp(zr[..., None] * tau_rev)
    rev_r, rev_i = pe_rev * jnp.cos(zi[..., None] * tau_rev), pe_rev * jnp.sin(zi[..., None] * tau_rev)
    bre = rev_r[..., None] * bbr[:, :, None, :] - rev_i[..., None] * bbi[:, :, None, :]
    bim = rev_r[..., None] * bbi[:, :, None, :] + rev_i[..., None] * bbr[:, :, None, :]
    bb = jnp.stack([bre, bim], axis=0).reshape(2, nh, gh, S5_STATE, ls, S5_GROUP)
    bc = bb.transpose(1, 4, 2, 5, 0, 3).reshape(nh, w, 2 * S5_STATE)
    qr = c_re[:, :, :, None] * pr[:, None, :, 1:] - c_im[:, :, :, None] * pi_[:, None, :, 1:]
    qi = c_re[:, :, :, None] * pi_[:, None, :, 1:] + c_im[:, :, :, None] * pr[:, None, :, 1:]
    cc = jnp.stack([qr, -qi], axis=0).reshape(2, nh, gh, S5_GROUP, S5_STATE, ls)
    cc = cc.transpose(1, 0, 2, 4, 5, 3).reshape(nh, 2 * S5_HSTATE, ls * S5_GROUP)
    lam_l = jnp.stack([pr[:, :, ls].reshape(nh, S5_HSTATE), pi_[:, :, ls].reshape(nh, S5_HSTATE)], axis=1)
    return tc.astype(BF16), bc.astype(BF16), cc.astype(BF16), lam_l.astype(F32)


def _mix_reset(cbuf, mbuf, hst):
    cbuf[0:HIST, :] = jnp.zeros((HIST, SC_WIDTH), F32)
    mbuf[0:HIST, :] = jnp.zeros((HIST, M2_XBC), F32)
    hst[...] = jnp.zeros(hst.shape, F32)


def _mix_chunk(rest_ref, r0, L, scw_ref, mcw_ref, mcb_ref, dtb_ref, alog_ref, dvec_ref, ng_ref,
               o_ref, cbuf, mbuf, hst, ybuf):
    rows = slice(r0, r0 + L)
    u = rest_ref[rows, R_SC + SC_WIDTH:R_SC + 2 * SC_WIDTH] * rest_ref[rows, R_SC + 2 * SC_WIDTH:R_SC + 3 * SC_WIDTH]
    cbuf[HIST:HIST + L, :] = u
    conv = (scw_ref[2:3, :] * u + scw_ref[1:2, :] * cbuf[HIST - 1:HIST - 1 + L, :]
            + scw_ref[0:1, :] * cbuf[HIST - 2:HIST - 2 + L, :])
    cbuf[0:HIST, :] = cbuf[L:L + HIST, :]
    o_ref[rows, 0:SC_WIDTH] = (rest_ref[rows, R_SC:R_SC + SC_WIDTH] * conv).astype(o_ref.dtype)

    xr = rest_ref[rows, R_XBC:R_XBC + M2_XBC]
    mbuf[HIST:HIST + L, :] = xr
    conv = (mcw_ref[3:4, :] * xr + mcw_ref[2:3, :] * mbuf[HIST - 1:HIST - 1 + L, :]
            + mcw_ref[1:2, :] * mbuf[HIST - 2:HIST - 2 + L, :]
            + mcw_ref[0:1, :] * mbuf[HIST - 3:HIST - 3 + L, :] + mcb_ref[...])
    mbuf[0:HIST, :] = mbuf[L:L + HIST, :]
    xbc = _silu(conv)
    xs = xbc[:, 0:M2_WIDTH]
    dtr = rest_ref[rows, R_DT:R_DT + LANES] + dtb_ref[...]
    dtv = jnp.maximum(dtr, 0.0) + jnp.log(1.0 + jnp.exp(-jnp.abs(dtr)))
    a = -jnp.exp(alog_ref[...]) * dtv
    r_i = lax.broadcasted_iota(jnp.int32, (L, L), 0)
    c_i = lax.broadcasted_iota(jnp.int32, (L, L), 1)
    tril = c_i <= r_i
    tri = jnp.where(tril, 1.0, 0.0).astype(BF16)
    cs = sum(_dot(tri, part) for part in _split_bf16(a, 3))
    cs_t = cs.T
    cs_last = cs[L - 1:L, :]
    ecs = jnp.exp(cs)
    dec = jnp.exp(cs_last - cs)
    ecl = jnp.exp(cs_last)
    lane = lax.broadcasted_iota(jnp.int32, (1, LANES), 1)
    for g in range(M2_GROUPS):
        bg = xbc[:, M2_WIDTH + g * M2_STATE:M2_WIDTH + (g + 1) * M2_STATE].astype(BF16)
        cg = xbc[:, M2_WIDTH + (M2_GROUPS + g) * M2_STATE:M2_WIDTH + (M2_GROUPS + g + 1) * M2_STATE].astype(BF16)
        gram = _dot_nt(cg, bg)
        hs = hst[g]
        yoff = _dot(cg, hs.astype(BF16))
        xdec = []
        for hh in range(2):
            h = 2 * g + hh
            seg = cs[:, h:h + 1] - cs_t[h:h + 1, :]
            dm = jnp.exp(jnp.where(tril, seg, -jnp.inf))
            xdt = xs[:, h * M2_HEAD_DIM:(h + 1) * M2_HEAD_DIM] * dtv[:, h:h + 1]
            yd = _dot((gram * dm).astype(BF16), xdt.astype(BF16))
            yo = yoff[:, hh * M2_HEAD_DIM:(hh + 1) * M2_HEAD_DIM] * ecs[:, h:h + 1]
            ybuf[:, h * M2_HEAD_DIM:(h + 1) * M2_HEAD_DIM] = yd + yo
            xdec.append(xdt * dec[:, h:h + 1])
        upd = _dot_tn(bg, jnp.concatenate(xdec, axis=1).astype(BF16))
        keep = jnp.where(lane < M2_HEAD_DIM, ecl[:, 2 * g:2 * g + 1], ecl[:, 2 * g + 1:2 * g + 2])
        hst[g] = hs * keep + upd
    y = ybuf[...] + dvec_ref[...] * xs
    yg = y * _silu(rest_ref[rows, R_Z:R_Z + M2_WIDTH])
    ms = jnp.mean(yg * yg, axis=-1, keepdims=True)
    o_ref[rows, SC_WIDTH:SC_WIDTH + M2_WIDTH] = (yg * lax.rsqrt(ms + EPS) * ng_ref[...]).astype(o_ref.dtype)


def _ffn_kernel(x_ref, oa_ref, obd_ref, oc_ref, wo_ref, g2_ref, wg_ref, wu_ref, cw_ref, wd_ref,
                o_ref, hist, cb, *, tiles_per_seq, tf):
    tm = x_ref.shape[0]

    @pl.when(pl.program_id(0) % tiles_per_seq == 0)
    def _():
        hist[...] = jnp.zeros(hist.shape, F32)

    r_b = DA_WIDTH
    r_c = r_b + SC_WIDTH
    r_d = r_c + S5_WIDTH
    x1 = (x_ref[...] + _dot(oa_ref[...], wo_ref[0:r_b, :])
          + _dot(obd_ref[:, 0:SC_WIDTH], wo_ref[r_b:r_c, :])
          + _dot(oc_ref[...], wo_ref[r_c:r_d, :])
          + _dot(obd_ref[:, SC_WIDTH:SC_WIDTH + M2_WIDTH], wo_ref[r_d:D_MIX, :]))
    ms = jnp.mean(x1 * x1, axis=-1, keepdims=True)
    h2 = (x1 * lax.rsqrt(ms + EPS) * g2_ref[...]).astype(BF16)
    o_ref[...] = x1
    for c in range(D_FF // tf):
        sl = slice(c * tf, (c + 1) * tf)
        gpre = _dot(h2, wg_ref[:, sl])
        cb[0:HIST, :] = hist[:, sl]
        cb[HIST:HIST + tm, :] = gpre
        gc = (cw_ref[2:3, sl] * gpre + cw_ref[1:2, sl] * cb[HIST - 1:HIST - 1 + tm, :]
              + cw_ref[0:1, sl] * cb[HIST - 2:HIST - 2 + tm, :])
        hist[:, sl] = cb[tm:tm + HIST, :]
        act = (_silu(gc) * _dot(h2, wu_ref[:, sl])).astype(BF16)
        o_ref[...] += _dot(act, wd_ref[sl, :])


def _outproj_ffn(x2, oa, obd, oc, wo, g2, wg, wu, cw, wd, layer, seq, tm, tf):
    t = x2.shape[0]
    kern = functools.partial(_ffn_kernel, tiles_per_seq=seq // tm, tf=tf)
    row = lambda i: (i, 0)
    return pl.pallas_call(
        kern,
        grid=(t // tm,),
        in_specs=[
            pl.BlockSpec((tm, D_MODEL), row),
            pl.BlockSpec((tm, DA_WIDTH), row),
            pl.BlockSpec((tm, SC_WIDTH + M2_WIDTH), row),
            pl.BlockSpec((tm, S5_WIDTH), row),
            _layer_spec(wo.shape, layer),
            _const_spec((1, D_MODEL)),
            _layer_spec(wg.shape, layer),
            _layer_spec(wu.shape, layer),
            _const_spec((3, D_FF)),
            _layer_spec(wd.shape, layer),
        ],
        out_specs=pl.BlockSpec((tm, D_MODEL), row),
        out_shape=jax.ShapeDtypeStruct((t, D_MODEL), F32),
        scratch_shapes=[
            pltpu.VMEM((HIST, D_FF), F32),
            pltpu.VMEM((tm + HIST, tf), F32),
        ],
        compiler_params=pltpu.CompilerParams(
            dimension_semantics=("arbitrary",), vmem_limit_bytes=VMEM_LIMIT),
        name="outproj_ffn",
    )(x2, oa, obd, oc, wo, g2, wg, wu, cw, wd)


def _rope_lane_tables(seq):
    inv = 1.0 / (ROPE_THETA ** (jnp.arange(0, ROPE_DIM, 2, dtype=F32) / ROPE_DIM))
    ang = jnp.arange(seq, dtype=F32)[:, None] * inv[None, :]
    cos, sin = jnp.cos(ang), jnp.sin(ang)
    ones = jnp.ones((seq, DA_QK - ROPE_DIM), F32)
    cos64 = jnp.concatenate([cos, cos, ones], axis=1)
    sin64 = jnp.concatenate([-sin, sin, 0.0 * ones], axis=1)
    return jnp.tile(cos64, (1, LANES // DA_QK)), jnp.tile(sin64, (1, LANES // DA_QK))


def _forward(x, p, blk, mix_l, s5_tr):
    bsz, seq, _ = x.shape
    depth = p["w_in"].shape[0]
    t = bsz * seq
    ls = S5_CHUNK
    nc = seq // ls
    assert seq % blk == 0 and seq % mix_l == 0 and nc % s5_tr == 0

    cos_t, sin_t = _rope_lane_tables(seq)
    comp = lax.broadcasted_iota(jnp.int32, (QK_COLS, QK_COLS), 0) // DA_QK
    gm = (comp == comp.T).astype(BF16)
    w_out = p["w_out"].astype(BF16)
    w_gate, w_up, w_down = (p[k].astype(BF16) for k in ("ffn_w_gate", "ffn_w_up", "ffn_w_down"))
    s5_ops = jax.vmap(functools.partial(_s5_operators, ls=ls))(
        p["s5_lam_re"], p["s5_lam_im"], p["s5_log_dt"], p["s5_b_re"], p["s5_b_im"],
        p["s5_c_re"], p["s5_c_im"])
    x2 = x.reshape(t, D_MODEL)
    for l in range(depth):
        lambda_init = 0.8 - 0.6 * math.exp(-0.3 * l)
        qkg = jnp.tile(p["qk_norm_g"][l], (1, LANES // DA_QK))
        pad4 = lambda a: jnp.pad(a, (0, LANES - M2_HEADS))[None, :]
        mix_params = (p["sc_conv_w"][l], p["m2_conv_w"][l], p["m2_conv_b"][l][None, :],
                      pad4(p["m2_dt_bias"][l]), pad4(p["m2_a_log"][l]),
                      jnp.repeat(p["m2_d"][l], M2_HEAD_DIM)[None, :], p["m2_norm_g"][l][None, :])
        qt, kb, vtb, obd, u5 = _inproj(x2, p["ln1_g"][l][None, :], p["w_in"], l, qkg, cos_t, sin_t,
                                       gm, mix_params, bsz, seq, blk, mix_l)
        oa = _attention(qt, kb, vtb, p["da_lambda"][l], p["subln_g"][l][None, :], lambda_init)
        oc = _s5_scan(u5, *s5_ops, l, p["s5_d"][l][None, :], p["s5_w_glu"][l].astype(BF16),
                      p["s5_b_glu"][l][None, :], ls, nc, s5_tr)
        x2 = _outproj_ffn(x2, oa.reshape(t, DA_WIDTH), obd, oc, w_out,
                          p["ln2_g"][l][None, :], w_gate, w_up, p["ffn_conv_w"][l], w_down,
                          l, seq, min(FFN_ROWS, seq), FFN_COLS)
    return x2.reshape(bsz, seq, D_MODEL)


def kernel(x, ln1_g, w_in, qk_norm_g, da_lambda, subln_g, sc_conv_w, s5_lam_re, s5_lam_im, s5_log_dt, s5_b_re, s5_b_im, s5_c_re, s5_c_im, s5_d, s5_w_glu, s5_b_glu, m2_conv_w, m2_conv_b, m2_dt_bias, m2_a_log, m2_d, m2_norm_g, w_out, ln2_g, ffn_w_gate, ffn_w_up, ffn_conv_w, ffn_w_down):
    params = dict(ln1_g=ln1_g, w_in=w_in, qk_norm_g=qk_norm_g, da_lambda=da_lambda, subln_g=subln_g,
                  sc_conv_w=sc_conv_w, s5_lam_re=s5_lam_re, s5_lam_im=s5_lam_im, s5_log_dt=s5_log_dt,
                  s5_b_re=s5_b_re, s5_b_im=s5_b_im, s5_c_re=s5_c_re, s5_c_im=s5_c_im, s5_d=s5_d,
                  s5_w_glu=s5_w_glu, s5_b_glu=s5_b_glu, m2_conv_w=m2_conv_w, m2_conv_b=m2_conv_b,
                  m2_dt_bias=m2_dt_bias, m2_a_log=m2_a_log, m2_d=m2_d, m2_norm_g=m2_norm_g,
                  w_out=w_out, ln2_g=ln2_g, ffn_w_gate=ffn_w_gate, ffn_w_up=ffn_w_up,
                  ffn_conv_w=ffn_conv_w, ffn_w_down=ffn_w_down)
    seq = x.shape[1]
    return _forward(x, params, blk=min(512, seq), mix_l=min(256, seq),
                    s5_tr=min(256, seq // S5_CHUNK))
```

```python
import functools
import math

import jax
import jax.numpy as jnp
from jax import lax
from jax.experimental import pallas as pl
from jax.experimental.pallas import tpu as pltpu

F32 = jnp.float32
BF16 = jnp.bfloat16

D_MODEL = 1024
DA_HEADS = 4
DA_QK = 64
DA_V = 2 * DA_QK
DA_VX = DA_V + 16
DA_WIDTH = DA_HEADS * DA_V
ROPE_DIM = DA_QK // 4
ROPE_THETA = 500000.0
SC_WIDTH = 256
S5_WIDTH = 256
S5_GROUP = 16
S5_GROUPS = S5_WIDTH // S5_GROUP
S5_STATE = 64
M2_HEADS = 4
M2_HEAD_DIM = 64
M2_WIDTH = M2_HEADS * M2_HEAD_DIM
M2_GROUPS = 2
M2_STATE = 128
M2_XBC = M2_WIDTH + 2 * M2_GROUPS * M2_STATE
D_MIX = DA_WIDTH + SC_WIDTH + S5_WIDTH + M2_WIDTH
D_FF = 2816
EPS = 1e-6
LOG2E = 1.4426950408889634

LANES = 128
HIST = 8
VMEM_LIMIT = 56 * 1024 * 1024

W_Q = 0
W_K = W_Q + DA_WIDTH
W_V = W_K + DA_WIDTH
W_SC = W_V + DA_WIDTH
W_S5 = W_SC + 3 * SC_WIDTH
W_Z = W_S5 + S5_WIDTH
W_DT = W_Z + M2_WIDTH + M2_XBC
D_PROJ = W_DT + M2_HEADS
R_SC = 0
R_Z = R_SC + 3 * SC_WIDTH
R_XBC = R_Z + M2_WIDTH
R_DT = R_XBC + M2_XBC
D_REST = R_DT + LANES
QK_COLS = 2 * LANES
ATT_HEADS_PER_STEP = 2
FFN_ROWS = 512
FFN_COLS = 2816

S5_CHUNK = 8
S5_HALF = LANES // S5_GROUP
S5_HSTATE = S5_HALF * S5_STATE


def _dot(a, b):
    return jnp.dot(a, b, preferred_element_type=F32)


def _dot_nt(a, b):
    return lax.dot_general(a, b, (((1,), (1,)), ((), ())), preferred_element_type=F32)


def _dot_tn(a, b):
    return lax.dot_general(a, b, (((0,), (0,)), ((), ())), preferred_element_type=F32)


def _split_bf16(x, parts):
    out = []
    r = x
    for _ in range(parts):
        p = r.astype(BF16)
        out.append(p)
        r = r - p.astype(F32)
    return out


def _sigmoid(x):
    return 0.5 + 0.5 * jnp.tanh(0.5 * x)


def _silu(x):
    return x * _sigmoid(x)


def _const_spec(shape):
    nd = len(shape)
    return pl.BlockSpec(shape, lambda *_: (0,) * nd, pipeline_mode=pl.Buffered(1))


def _layer_spec(shape, layer):
    nd = len(shape)
    return pl.BlockSpec((None,) + tuple(shape[1:]), lambda *_: (layer,) + (0,) * (nd - 1),
                        pipeline_mode=pl.Buffered(1))


def _inproj_kernel(x_ref, g1_ref, wf_ref, qkg_ref, cos_ref, sin_ref, gm_ref,
                   scw_ref, mcw_ref, mcb_ref, dtb_ref, alog_ref, dvec_ref, ng_ref,
                   qt_ref, kb_ref, vt_ref, obd_ref, u5_ref,
                   w_ref, wdt_ref, rest_ref, cbuf, mbuf, hst, ybuf, *, tiles_per_seq, mix_l):
    tm = x_ref.shape[0]

    @pl.when(pl.program_id(0) == 0)
    def _():
        step = 4 * LANES
        for c0 in range(0, W_DT, step):
            w_ref[:, c0:c0 + step] = wf_ref[:, c0:c0 + step].astype(BF16)
        wdt_ref[...] = jnp.zeros(wdt_ref.shape, BF16)
        wdt_ref[:, 0:M2_HEADS] = wf_ref[:, W_DT:D_PROJ].astype(BF16)

    @pl.when(pl.program_id(0) % tiles_per_seq == 0)
    def _():
        _mix_reset(cbuf, mbuf, hst)

    x = x_ref[...]
    ms = jnp.mean(x * x, axis=-1, keepdims=True)
    hn = (x * lax.rsqrt(ms + EPS) * g1_ref[...]).astype(BF16)
    rest_ref[:, R_SC:R_Z] = _dot(hn, w_ref[:, W_SC:W_S5])
    rest_ref[:, R_Z:R_DT] = _dot(hn, w_ref[:, W_Z:W_DT])
    rest_ref[:, R_DT:D_REST] = _dot(hn, wdt_ref[...])
    for r0 in range(0, tm, mix_l):
        _mix_chunk(rest_ref, r0, mix_l, scw_ref, mcw_ref, mcb_ref, dtb_ref, alog_ref, dvec_ref, ng_ref,
                   obd_ref, cbuf, mbuf, hst, ybuf)
    cosv = cos_ref[...]
    sinv = sin_ref[...]
    gm = gm_ref[...]
    lane = lax.broadcasted_iota(jnp.int32, (tm, LANES), 1)
    pair_up = (lane % DA_QK) < (ROPE_DIM // 2)
    heads_per_dot = QK_COLS // DA_V

    def qk_heads(col0, gvec, scale):
        y = _dot(hn, w_ref[:, col0:col0 + QK_COLS])
        hi, lo = _split_bf16(y * y, 2)
        ss = _dot(hi, gm) + _dot(lo, gm)
        yn = y * lax.rsqrt(ss * (1.0 / DA_QK) + EPS)
        out = []
        for j in range(heads_per_dot):
            yh = yn[:, j * DA_V:(j + 1) * DA_V] * gvec
            partner = jnp.where(pair_up,
                                pltpu.roll(yh, LANES - ROPE_DIM // 2, 1),
                                pltpu.roll(yh, ROPE_DIM // 2, 1))
            out.append((yh * cosv + partner * sinv) * scale)
        return out

    for h0 in range(0, DA_HEADS, heads_per_dot):
        qs = qk_heads(W_Q + h0 * DA_V, qkg_ref[0:1, :], DA_QK ** -0.5 * LOG2E)
        ks = qk_heads(W_K + h0 * DA_V, qkg_ref[1:2, :], 1.0)
        v = _dot(hn, w_ref[:, W_V + h0 * DA_V:W_V + h0 * DA_V + QK_COLS])
        for j in range(heads_per_dot):
            qt_ref[h0 + j] = qs[j].T.astype(BF16)
            kb_ref[h0 + j] = ks[j].astype(BF16)
            vt_ref[h0 + j, 0:DA_V, :] = v[:, j * DA_V:(j + 1) * DA_V].T.astype(BF16)
            vt_ref[h0 + j, DA_V:DA_VX, :] = jnp.ones((DA_VX - DA_V, tm), BF16)
    u5 = _dot(hn, w_ref[:, W_S5:W_Z])
    for hh in range(S5_WIDTH // LANES):
        u5_ref[hh] = u5[:, hh * LANES:(hh + 1) * LANES]


def _inproj(x2, g1, w, layer, qkg, cos_t, sin_t, gm, mix_params, bsz, seq, tm, mix_l):
    t = x2.shape[0]
    nt = seq // tm
    row = lambda i: (i, 0)
    tab = lambda i: (i % nt, 0)
    kern = functools.partial(_inproj_kernel, tiles_per_seq=nt, mix_l=mix_l)
    return pl.pallas_call(
        kern,
        grid=(t // tm,),
        in_specs=[
            pl.BlockSpec((tm, D_MODEL), row),
            _const_spec((1, D_MODEL)),
            _layer_spec(w.shape, layer),
            _const_spec((2, LANES)),
            pl.BlockSpec((tm, LANES), tab),
            pl.BlockSpec((tm, LANES), tab),
            _const_spec((QK_COLS, QK_COLS)),
        ] + [_const_spec(a.shape) for a in mix_params],
        out_specs=[
            pl.BlockSpec((None, DA_HEADS, DA_V, tm), lambda i: (i // nt, 0, 0, i % nt)),
            pl.BlockSpec((None, DA_HEADS, None, tm, DA_V), lambda i: (i // nt, 0, i % nt, 0, 0)),
            pl.BlockSpec((None, DA_HEADS, None, DA_VX, tm), lambda i: (i // nt, 0, i % nt, 0, 0)),
            pl.BlockSpec((tm, SC_WIDTH + M2_WIDTH), row),
            pl.BlockSpec((S5_WIDTH // LANES, tm, LANES), lambda i: (0, i, 0)),
        ],
        out_shape=[
            jax.ShapeDtypeStruct((bsz, DA_HEADS, DA_V, seq), BF16),
            jax.ShapeDtypeStruct((bsz, DA_HEADS, nt, tm, DA_V), BF16),
            jax.ShapeDtypeStruct((bsz, DA_HEADS, nt, DA_VX, tm), BF16),
            jax.ShapeDtypeStruct((t, SC_WIDTH + M2_WIDTH), BF16),
            jax.ShapeDtypeStruct((S5_WIDTH // LANES, t, LANES), F32),
        ],
        scratch_shapes=[
            pltpu.VMEM((D_MODEL, W_DT), BF16),
            pltpu.VMEM((D_MODEL, LANES), BF16),
            pltpu.VMEM((tm, D_REST), F32),
            pltpu.VMEM((mix_l + HIST, SC_WIDTH), F32),
            pltpu.VMEM((mix_l + HIST, M2_XBC), F32),
            pltpu.VMEM((M2_GROUPS, M2_STATE, 2 * M2_HEAD_DIM), F32),
            pltpu.VMEM((mix_l, M2_WIDTH), F32),
        ],
        compiler_params=pltpu.CompilerParams(
            dimension_semantics=("arbitrary",), vmem_limit_bytes=VMEM_LIMIT),
        name="inproj_mix",
    )(x2, g1, w, qkg, cos_t, sin_t, gm, *mix_params)


def _attn_kernel(qt_ref, k_ref, vt_ref, lam_ref, sg_ref, o_ref, qz_sc, s_sc, m_sc, acc_sc,
                 *, lambda_init):
    qi = pl.program_id(2)
    heads = qt_ref.shape[0]
    streams = [(hd, c) for hd in range(heads) for c in range(2)]
    for n, (hd, c) in enumerate(streams):
        qt = qt_ref[hd]
        comp = lax.broadcasted_iota(jnp.int32, qt.shape, 0) // DA_QK
        qz_sc[n] = jnp.where(comp == c, qt, jnp.zeros_like(qt))
    m_sc[...] = jnp.full(m_sc.shape, -1e30, F32)
    acc_sc[...] = jnp.zeros(acc_sc.shape, F32)

    def scores(j, buf):
        for n, (hd, c) in enumerate(streams):
            s_sc[buf, n] = _dot(k_ref[hd, j], qz_sc[n])

    def absorb(j, buf, masked):
        for n, (hd, c) in enumerate(streams):
            s = s_sc[buf, n]
            if masked:
                kpos = lax.broadcasted_iota(jnp.int32, s.shape, 0)
                qpos = lax.broadcasted_iota(jnp.int32, s.shape, 1)
                s = jnp.where(kpos <= qpos, s, -jnp.inf)
            m_prev = m_sc[n]
            m_new = jnp.maximum(m_prev, jnp.max(s, axis=0, keepdims=True))
            alpha = jnp.exp2(m_prev - m_new)
            p = jnp.exp2(s - m_new).astype(BF16)
            acc_sc[n] = alpha * acc_sc[n] + _dot(vt_ref[hd, j], p)
            m_sc[n] = m_new

    scores(0, 0)
    pairs = qi // 2

    def body(t, carry):
        j = 2 * t
        scores(j + 1, 1)
        absorb(j, 0, False)
        scores(j + 2, 0)
        absorb(j + 1, 1, False)
        return carry

    lax.fori_loop(0, pairs, body, 0)
    j0 = 2 * pairs

    @pl.when(j0 == qi)
    def _():
        absorb(qi, 0, True)

    @pl.when(j0 != qi)
    def _():
        scores(qi, 1)
        absorb(j0, 0, False)
        absorb(qi, 1, True)

    lp = lam_ref[...]
    lam = (jnp.exp(jnp.sum(lp[0:1] * lp[1:2], axis=1, keepdims=True))
           - jnp.exp(jnp.sum(lp[2:3] * lp[3:4], axis=1, keepdims=True)) + lambda_init)
    for hd in range(heads):
        a0, a1 = acc_sc[2 * hd], acc_sc[2 * hd + 1]
        ot = (a0[0:DA_V] / a0[DA_V:DA_V + 1] - lam * (a1[0:DA_V] / a1[DA_V:DA_V + 1]))
        ms = jnp.mean(ot * ot, axis=0, keepdims=True)
        o = (ot * lax.rsqrt(ms + EPS)).T * (sg_ref[...] * (1.0 - lambda_init))
        o_ref[:, hd * DA_V:(hd + 1) * DA_V] = o.astype(o_ref.dtype)


def _attention(qt, kb, vtb, lam_p, sub_g, lambda_init):
    bsz, _, _, seq = qt.shape
    nb, blk = kb.shape[2], kb.shape[3]
    hps = ATT_HEADS_PER_STEP
    kern = functools.partial(_attn_kernel, lambda_init=lambda_init)
    return pl.pallas_call(
        kern,
        grid=(bsz, DA_HEADS // hps, nb),
        in_specs=[
            pl.BlockSpec((None, hps, DA_V, blk), lambda b, h, i: (b, h, 0, i)),
            pl.BlockSpec((None, hps, nb, blk, DA_V), lambda b, h, i: (b, h, 0, 0, 0)),
            pl.BlockSpec((None, hps, nb, DA_VX, blk), lambda b, h, i: (b, h, 0, 0, 0)),
            pl.BlockSpec((4, DA_QK), lambda b, h, i: (0, 0)),
            pl.BlockSpec((1, DA_V), lambda b, h, i: (0, 0)),
        ],
        out_specs=pl.BlockSpec((None, blk, hps * DA_V), lambda b, h, i: (b, i, h)),
        out_shape=jax.ShapeDtypeStruct((bsz, seq, DA_WIDTH), BF16),
        scratch_shapes=[
            pltpu.VMEM((2 * hps, DA_V, blk), BF16),
            pltpu.VMEM((2, 2 * hps, blk, blk), F32),
            pltpu.VMEM((2 * hps, 1, blk), F32),
            pltpu.VMEM((2 * hps, DA_VX, blk), F32),
        ],
        compiler_params=pltpu.CompilerParams(
            dimension_semantics=("arbitrary", "arbitrary", "arbitrary"),
            vmem_limit_bytes=VMEM_LIMIT),
        name="diff_attn",
    )(qt, kb, vtb, lam_p, sub_g)


def _expand_blockdiag(src_ref, dst_ref, row_group, col_div, col_inner):
    rows, wide = dst_ref.shape[1], dst_ref.shape[2]
    r_s = lax.broadcasted_iota(jnp.int32, (LANES, wide), 0)
    c_s = lax.broadcasted_iota(jnp.int32, (LANES, wide), 1)
    sel = jnp.where(r_s == (c_s // col_div) * col_inner + c_s % col_inner, 1.0, 0.0).astype(BF16)
    step = 256
    r_i = lax.broadcasted_iota(jnp.int32, (step, wide), 0)
    c_i = lax.broadcasted_iota(jnp.int32, (step, wide), 1)
    col_group = (c_i // (col_div // S5_HALF)) % S5_HALF
    for hh in range(dst_ref.shape[0]):
        for r0 in range(0, rows, step):
            same = ((r_i + r0) // row_group) % S5_HALF == col_group
            wide_blk = _dot(src_ref[hh, r0:r0 + step, :], sel)
            dst_ref[hh, r0:r0 + step, :] = jnp.where(same, wide_blk, 0.0).astype(BF16)


def _s5_kernel(u_ref, tc_ref, bc_ref, cc_ref, lam_ref, s5d_ref, wglu_ref, bglu_ref, o_ref,
               toep_ref, bpow_ref, cpow_ref, st, hc, y_ref, *, ls, tiles_per_seq):
    tr = u_ref.shape[1] // ls
    hs = S5_HSTATE

    @pl.when(pl.program_id(0) == 0)
    def _():
        _expand_blockdiag(tc_ref, toep_ref, S5_GROUP, LANES, S5_GROUP)
        _expand_blockdiag(bc_ref, bpow_ref, S5_GROUP, S5_HSTATE, S5_STATE)
        _expand_blockdiag(cc_ref, cpow_ref, S5_STATE, LANES, S5_GROUP)

    @pl.when(pl.program_id(0) % tiles_per_seq == 0)
    def _():
        hc[...] = jnp.zeros(hc.shape, F32)

    def half_input(hh):
        cols = [u_ref[hh, pl.ds(i, tr, stride=ls), :] for i in range(ls)]
        return jnp.concatenate(cols, axis=1).astype(BF16)

    for hh in range(2):
        st[hh] = _dot(half_input(hh), bpow_ref[hh])

    lam = [(lam_ref[hh, 0:1, :], lam_ref[hh, 1:2, :]) for hh in range(2)]

    def body(c, carry):
        new = []
        for hh in range(2):
            hr, hi = carry[2 * hh], carry[2 * hh + 1]
            lr, li = lam[hh]
            s_r = st[hh, pl.ds(c, 1), 0:hs]
            s_i = st[hh, pl.ds(c, 1), hs:2 * hs]
            st[hh, pl.ds(c, 1), 0:hs] = hr
            st[hh, pl.ds(c, 1), hs:2 * hs] = hi
            new.append(lr * hr - li * hi + s_r)
            new.append(lr * hi + li * hr + s_i)
        return tuple(new)

    init = tuple(hc[hh, :, ri * hs:(ri + 1) * hs] for hh in range(2) for ri in range(2))
    fin = init
    for c in range(tr):
        fin = body(c, fin)
    for hh in range(2):
        for ri in range(2):
            hc[hh, :, ri * hs:(ri + 1) * hs] = fin[2 * hh + ri]

    for hh in range(2):
        y = _dot(half_input(hh), toep_ref[hh]) + _dot(st[hh].astype(BF16), cpow_ref[hh])
        for i in range(ls):
            y_ref[hh, pl.ds(i, tr, stride=ls), :] = y[:, i * LANES:(i + 1) * LANES]

    halves = lambda ref: jnp.concatenate([ref[hh] for hh in range(S5_WIDTH // LANES)], axis=1)
    y5 = halves(y_ref) + s5d_ref[...] * halves(u_ref)
    gl = jax.nn.gelu(y5, approximate=True)
    gate = _dot(gl.astype(BF16), wglu_ref[...]) + bglu_ref[...]
    o_ref[...] = (gl * _sigmoid(gate)).astype(o_ref.dtype)


def _s5_scan(u5, tc, bc, cc, lam_l, layer, s5d, wglu, bglu, ls, nc, tr):
    nh, t, w = u5.shape
    rows = tr * ls
    wide = ls * LANES
    kern = functools.partial(_s5_kernel, ls=ls, tiles_per_seq=nc // tr)
    row = lambda i: (0, i, 0)
    return pl.pallas_call(
        kern,
        grid=(t // rows,),
        in_specs=[
            pl.BlockSpec((nh, rows, w), row),
            _layer_spec(tc.shape, layer),
            _layer_spec(bc.shape, layer),
            _layer_spec(cc.shape, layer),
            _layer_spec(lam_l.shape, layer),
            _const_spec(s5d.shape),
            _const_spec(wglu.shape),
            _const_spec(bglu.shape),
        ],
        out_specs=pl.BlockSpec((rows, S5_WIDTH), lambda i: (i, 0)),
        out_shape=jax.ShapeDtypeStruct((t, S5_WIDTH), BF16),
        scratch_shapes=[
            pltpu.VMEM((nh, wide, wide), BF16),
            pltpu.VMEM((nh, wide, 2 * S5_HSTATE), BF16),
            pltpu.VMEM((nh, 2 * S5_HSTATE, wide), BF16),
            pltpu.VMEM((nh, tr, 2 * S5_HSTATE), F32),
            pltpu.VMEM((nh, 1, 2 * S5_HSTATE), F32),
            pltpu.VMEM((nh, rows, w), F32),
        ],
        compiler_params=pltpu.CompilerParams(
            dimension_semantics=("arbitrary",), vmem_limit_bytes=VMEM_LIMIT),
        name="s5_mixer",
    )(u5, tc, bc, cc, lam_l, s5d, wglu, bglu)


def _s5_operators(lam_re, lam_im, log_dt, b_re, b_im, c_re, c_im, ls):
    hp = lax.Precision.HIGHEST
    dt = jnp.exp(log_dt)[:, None]
    zr, zi = lam_re * dt, lam_im * dt
    er = jnp.exp(zr)
    lbr, lbi = er * jnp.cos(zi), er * jnp.sin(zi)
    den = lam_re * lam_re + lam_im * lam_im
    nr, ni = lbr - 1.0, lbi
    fr = (nr * lam_re + ni * lam_im) / den
    fi = (ni * lam_re - nr * lam_im) / den
    bbr = fr[..., None] * b_re - fi[..., None] * b_im
    bbi = fr[..., None] * b_im + fi[..., None] * b_re
    tau = jnp.arange(ls + 1, dtype=F32)
    pe = jnp.exp(zr[..., None] * tau)
    pr, pi_ = pe * jnp.cos(zi[..., None] * tau), pe * jnp.sin(zi[..., None] * tau)
    nh, gh = S5_GROUPS // S5_HALF, S5_HALF
    w = ls * LANES
    lagw = ls * S5_GROUP
    c_rt, c_it = jnp.swapaxes(c_re, 1, 2)[:, :, None, :], jnp.swapaxes(c_im, 1, 2)[:, :, None, :]
    cpr = (c_rt * pr[:, :, :ls, None] - c_it * pi_[:, :, :ls, None]).reshape(S5_GROUPS, S5_STATE, lagw)
    cpi = (c_rt * pi_[:, :, :ls, None] + c_it * pr[:, :, :ls, None]).reshape(S5_GROUPS, S5_STATE, lagw)
    kf = (jnp.matmul(jnp.swapaxes(bbr, 1, 2), cpr, precision=hp)
          - jnp.matmul(jnp.swapaxes(bbi, 1, 2), cpi, precision=hp))
    kt = jnp.stack([jnp.pad(kf[..., :lagw - s * S5_GROUP], ((0, 0), (0, 0), (s * S5_GROUP, 0)))
                    for s in range(ls)], axis=0)
    tc = kt.reshape(ls, nh, gh, S5_GROUP, lagw).transpose(1, 0, 2, 3, 4).reshape(nh, w, lagw)
    tau_rev = (ls - 1) - jnp.arange(ls, dtype=F32)
    pe_rev = jnp.exp(zr[..., None] * tau_rev)
    rev_r, rev_i = pe_rev * jnp.cos(zi[..., None] * tau_rev), pe_rev * jnp.sin(zi[..., None] * tau_rev)
    bre = rev_r[..., None] * bbr[:, :, None, :] - rev_i[..., None] * bbi[:, :, None, :]
    bim = rev_r[..., None] * bbi[:, :, None, :] + rev_i[..., None] * bbr[:, :, None, :]
    bb = jnp.stack([bre, bim], axis=0).reshape(2, nh, gh, S5_STATE, ls, S5_GROUP)
    bc = bb.transpose(1, 4, 2, 5, 0, 3).reshape(nh, w, 2 * S5_STATE)
    qr = c_re[:, :, :, None] * pr[:, None, :, 1:] - c_im[:, :, :, None] * pi_[:, None, :, 1:]
    qi = c_re[:, :, :, None] * pi_[:, None, :, 1:] + c_im[:, :, :, None] * pr[:, None, :, 1:]
    cc = jnp.stack([qr, -qi], axis=0).reshape(2, nh, gh, S5_GROUP, S5_STATE, ls)
    cc = cc.transpose(1, 0, 2, 4, 5, 3).reshape(nh, 2 * S5_HSTATE, ls * S5_GROUP)
    lam_l = jnp.stack([pr[:, :, ls].reshape(nh, S5_HSTATE), pi_[:, :, ls].reshape(nh, S5_HSTATE)], axis=1)
    return tc.astype(BF16), bc.astype(BF16), cc.astype(BF16), lam_l.astype(F32)


def _mix_reset(cbuf, mbuf, hst):
    cbuf[0:HIST, :] = jnp.zeros((HIST, SC_WIDTH), F32)
    mbuf[0:HIST, :] = jnp.zeros((HIST, M2_XBC), F32)
    hst[...] = jnp.zeros(hst.shape, F32)


def _mix_chunk(rest_ref, r0, L, scw_ref, mcw_ref, mcb_ref, dtb_ref, alog_ref, dvec_ref, ng_ref,
               o_ref, cbuf, mbuf, hst, ybuf):
    rows = slice(r0, r0 + L)
    u = rest_ref[rows, R_SC + SC_WIDTH:R_SC + 2 * SC_WIDTH] * rest_ref[rows, R_SC + 2 * SC_WIDTH:R_SC + 3 * SC_WIDTH]
    cbuf[HIST:HIST + L, :] = u
    conv = (scw_ref[2:3, :] * u + scw_ref[1:2, :] * cbuf[HIST - 1:HIST - 1 + L, :]
            + scw_ref[0:1, :] * cbuf[HIST - 2:HIST - 2 + L, :])
    cbuf[0:HIST, :] = cbuf[L:L + HIST, :]
    o_ref[rows, 0:SC_WIDTH] = (rest_ref[rows, R_SC:R_SC + SC_WIDTH] * conv).astype(o_ref.dtype)

    xr = rest_ref[rows, R_XBC:R_XBC + M2_XBC]
    mbuf[HIST:HIST + L, :] = xr
    conv = (mcw_ref[3:4, :] * xr + mcw_ref[2:3, :] * mbuf[HIST - 1:HIST - 1 + L, :]
            + mcw_ref[1:2, :] * mbuf[HIST - 2:HIST - 2 + L, :]
            + mcw_ref[0:1, :] * mbuf[HIST - 3:HIST - 3 + L, :] + mcb_ref[...])
    mbuf[0:HIST, :] = mbuf[L:L + HIST, :]
    xbc = _silu(conv)
    xs = xbc[:, 0:M2_WIDTH]
    dtr = rest_ref[rows, R_DT:R_DT + LANES] + dtb_ref[...]
    dtv = jnp.maximum(dtr, 0.0) + jnp.log(1.0 + jnp.exp(-jnp.abs(dtr)))
    a = -jnp.exp(alog_ref[...]) * dtv
    r_i = lax.broadcasted_iota(jnp.int32, (L, L), 0)
    c_i = lax.broadcasted_iota(jnp.int32, (L, L), 1)
    tril = c_i <= r_i
    tri = jnp.where(tril, 1.0, 0.0).astype(BF16)
    cs = sum(_dot(tri, part) for part in _split_bf16(a, 3))
    cs_t = cs.T
    cs_last = cs[L - 1:L, :]
    ecs = jnp.exp(cs)
    dec = jnp.exp(cs_last - cs)
    ecl = jnp.exp(cs_last)
    lane = lax.broadcasted_iota(jnp.int32, (1, LANES), 1)
    for g in range(M2_GROUPS):
        bg = xbc[:, M2_WIDTH + g * M2_STATE:M2_WIDTH + (g + 1) * M2_STATE].astype(BF16)
        cg = xbc[:, M2_WIDTH + (M2_GROUPS + g) * M2_STATE:M2_WIDTH + (M2_GROUPS + g + 1) * M2_STATE].astype(BF16)
        gram = _dot_nt(cg, bg)
        hs = hst[g]
        yoff = _dot(cg, hs.astype(BF16))
        xdec = []
        for hh in range(2):
            h = 2 * g + hh
            seg = cs[:, h:h + 1] - cs_t[h:h + 1, :]
            dm = jnp.exp(jnp.where(tril, seg, -jnp.inf))
            xdt = xs[:, h * M2_HEAD_DIM:(h + 1) * M2_HEAD_DIM] * dtv[:, h:h + 1]
            yd = _dot((gram * dm).astype(BF16), xdt.astype(BF16))
            yo = yoff[:, hh * M2_HEAD_DIM:(hh + 1) * M2_HEAD_DIM] * ecs[:, h:h + 1]
            ybuf[:, h * M2_HEAD_DIM:(h + 1) * M2_HEAD_DIM] = yd + yo
            xdec.append(xdt * dec[:, h:h + 1])
        upd = _dot_tn(bg, jnp.concatenate(xdec, axis=1).astype(BF16))
        keep = jnp.where(lane < M2_HEAD_DIM, ecl[:, 2 * g:2 * g + 1], ecl[:, 2 * g + 1:2 * g + 2])
        hst[g] = hs * keep + upd
    y = ybuf[...] + dvec_ref[...] * xs
    yg = y * _silu(rest_ref[rows, R_Z:R_Z + M2_WIDTH])
    ms = jnp.mean(yg * yg, axis=-1, keepdims=True)
    o_ref[rows, SC_WIDTH:SC_WIDTH + M2_WIDTH] = (yg * lax.rsqrt(ms + EPS) * ng_ref[...]).astype(o_ref.dtype)


def _ffn_kernel(x_ref, oa_ref, obd_ref, oc_ref, wo_ref, g2_ref, wg_ref, wu_ref, cw_ref, wd_ref,
                o_ref, hist, cb, *, tiles_per_seq, tf):
    tm = x_ref.shape[0]

    @pl.when(pl.program_id(0) % tiles_per_seq == 0)
    def _():
        hist[...] = jnp.zeros(hist.shape, F32)

    r_b = DA_WIDTH
    r_c = r_b + SC_WIDTH
    r_d = r_c + S5_WIDTH
    x1 = (x_ref[...] + _dot(oa_ref[...], wo_ref[0:r_b, :])
          + _dot(obd_ref[:, 0:SC_WIDTH], wo_ref[r_b:r_c, :])
          + _dot(oc_ref[...], wo_ref[r_c:r_d, :])
          + _dot(obd_ref[:, SC_WIDTH:SC_WIDTH + M2_WIDTH], wo_ref[r_d:D_MIX, :]))
    ms = jnp.mean(x1 * x1, axis=-1, keepdims=True)
    h2 = (x1 * lax.rsqrt(ms + EPS) * g2_ref[...]).astype(BF16)
    o_ref[...] = x1
    for c in range(D_FF // tf):
        sl = slice(c * tf, (c + 1) * tf)
        gpre = _dot(h2, wg_ref[:, sl])
        cb[0:HIST, :] = hist[:, sl]
        cb[HIST:HIST + tm, :] = gpre
        gc = (cw_ref[2:3, sl] * gpre + cw_ref[1:2, sl] * cb[HIST - 1:HIST - 1 + tm, :]
              + cw_ref[0:1, sl] * cb[HIST - 2:HIST - 2 + tm, :])
        hist[:, sl] = cb[tm:tm + HIST, :]
        act = (_silu(gc) * _dot(h2, wu_ref[:, sl])).astype(BF16)
        o_ref[...] += _dot(act, wd_ref[sl, :])


def _outproj_ffn(x2, oa, obd, oc, wo, g2, wg, wu, cw, wd, layer, seq, tm, tf):
    t = x2.shape[0]
    kern = functools.partial(_ffn_kernel, tiles_per_seq=seq // tm, tf=tf)
    row = lambda i: (i, 0)
    return pl.pallas_call(
        kern,
        grid=(t // tm,),
        in_specs=[
            pl.BlockSpec((tm, D_MODEL), row),
            pl.BlockSpec((tm, DA_WIDTH), row),
            pl.BlockSpec((tm, SC_WIDTH + M2_WIDTH), row),
            pl.BlockSpec((tm, S5_WIDTH), row),
            _layer_spec(wo.shape, layer),
            _const_spec((1, D_MODEL)),
            _layer_spec(wg.shape, layer),
            _layer_spec(wu.shape, layer),
            _const_spec((3, D_FF)),
            _layer_spec(wd.shape, layer),
        ],
        out_specs=pl.BlockSpec((tm, D_MODEL), row),
        out_shape=jax.ShapeDtypeStruct((t, D_MODEL), F32),
        scratch_shapes=[
            pltpu.VMEM((HIST, D_FF), F32),
            pltpu.VMEM((tm + HIST, tf), F32),
        ],
        compiler_params=pltpu.CompilerParams(
            dimension_semantics=("arbitrary",), vmem_limit_bytes=VMEM_LIMIT),
        name="outproj_ffn",
    )(x2, oa, obd, oc, wo, g2, wg, wu, cw, wd)


def _rope_lane_tables(seq):
    inv = 1.0 / (ROPE_THETA ** (jnp.arange(0, ROPE_DIM, 2, dtype=F32) / ROPE_DIM))
    ang = jnp.arange(seq, dtype=F32)[:, None] * inv[None, :]
    cos, sin = jnp.cos(ang), jnp.sin(ang)
    ones = jnp.ones((seq, DA_QK - ROPE_DIM), F32)
    cos64 = jnp.concatenate([cos, cos, ones], axis=1)
    sin64 = jnp.concatenate([-sin, sin, 0.0 * ones], axis=1)
    return jnp.tile(cos64, (1, LANES // DA_QK)), jnp.tile(sin64, (1, LANES // DA_QK))


def _forward(x, p, blk, mix_l, s5_tr):
    bsz, seq, _ = x.shape
    depth = p["w_in"].shape[0]
    t = bsz * seq
    ls = S5_CHUNK
    nc = seq // ls
    assert seq % blk == 0 and seq % mix_l == 0 and nc % s5_tr == 0

    cos_t, sin_t = _rope_lane_tables(seq)
    comp = lax.broadcasted_iota(jnp.int32, (QK_COLS, QK_COLS), 0) // DA_QK
    gm = (comp == comp.T).astype(BF16)
    w_out = p["w_out"].astype(BF16)
    w_gate, w_up, w_down = (p[k].astype(BF16) for k in ("ffn_w_gate", "ffn_w_up", "ffn_w_down"))
    s5_ops = jax.vmap(functools.partial(_s5_operators, ls=ls))(
        p["s5_lam_re"], p["s5_lam_im"], p["s5_log_dt"], p["s5_b_re"], p["s5_b_im"],
        p["s5_c_re"], p["s5_c_im"])
    x2 = x.reshape(t, D_MODEL)
    for l in range(depth):
        lambda_init = 0.8 - 0.6 * math.exp(-0.3 * l)
        qkg = jnp.tile(p["qk_norm_g"][l], (1, LANES // DA_QK))
        pad4 = lambda a: jnp.pad(a, (0, LANES - M2_HEADS))[None, :]
        mix_params = (p["sc_conv_w"][l], p["m2_conv_w"][l], p["m2_conv_b"][l][None, :],
                      pad4(p["m2_dt_bias"][l]), pad4(p["m2_a_log"][l]),
                      jnp.repeat(p["m2_d"][l], M2_HEAD_DIM)[None, :], p["m2_norm_g"][l][None, :])
        qt, kb, vtb, obd, u5 = _inproj(x2, p["ln1_g"][l][None, :], p["w_in"], l, qkg, cos_t, sin_t,
                                       gm, mix_params, bsz, seq, blk, mix_l)
        oa = _attention(qt, kb, vtb, p["da_lambda"][l], p["subln_g"][l][None, :], lambda_init)
        oc = _s5_scan(u5, *s5_ops, l, p["s5_d"][l][None, :], p["s5_w_glu"][l].astype(BF16),
                      p["s5_b_glu"][l][None, :], ls, nc, s5_tr)
        x2 = _outproj_ffn(x2, oa.reshape(t, DA_WIDTH), obd, oc, w_out,
                          p["ln2_g"][l][None, :], w_gate, w_up, p["ffn_conv_w"][l], w_down,
                          l, seq, min(FFN_ROWS, seq), FFN_COLS)
    return x2.reshape(bsz, seq, D_MODEL)


def kernel(x, ln1_g, w_in, qk_norm_g, da_lambda, subln_g, sc_conv_w, s5_lam_re, s5_lam_im, s5_log_dt, s5_b_re, s5_b_im, s5_c_re, s5_c_im, s5_d, s5_w_glu, s5_b_glu, m2_conv_w, m2_conv_b, m2_dt_bias, m2_a_log, m2_d, m2_norm_g, w_out, ln2_g, ffn_w_gate, ffn_w_up, ffn_conv_w, ffn_w_down):
    params = dict(ln1_g=ln1_g, w_in=w_in, qk_norm_g=qk_norm_g, da_lambda=da_lambda, subln_g=subln_g,
                  sc_conv_w=sc_conv_w, s5_lam_re=s5_lam_re, s5_lam_im=s5_lam_im, s5_log_dt=s5_log_dt,
                  s5_b_re=s5_b_re, s5_b_im=s5_b_im, s5_c_re=s5_c_re, s5_c_im=s5_c_im, s5_d=s5_d,
                  s5_w_glu=s5_w_glu, s5_b_glu=s5_b_glu, m2_conv_w=m2_conv_w, m2_conv_b=m2_conv_b,
                  m2_dt_bias=m2_dt_bias, m2_a_log=m2_a_log, m2_d=m2_d, m2_norm_g=m2_norm_g,
                  w_out=w_out, ln2_g=ln2_g, ffn_w_gate=ffn_w_gate, ffn_w_up=ffn_w_up,
                  ffn_conv_w=ffn_conv_w, ffn_w_down=ffn_w_down)
    seq = x.shape[1]
    return _forward(x, params, blk=min(512, seq), mix_l=min(256, seq),
                    s5_tr=min(256, seq // S5_CHUNK))
```

```python
import functools
import math

import jax
import jax.numpy as jnp
from jax import lax
from jax.experimental import pallas as pl
from jax.experimental.pallas import tpu as pltpu

F32 = jnp.float32
BF16 = jnp.bfloat16

D_MODEL = 1024
DA_HEADS = 4
DA_QK = 64
DA_V = 2 * DA_QK
DA_VX = DA_V + 16
DA_WIDTH = DA_HEADS * DA_V
ROPE_DIM = DA_QK // 4
ROPE_THETA = 500000.0
SC_WIDTH = 256
S5_WIDTH = 256
S5_GROUP = 16
S5_GROUPS = S5_WIDTH // S5_GROUP
S5_STATE = 64
M2_HEADS = 4
M2_HEAD_DIM = 64
M2_WIDTH = M2_HEADS * M2_HEAD_DIM
M2_GROUPS = 2
M2_STATE = 128
M2_XBC = M2_WIDTH + 2 * M2_GROUPS * M2_STATE
D_MIX = DA_WIDTH + SC_WIDTH + S5_WIDTH + M2_WIDTH
D_FF = 2816
EPS = 1e-6
LOG2E = 1.4426950408889634

LANES = 128
HIST = 8
VMEM_LIMIT = 56 * 1024 * 1024

W_Q = 0
W_K = W_Q + DA_WIDTH
W_V = W_K + DA_WIDTH
W_SC = W_V + DA_WIDTH
W_S5 = W_SC + 3 * SC_WIDTH
W_Z = W_S5 + S5_WIDTH
W_DT = W_Z + M2_WIDTH + M2_XBC
D_PROJ = W_DT + M2_HEADS
R_SC = 0
R_Z = R_SC + 3 * SC_WIDTH
R_XBC = R_Z + M2_WIDTH
R_DT = R_XBC + M2_XBC
D_REST = R_DT + LANES
QK_COLS = 2 * LANES
ATT_HEADS_PER_STEP = 2
FFN_ROWS = 512
FFN_COLS = 2816

S5_CHUNK = 8
S5_HALF = LANES // S5_GROUP
S5_HSTATE = S5_HALF * S5_STATE


def _dot(a, b):
    return jnp.dot(a, b, preferred_element_type=F32)


def _dot_nt(a, b):
    return lax.dot_general(a, b, (((1,), (1,)), ((), ())), preferred_element_type=F32)


def _dot_tn(a, b):
    return lax.dot_general(a, b, (((0,), (0,)), ((), ())), preferred_element_type=F32)


def _split_bf16(x, parts):
    out = []
    r = x
    for _ in range(parts):
        p = r.astype(BF16)
        out.append(p)
        r = r - p.astype(F32)
    return out


def _sigmoid(x):
    return 0.5 + 0.5 * jnp.tanh(0.5 * x)


def _silu(x):
    return x * _sigmoid(x)


def _const_spec(shape):
    nd = len(shape)
    return pl.BlockSpec(shape, lambda *_: (0,) * nd, pipeline_mode=pl.Buffered(1))


def _layer_spec(shape, layer):
    nd = len(shape)
    return pl.BlockSpec((None,) + tuple(shape[1:]), lambda *_: (layer,) + (0,) * (nd - 1),
                        pipeline_mode=pl.Buffered(1))


def _inproj_kernel(x_ref, g1_ref, wf_ref, qkg_ref, cos_ref, sin_ref, gm_ref,
                   scw_ref, mcw_ref, mcb_ref, dtb_ref, alog_ref, dvec_ref, ng_ref,
                   qt_ref, kb_ref, vt_ref, obd_ref, u5_ref,
                   w_ref, wdt_ref, rest_ref, cbuf, mbuf, hst, ybuf, *, tiles_per_seq, mix_l):
    tm = x_ref.shape[0]

    @pl.when(pl.program_id(0) == 0)
    def _():
        step = 4 * LANES
        for c0 in range(0, W_DT, step):
            w_ref[:, c0:c0 + step] = wf_ref[:, c0:c0 + step].astype(BF16)
        wdt_ref[...] = jnp.zeros(wdt_ref.shape, BF16)
        wdt_ref[:, 0:M2_HEADS] = wf_ref[:, W_DT:D_PROJ].astype(BF16)

    @pl.when(pl.program_id(0) % tiles_per_seq == 0)
    def _():
        _mix_reset(cbuf, mbuf, hst)

    x = x_ref[...]
    ms = jnp.mean(x * x, axis=-1, keepdims=True)
    hn = (x * lax.rsqrt(ms + EPS) * g1_ref[...]).astype(BF16)
    rest_ref[:, R_SC:R_Z] = _dot(hn, w_ref[:, W_SC:W_S5])
    rest_ref[:, R_Z:R_DT] = _dot(hn, w_ref[:, W_Z:W_DT])
    rest_ref[:, R_DT:D_REST] = _dot(hn, wdt_ref[...])
    for r0 in range(0, tm, mix_l):
        _mix_chunk(rest_ref, r0, mix_l, scw_ref, mcw_ref, mcb_ref, dtb_ref, alog_ref, dvec_ref, ng_ref,
                   obd_ref, cbuf, mbuf, hst, ybuf)
    cosv = cos_ref[...]
    sinv = sin_ref[...]
    gm = gm_ref[...]
    lane = lax.broadcasted_iota(jnp.int32, (tm, LANES), 1)
    pair_up = (lane % DA_QK) < (ROPE_DIM // 2)
    heads_per_dot = QK_COLS // DA_V

    def qk_heads(col0, gvec, scale):
        y = _dot(hn, w_ref[:, col0:col0 + QK_COLS])
        hi, lo = _split_bf16(y * y, 2)
        ss = _dot(hi, gm) + _dot(lo, gm)
        yn = y * lax.rsqrt(ss * (1.0 / DA_QK) + EPS)
        out = []
        for j in range(heads_per_dot):
            yh = yn[:, j * DA_V:(j + 1) * DA_V] * gvec
            partner = jnp.where(pair_up,
                                pltpu.roll(yh, LANES - ROPE_DIM // 2, 1),
                                pltpu.roll(yh, ROPE_DIM // 2, 1))
            out.append((yh * cosv + partner * sinv) * scale)
        return out

    for h0 in range(0, DA_HEADS, heads_per_dot):
        qs = qk_heads(W_Q + h0 * DA_V, qkg_ref[0:1, :], DA_QK ** -0.5 * LOG2E)
        ks = qk_heads(W_K + h0 * DA_V, qkg_ref[1:2, :], 1.0)
        v = _dot(hn, w_ref[:, W_V + h0 * DA_V:W_V + h0 * DA_V + QK_COLS])
        for j in range(heads_per_dot):
            qt_ref[h0 + j] = qs[j].T.astype(BF16)
            kb_ref[h0 + j] = ks[j].astype(BF16)
            vt_ref[h0 + j, 0:DA_V, :] = v[:, j * DA_V:(j + 1) * DA_V].T.astype(BF16)
            vt_ref[h0 + j, DA_V:DA_VX, :] = jnp.ones((DA_VX - DA_V, tm), BF16)
    u5 = _dot(hn, w_ref[:, W_S5:W_Z])
    for hh in range(S5_WIDTH // LANES):
        u5_ref[hh] = u5[:, hh * LANES:(hh + 1) * LANES]


def _inproj(x2, g1, w, layer, qkg, cos_t, sin_t, gm, mix_params, bsz, seq, tm, mix_l):
    t = x2.shape[0]
    nt = seq // tm
    row = lambda i: (i, 0)
    tab = lambda i: (i % nt, 0)
    kern = functools.partial(_inproj_kernel, tiles_per_seq=nt, mix_l=mix_l)
    return pl.pallas_call(
        kern,
        grid=(t // tm,),
        in_specs=[
            pl.BlockSpec((tm, D_MODEL), row),
            _const_spec((1, D_MODEL)),
            _layer_spec(w.shape, layer),
            _const_spec((2, LANES)),
            pl.BlockSpec((tm, LANES), tab),
            pl.BlockSpec((tm, LANES), tab),
            _const_spec((QK_COLS, QK_COLS)),
        ] + [_const_spec(a.shape) for a in mix_params],
        out_specs=[
            pl.BlockSpec((None, DA_HEADS, DA_V, tm), lambda i: (i // nt, 0, 0, i % nt)),
            pl.BlockSpec((None, DA_HEADS, None, tm, DA_V), lambda i: (i // nt, 0, i % nt, 0, 0)),
            pl.BlockSpec((None, DA_HEADS, None, DA_VX, tm), lambda i: (i // nt, 0, i % nt, 0, 0)),
            pl.BlockSpec((tm, SC_WIDTH + M2_WIDTH), row),
            pl.BlockSpec((S5_WIDTH // LANES, tm, LANES), lambda i: (0, i, 0)),
        ],
        out_shape=[
            jax.ShapeDtypeStruct((bsz, DA_HEADS, DA_V, seq), BF16),
            jax.ShapeDtypeStruct((bsz, DA_HEADS, nt, tm, DA_V), BF16),
            jax.ShapeDtypeStruct((bsz, DA_HEADS, nt, DA_VX, tm), BF16),
            jax.ShapeDtypeStruct((t, SC_WIDTH + M2_WIDTH), BF16),
            jax.ShapeDtypeStruct((S5_WIDTH // LANES, t, LANES), F32),
        ],
        scratch_shapes=[
            pltpu.VMEM((D_MODEL, W_DT), BF16),
            pltpu.VMEM((D_MODEL, LANES), BF16),
            pltpu.VMEM((tm, D_REST), F32),
            pltpu.VMEM((mix_l + HIST, SC_WIDTH), F32),
            pltpu.VMEM((mix_l + HIST, M2_XBC), F32),
            pltpu.VMEM((M2_GROUPS, M2_STATE, 2 * M2_HEAD_DIM), F32),
            pltpu.VMEM((mix_l, M2_WIDTH), F32),
        ],
        compiler_params=pltpu.CompilerParams(
            dimension_semantics=("arbitrary",), vmem_limit_bytes=VMEM_LIMIT),
        name="inproj_mix",
    )(x2, g1, w, qkg, cos_t, sin_t, gm, *mix_params)


def _attn_kernel(qt_ref, k_ref, vt_ref, lam_ref, sg_ref, o_ref, qz_sc, s_sc, m_sc, acc_sc,
                 *, lambda_init):
    qi = pl.program_id(2)
    heads = qt_ref.shape[0]
    streams = [(hd, c) for hd in range(heads) for c in range(2)]
    for n, (hd, c) in enumerate(streams):
        qt = qt_ref[hd]
        comp = lax.broadcasted_iota(jnp.int32, qt.shape, 0) // DA_QK
        qz_sc[n] = jnp.where(comp == c, qt, jnp.zeros_like(qt))
    m_sc[...] = jnp.full(m_sc.shape, -1e30, F32)
    acc_sc[...] = jnp.zeros(acc_sc.shape, F32)

    def scores(j, buf):
        for n, (hd, c) in enumerate(streams):
            s_sc[buf, n] = _dot(k_ref[hd, j], qz_sc[n])

    def absorb(j, buf, masked):
        for n, (hd, c) in enumerate(streams):
            s = s_sc[buf, n]
            if masked:
                kpos = lax.broadcasted_iota(jnp.int32, s.shape, 0)
                qpos = lax.broadcasted_iota(jnp.int32, s.shape, 1)
                s = jnp.where(kpos <= qpos, s, -jnp.inf)
            m_prev = m_sc[n]
            m_new = jnp.maximum(m_prev, jnp.max(s, axis=0, keepdims=True))
            alpha = jnp.exp2(m_prev - m_new)
            p = jnp.exp2(s - m_new).astype(BF16)
            acc_sc[n] = alpha * acc_sc[n] + _dot(vt_ref[hd, j], p)
            m_sc[n] = m_new

    scores(0, 0)
    pairs = qi // 2

    def body(t, carry):
        j = 2 * t
        scores(j + 1, 1)
        absorb(j, 0, False)
        scores(j + 2, 0)
        absorb(j + 1, 1, False)
        return carry

    def body2(t, carry):
        return body(2 * t + 1, body(2 * t, carry))

    lax.fori_loop(0, pairs // 2, body2, 0)

    @pl.when(pairs % 2 == 1)
    def _():
        body(pairs - 1, 0)

    j0 = 2 * pairs

    @pl.when(j0 == qi)
    def _():
        absorb(qi, 0, True)

    @pl.when(j0 != qi)
    def _():
        scores(qi, 1)
        absorb(j0, 0, False)
        absorb(qi, 1, True)

    lp = lam_ref[...]
    lam = (jnp.exp(jnp.sum(lp[0:1] * lp[1:2], axis=1, keepdims=True))
           - jnp.exp(jnp.sum(lp[2:3] * lp[3:4], axis=1, keepdims=True)) + lambda_init)
    for hd in range(heads):
        a0, a1 = acc_sc[2 * hd], acc_sc[2 * hd + 1]
        ot = (a0[0:DA_V] / a0[DA_V:DA_V + 1] - lam * (a1[0:DA_V] / a1[DA_V:DA_V + 1]))
        ms = jnp.mean(ot * ot, axis=0, keepdims=True)
        o = (ot * lax.rsqrt(ms + EPS)).T * (sg_ref[...] * (1.0 - lambda_init))
        o_ref[:, hd * DA_V:(hd + 1) * DA_V] = o.astype(o_ref.dtype)


def _attention(qt, kb, vtb, lam_p, sub_g, lambda_init):
    bsz, _, _, seq = qt.shape
    nb, blk = kb.shape[2], kb.shape[3]
    hps = ATT_HEADS_PER_STEP
    kern = functools.partial(_attn_kernel, lambda_init=lambda_init)
    return pl.pallas_call(
        kern,
        grid=(bsz, DA_HEADS // hps, nb),
        in_specs=[
            pl.BlockSpec((None, hps, DA_V, blk), lambda b, h, i: (b, h, 0, i)),
            pl.BlockSpec((None, hps, nb, blk, DA_V), lambda b, h, i: (b, h, 0, 0, 0)),
            pl.BlockSpec((None, hps, nb, DA_VX, blk), lambda b, h, i: (b, h, 0, 0, 0)),
            pl.BlockSpec((4, DA_QK), lambda b, h, i: (0, 0)),
            pl.BlockSpec((1, DA_V), lambda b, h, i: (0, 0)),
        ],
        out_specs=pl.BlockSpec((None, blk, hps * DA_V), lambda b, h, i: (b, i, h)),
        out_shape=jax.ShapeDtypeStruct((bsz, seq, DA_WIDTH), BF16),
        scratch_shapes=[
            pltpu.VMEM((2 * hps, DA_V, blk), BF16),
            pltpu.VMEM((2, 2 * hps, blk, blk), F32),
            pltpu.VMEM((2 * hps, 1, blk), F32),
            pltpu.VMEM((2 * hps, DA_VX, blk), F32),
        ],
        compiler_params=pltpu.CompilerParams(
            dimension_semantics=("arbitrary", "arbitrary", "arbitrary"),
            vmem_limit_bytes=VMEM_LIMIT),
        name="diff_attn",
    )(qt, kb, vtb, lam_p, sub_g)


def _expand_blockdiag(src_ref, dst_ref, row_group, col_div, col_inner):
    rows, wide = dst_ref.shape[1], dst_ref.shape[2]
    r_s = lax.broadcasted_iota(jnp.int32, (LANES, wide), 0)
    c_s = lax.broadcasted_iota(jnp.int32, (LANES, wide), 1)
    sel = jnp.where(r_s == (c_s // col_div) * col_inner + c_s % col_inner, 1.0, 0.0).astype(BF16)
    step = 256
    r_i = lax.broadcasted_iota(jnp.int32, (step, wide), 0)
    c_i = lax.broadcasted_iota(jnp.int32, (step, wide), 1)
    col_group = (c_i // (col_div // S5_HALF)) % S5_HALF
    for hh in range(dst_ref.shape[0]):
        for r0 in range(0, rows, step):
            same = ((r_i + r0) // row_group) % S5_HALF == col_group
            wide_blk = _dot(src_ref[hh, r0:r0 + step, :], sel)
            dst_ref[hh, r0:r0 + step, :] = jnp.where(same, wide_blk, 0.0).astype(BF16)


def _s5_kernel(u_ref, tc_ref, bc_ref, cc_ref, lam_ref, s5d_ref, wglu_ref, bglu_ref, o_ref,
               toep_ref, bpow_ref, cpow_ref, st, hc, y_ref, *, ls, tiles_per_seq):
    tr = u_ref.shape[1] // ls
    hs = S5_HSTATE

    @pl.when(pl.program_id(0) == 0)
    def _():
        _expand_blockdiag(tc_ref, toep_ref, S5_GROUP, LANES, S5_GROUP)
        _expand_blockdiag(bc_ref, bpow_ref, S5_GROUP, S5_HSTATE, S5_STATE)
        _expand_blockdiag(cc_ref, cpow_ref, S5_STATE, LANES, S5_GROUP)

    @pl.when(pl.program_id(0) % tiles_per_seq == 0)
    def _():
        hc[...] = jnp.zeros(hc.shape, F32)

    def half_input(hh):
        cols = [u_ref[hh, pl.ds(i, tr, stride=ls), :] for i in range(ls)]
        return jnp.concatenate(cols, axis=1).astype(BF16)

    for hh in range(2):
        st[hh] = _dot(half_input(hh), bpow_ref[hh])

    lam = [(lam_ref[hh, 0:1, :], lam_ref[hh, 1:2, :]) for hh in range(2)]

    def body(c, carry):
        new = []
        for hh in range(2):
            hr, hi = carry[2 * hh], carry[2 * hh + 1]
            lr, li = lam[hh]
            s_r = st[hh, pl.ds(c, 1), 0:hs]
            s_i = st[hh, pl.ds(c, 1), hs:2 * hs]
            st[hh, pl.ds(c, 1), 0:hs] = hr
            st[hh, pl.ds(c, 1), hs:2 * hs] = hi
            new.append(lr * hr - li * hi + s_r)
            new.append(lr * hi + li * hr + s_i)
        return tuple(new)

    init = tuple(hc[hh, :, ri * hs:(ri + 1) * hs] for hh in range(2) for ri in range(2))
    fin = init
    for c in range(tr):
        fin = body(c, fin)
    for hh in range(2):
        for ri in range(2):
            hc[hh, :, ri * hs:(ri + 1) * hs] = fin[2 * hh + ri]

    for hh in range(2):
        y = _dot(half_input(hh), toep_ref[hh]) + _dot(st[hh].astype(BF16), cpow_ref[hh])
        for i in range(ls):
            y_ref[hh, pl.ds(i, tr, stride=ls), :] = y[:, i * LANES:(i + 1) * LANES]

    halves = lambda ref: jnp.concatenate([ref[hh] for hh in range(S5_WIDTH // LANES)], axis=1)
    y5 = halves(y_ref) + s5d_ref[...] * halves(u_ref)
    gl = jax.nn.gelu(y5, approximate=True)
    gate = _dot(gl.astype(BF16), wglu_ref[...]) + bglu_ref[...]
    o_ref[...] = (gl * _sigmoid(gate)).astype(o_ref.dtype)


def _s5_scan(u5, tc, bc, cc, lam_l, layer, s5d, wglu, bglu, ls, nc, tr):
    nh, t, w = u5.shape
    rows = tr * ls
    wide = ls * LANES
    kern = functools.partial(_s5_kernel, ls=ls, tiles_per_seq=nc // tr)
    row = lambda i: (0, i, 0)
    return pl.pallas_call(
        kern,
        grid=(t // rows,),
        in_specs=[
            pl.BlockSpec((nh, rows, w), row),
            _layer_spec(tc.shape, layer),
            _layer_spec(bc.shape, layer),
            _layer_spec(cc.shape, layer),
            _layer_spec(lam_l.shape, layer),
            _const_spec(s5d.shape),
            _const_spec(wglu.shape),
            _const_spec(bglu.shape),
        ],
        out_specs=pl.BlockSpec((rows, S5_WIDTH), lambda i: (i, 0)),
        out_shape=jax.ShapeDtypeStruct((t, S5_WIDTH), BF16),
        scratch_shapes=[
            pltpu.VMEM((nh, wide, wide), BF16),
            pltpu.VMEM((nh, wide, 2 * S5_HSTATE), BF16),
            pltpu.VMEM((nh, 2 * S5_HSTATE, wide), BF16),
            pltpu.VMEM((nh, tr, 2 * S5_HSTATE), F32),
            pltpu.VMEM((nh, 1, 2 * S5_HSTATE), F32),
            pltpu.VMEM((nh, rows, w), F32),
        ],
        compiler_params=pltpu.CompilerParams(
            dimension_semantics=("arbitrary",), vmem_limit_bytes=VMEM_LIMIT),
        name="s5_mixer",
    )(u5, tc, bc, cc, lam_l, s5d, wglu, bglu)


def _s5_operators(lam_re, lam_im, log_dt, b_re, b_im, c_re, c_im, ls):
    hp = lax.Precision.HIGHEST
    dt = jnp.exp(log_dt)[:, None]
    zr, zi = lam_re * dt, lam_im * dt
    er = jnp.exp(zr)
    lbr, lbi = er * jnp.cos(zi), er * jnp.sin(zi)
    den = lam_re * lam_re + lam_im * lam_im
    nr, ni = lbr - 1.0, lbi
    fr = (nr * lam_re + ni * lam_im) / den
    fi = (ni * lam_re - nr * lam_im) / den
    bbr = fr[..., None] * b_re - fi[..., None] * b_im
    bbi = fr[..., None] * b_im + fi[..., None] * b_re
    tau = jnp.arange(ls + 1, dtype=F32)
    pe = jnp.exp(zr[..., None] * tau)
    pr, pi_ = pe * jnp.cos(zi[..., None] * tau), pe * jnp.sin(zi[..., None] * tau)
    nh, gh = S5_GROUPS // S5_HALF, S5_HALF
    w = ls * LANES
    lagw = ls * S5_GROUP
    c_rt, c_it = jnp.swapaxes(c_re, 1, 2)[:, :, None, :], jnp.swapaxes(c_im, 1, 2)[:, :, None, :]
    cpr = (c_rt * pr[:, :, :ls, None] - c_it * pi_[:, :, :ls, None]).reshape(S5_GROUPS, S5_STATE, lagw)
    cpi = (c_rt * pi_[:, :, :ls, None] + c_it * pr[:, :, :ls, None]).reshape(S5_GROUPS, S5_STATE, lagw)
    kf = (jnp.matmul(jnp.swapaxes(bbr, 1, 2), cpr, precision=hp)
          - jnp.matmul(jnp.swapaxes(bbi, 1, 2), cpi, precision=hp))
    kt = jnp.stack([jnp.pad(kf[..., :lagw - s * S5_GROUP], ((0, 0), (0, 0), (s * S5_GROUP, 0)))
                    for s in range(ls)], axis=0)
    tc = kt.reshape(ls, nh, gh, S5_GROUP, lagw).transpose(1, 0, 2, 3, 4).reshape(nh, w, lagw)
    tau_rev = (ls - 1) - jnp.arange(ls, dtype=F32)
    pe_rev = jnp.exp(zr[..., None] * tau_rev)
    rev_r, rev_i = pe_rev * jnp.cos(zi[..., None] * tau_rev), pe_rev * jnp.sin(zi[..., None] * tau_rev)
    bre = rev_r[..., None] * bbr[:, :, None, :] - rev_i[..., None] * bbi[:, :, None, :]
    bim = rev_r[..., None] * bbi[:, :, None, :] + rev_i[..., None] * bbr[:, :, None, :]
    bb = jnp.stack([bre, bim], axis=0).reshape(2, nh, gh, S5_STATE, ls, S5_GROUP)
    bc = bb.transpose(1, 4, 2, 5, 0, 3).reshape(nh, w, 2 * S5_STATE)
    qr = c_re[:, :, :, None] * pr[:, None, :, 1:] - c_im[:, :, :, None] * pi_[:, None, :, 1:]
    qi = c_re[:, :, :, None] * pi_[:, None, :, 1:] + c_im[:, :, :, None] * pr[:, None, :, 1:]
    cc = jnp.stack([qr, -qi], axis=0).reshape(2, nh, gh, S5_GROUP, S5_STATE, ls)
    cc = cc.transpose(1, 0, 2, 4, 5, 3).reshape(nh, 2 * S5_HSTATE, ls * S5_GROUP)
    lam_l = jnp.stack([pr[:, :, ls].reshape(nh, S5_HSTATE), pi_[:, :, ls].reshape(nh, S5_HSTATE)], axis=1)
    return tc.astype(BF16), bc.astype(BF16), cc.astype(BF16), lam_l.astype(F32)


def _mix_reset(cbuf, mbuf, hst):
    cbuf[0:HIST, :] = jnp.zeros((HIST, SC_WIDTH), F32)
    mbuf[0:HIST, :] = jnp.zeros((HIST, M2_XBC), F32)
    hst[...] = jnp.zeros(hst.shape, F32)


def _mix_chunk(rest_ref, r0, L, scw_ref, mcw_ref, mcb_ref, dtb_ref, alog_ref, dvec_ref, ng_ref,
               o_ref, cbuf, mbuf, hst, ybuf):
    rows = slice(r0, r0 + L)
    u = rest_ref[rows, R_SC + SC_WIDTH:R_SC + 2 * SC_WIDTH] * rest_ref[rows, R_SC + 2 * SC_WIDTH:R_SC + 3 * SC_WIDTH]
    cbuf[HIST:HIST + L, :] = u
    conv = (scw_ref[2:3, :] * u + scw_ref[1:2, :] * cbuf[HIST - 1:HIST - 1 + L, :]
            + scw_ref[0:1, :] * cbuf[HIST - 2:HIST - 2 + L, :])
    cbuf[0:HIST, :] = cbuf[L:L + HIST, :]
    o_ref[rows, 0:SC_WIDTH] = (rest_ref[rows, R_SC:R_SC + SC_WIDTH] * conv).astype(o_ref.dtype)

    xr = rest_ref[rows, R_XBC:R_XBC + M2_XBC]
    mbuf[HIST:HIST + L, :] = xr
    conv = (mcw_ref[3:4, :] * xr + mcw_ref[2:3, :] * mbuf[HIST - 1:HIST - 1 + L, :]
            + mcw_ref[1:2, :] * mbuf[HIST - 2:HIST - 2 + L, :]
            + mcw_ref[0:1, :] * mbuf[HIST - 3:HIST - 3 + L, :] + mcb_ref[...])
    mbuf[0:HIST, :] = mbuf[L:L + HIST, :]
    xbc = _silu(conv)
    xs = xbc[:, 0:M2_WIDTH]
    dtr = rest_ref[rows, R_DT:R_DT + LANES] + dtb_ref[...]
    dtv = jnp.maximum(dtr, 0.0) + jnp.log(1.0 + jnp.exp(-jnp.abs(dtr)))
    a = -jnp.exp(alog_ref[...]) * dtv
    r_i = lax.broadcasted_iota(jnp.int32, (L, L), 0)
    c_i = lax.broadcasted_iota(jnp.int32, (L, L), 1)
    tril = c_i <= r_i
    tri = jnp.where(tril, 1.0, 0.0).astype(BF16)
    cs = sum(_dot(tri, part) for part in _split_bf16(a, 3))
    cs_t = cs.T
    cs_last = cs[L - 1:L, :]
    ecs = jnp.exp(cs)
    dec = jnp.exp(cs_last - cs)
    ecl = jnp.exp(cs_last)
    lane = lax.broadcasted_iota(jnp.int32, (1, LANES), 1)
    for g in range(M2_GROUPS):
        bg = xbc[:, M2_WIDTH + g * M2_STATE:M2_WIDTH + (g + 1) * M2_STATE].astype(BF16)
        cg = xbc[:, M2_WIDTH + (M2_GROUPS + g) * M2_STATE:M2_WIDTH + (M2_GROUPS + g + 1) * M2_STATE].astype(BF16)
        gram = _dot_nt(cg, bg)
        hs = hst[g]
        yoff = _dot(cg, hs.astype(BF16))
        xdec = []
        for hh in range(2):
            h = 2 * g + hh
            seg = cs[:, h:h + 1] - cs_t[h:h + 1, :]
            dm = jnp.exp(jnp.where(tril, seg, -jnp.inf))
            xdt = xs[:, h * M2_HEAD_DIM:(h + 1) * M2_HEAD_DIM] * dtv[:, h:h + 1]
            yd = _dot((gram * dm).astype(BF16), xdt.astype(BF16))
            yo = yoff[:, hh * M2_HEAD_DIM:(hh + 1) * M2_HEAD_DIM] * ecs[:, h:h + 1]
            ybuf[:, h * M2_HEAD_DIM:(h + 1) * M2_HEAD_DIM] = yd + yo
            xdec.append(xdt * dec[:, h:h + 1])
        upd = _dot_tn(bg, jnp.concatenate(xdec, axis=1).astype(BF16))
        keep = jnp.where(lane < M2_HEAD_DIM, ecl[:, 2 * g:2 * g + 1], ecl[:, 2 * g + 1:2 * g + 2])
        hst[g] = hs * keep + upd
    y = ybuf[...] + dvec_ref[...] * xs
    yg = y * _silu(rest_ref[rows, R_Z:R_Z + M2_WIDTH])
    ms = jnp.mean(yg * yg, axis=-1, keepdims=True)
    o_ref[rows, SC_WIDTH:SC_WIDTH + M2_WIDTH] = (yg * lax.rsqrt(ms + EPS) * ng_ref[...]).astype(o_ref.dtype)


def _ffn_kernel(x_ref, oa_ref, obd_ref, oc_ref, wo_ref, g2_ref, wg_ref, wu_ref, cw_ref, wd_ref,
                o_ref, hist, cb, *, tiles_per_seq, tf):
    tm = x_ref.shape[0]

    @pl.when(pl.program_id(0) % tiles_per_seq == 0)
    def _():
        hist[...] = jnp.zeros(hist.shape, F32)

    r_b = DA_WIDTH
    r_c = r_b + SC_WIDTH
    r_d = r_c + S5_WIDTH
    x1 = (x_ref[...] + _dot(oa_ref[...], wo_ref[0:r_b, :])
          + _dot(obd_ref[:, 0:SC_WIDTH], wo_ref[r_b:r_c, :])
          + _dot(oc_ref[...], wo_ref[r_c:r_d, :])
          + _dot(obd_ref[:, SC_WIDTH:SC_WIDTH + M2_WIDTH], wo_ref[r_d:D_MIX, :]))
    ms = jnp.mean(x1 * x1, axis=-1, keepdims=True)
    h2 = (x1 * lax.rsqrt(ms + EPS) * g2_ref[...]).astype(BF16)
    o_ref[...] = x1
    for c in range(D_FF // tf):
        sl = slice(c * tf, (c + 1) * tf)
        gpre = _dot(h2, wg_ref[:, sl])
        cb[0:HIST, :] = hist[:, sl]
        cb[HIST:HIST + tm, :] = gpre
        gc = (cw_ref[2:3, sl] * gpre + cw_ref[1:2, sl] * cb[HIST - 1:HIST - 1 + tm, :]
              + cw_ref[0:1, sl] * cb[HIST - 2:HIST - 2 + tm, :])
        hist[:, sl] = cb[tm:tm + HIST, :]
        act = (_silu(gc) * _dot(h2, wu_ref[:, sl])).astype(BF16)
        o_ref[...] += _dot(act, wd_ref[sl, :])


def _outproj_ffn(x2, oa, obd, oc, wo, g2, wg, wu, cw, wd, layer, seq, tm, tf):
    t = x2.shape[0]
    kern = functools.partial(_ffn_kernel, tiles_per_seq=seq // tm, tf=tf)
    row = lambda i: (i, 0)
    return pl.pallas_call(
        kern,
        grid=(t // tm,),
        in_specs=[
            pl.BlockSpec((tm, D_MODEL), row),
            pl.BlockSpec((tm, DA_WIDTH), row),
            pl.BlockSpec((tm, SC_WIDTH + M2_WIDTH), row),
            pl.BlockSpec((tm, S5_WIDTH), row),
            _layer_spec(wo.shape, layer),
            _const_spec((1, D_MODEL)),
            _layer_spec(wg.shape, layer),
            _layer_spec(wu.shape, layer),
            _const_spec((3, D_FF)),
            _layer_spec(wd.shape, layer),
        ],
        out_specs=pl.BlockSpec((tm, D_MODEL), row),
        out_shape=jax.ShapeDtypeStruct((t, D_MODEL), F32),
        scratch_shapes=[
            pltpu.VMEM((HIST, D_FF), F32),
            pltpu.VMEM((tm + HIST, tf), F32),
        ],
        compiler_params=pltpu.CompilerParams(
            dimension_semantics=("arbitrary",), vmem_limit_bytes=VMEM_LIMIT),
        name="outproj_ffn",
    )(x2, oa, obd, oc, wo, g2, wg, wu, cw, wd)


def _rope_lane_tables(seq):
    inv = 1.0 / (ROPE_THETA ** (jnp.arange(0, ROPE_DIM, 2, dtype=F32) / ROPE_DIM))
    ang = jnp.arange(seq, dtype=F32)[:, None] * inv[None, :]
    cos, sin = jnp.cos(ang), jnp.sin(ang)
    ones = jnp.ones((seq, DA_QK - ROPE_DIM), F32)
    cos64 = jnp.concatenate([cos, cos, ones], axis=1)
    sin64 = jnp.concatenate([-sin, sin, 0.0 * ones], axis=1)
    return jnp.tile(cos64, (1, LANES // DA_QK)), jnp.tile(sin64, (1, LANES // DA_QK))


def _forward(x, p, blk, mix_l, s5_tr):
    bsz, seq, _ = x.shape
    depth = p["w_in"].shape[0]
    t = bsz * seq
    ls = S5_CHUNK
    nc = seq // ls
    assert seq % blk == 0 and seq % mix_l == 0 and nc % s5_tr == 0

    cos_t, sin_t = _rope_lane_tables(seq)
    comp = lax.broadcasted_iota(jnp.int32, (QK_COLS, QK_COLS), 0) // DA_QK
    gm = (comp == comp.T).astype(BF16)
    w_out = p["w_out"].astype(BF16)
    w_gate, w_up, w_down = (p[k].astype(BF16) for k in ("ffn_w_gate", "ffn_w_up", "ffn_w_down"))
    s5_ops = jax.vmap(functools.partial(_s5_operators, ls=ls))(
        p["s5_lam_re"], p["s5_lam_im"], p["s5_log_dt"], p["s5_b_re"], p["s5_b_im"],
        p["s5_c_re"], p["s5_c_im"])
    x2 = x.reshape(t, D_MODEL)
    for l in range(depth):
        lambda_init = 0.8 - 0.6 * math.exp(-0.3 * l)
        qkg = jnp.tile(p["qk_norm_g"][l], (1, LANES // DA_QK))
        pad4 = lambda a: jnp.pad(a, (0, LANES - M2_HEADS))[None, :]
        mix_params = (p["sc_conv_w"][l], p["m2_conv_w"][l], p["m2_conv_b"][l][None, :],
                      pad4(p["m2_dt_bias"][l]), pad4(p["m2_a_log"][l]),
                      jnp.repeat(p["m2_d"][l], M2_HEAD_DIM)[None, :], p["m2_norm_g"][l][None, :])
        qt, kb, vtb, obd, u5 = _inproj(x2, p["ln1_g"][l][None, :], p["w_in"], l, qkg, cos_t, sin_t,
                                       gm, mix_params, bsz, seq, blk, mix_l)
        oa = _attention(qt, kb, vtb, p["da_lambda"][l], p["subln_g"][l][None, :], lambda_init)
        oc = _s5_scan(u5, *s5_ops, l, p["s5_d"][l][None, :], p["s5_w_glu"][l].astype(BF16),
                      p["s5_b_glu"][l][None, :], ls, nc, s5_tr)
        x2 = _outproj_ffn(x2, oa.reshape(t, DA_WIDTH), obd, oc, w_out,
                          p["ln2_g"][l][None, :], w_gate, w_up, p["ffn_conv_w"][l], w_down,
                          l, seq, min(FFN_ROWS, seq), FFN_COLS)
    return x2.reshape(bsz, seq, D_MODEL)


def kernel(x, ln1_g, w_in, qk_norm_g, da_lambda, subln_g, sc_conv_w, s5_lam_re, s5_lam_im, s5_log_dt, s5_b_re, s5_b_im, s5_c_re, s5_c_im, s5_d, s5_w_glu, s5_b_glu, m2_conv_w, m2_conv_b, m2_dt_bias, m2_a_log, m2_d, m2_norm_g, w_out, ln2_g, ffn_w_gate, ffn_w_up, ffn_conv_w, ffn_w_down):
    params = dict(ln1_g=ln1_g, w_in=w_in, qk_norm_g=qk_norm_g, da_lambda=da_lambda, subln_g=subln_g,
                  sc_conv_w=sc_conv_w, s5_lam_re=s5_lam_re, s5_lam_im=s5_lam_im, s5_log_dt=s5_log_dt,
                  s5_b_re=s5_b_re, s5_b_im=s5_b_im, s5_c_re=s5_c_re, s5_c_im=s5_c_im, s5_d=s5_d,
                  s5_w_glu=s5_w_glu, s5_b_glu=s5_b_glu, m2_conv_w=m2_conv_w, m2_conv_b=m2_conv_b,
                  m2_dt_bias=m2_dt_bias, m2_a_log=m2_a_log, m2_d=m2_d, m2_norm_g=m2_norm_g,
                  w_out=w_out, ln2_g=ln2_g, ffn_w_gate=ffn_w_gate, ffn_w_up=ffn_w_up,
                  ffn_conv_w=ffn_conv_w, ffn_w_down=ffn_w_down)
    seq = x.shape[1]
    return _forward(x, params, blk=min(512, seq), mix_l=min(256, seq),
                    s5_tr=min(256, seq // S5_CHUNK))
```

```python
import functools
import math

import jax
import jax.numpy as jnp
from jax import lax
from jax.experimental import pallas as pl
from jax.experimental.pallas import tpu as pltpu

F32 = jnp.float32
BF16 = jnp.bfloat16

D_MODEL = 1024
DA_HEADS = 4
DA_QK = 64
DA_V = 2 * DA_QK
DA_VX = DA_V + 16
DA_WIDTH = DA_HEADS * DA_V
ROPE_DIM = DA_QK // 4
ROPE_THETA = 500000.0
SC_WIDTH = 256
S5_WIDTH = 256
S5_GROUP = 16
S5_GROUPS = S5_WIDTH // S5_GROUP
S5_STATE = 64
M2_HEADS = 4
M2_HEAD_DIM = 64
M2_WIDTH = M2_HEADS * M2_HEAD_DIM
M2_GROUPS = 2
M2_STATE = 128
M2_XBC = M2_WIDTH + 2 * M2_GROUPS * M2_STATE
D_MIX = DA_WIDTH + SC_WIDTH + S5_WIDTH + M2_WIDTH
D_FF = 2816
EPS = 1e-6
LOG2E = 1.4426950408889634

LANES = 128
HIST = 8
VMEM_LIMIT = 56 * 1024 * 1024

W_Q = 0
W_K = W_Q + DA_WIDTH
W_V = W_K + DA_WIDTH
W_SC = W_V + DA_WIDTH
W_S5 = W_SC + 3 * SC_WIDTH
W_Z = W_S5 + S5_WIDTH
W_DT = W_Z + M2_WIDTH + M2_XBC
D_PROJ = W_DT + M2_HEADS
R_SC = 0
R_Z = R_SC + 3 * SC_WIDTH
R_XBC = R_Z + M2_WIDTH
R_DT = R_XBC + M2_XBC
D_REST = R_DT + LANES
QK_COLS = 2 * LANES
ATT_HEADS_PER_STEP = 2
INPROJ_ROWS = 1024
FFN_ROWS = 512
FFN_COLS = 2816

S5_CHUNK = 8
S5_HALF = LANES // S5_GROUP
S5_HSTATE = S5_HALF * S5_STATE


def _dot(a, b):
    return jnp.dot(a, b, preferred_element_type=F32)


def _dot_nt(a, b):
    return lax.dot_general(a, b, (((1,), (1,)), ((), ())), preferred_element_type=F32)


def _dot_tn(a, b):
    return lax.dot_general(a, b, (((0,), (0,)), ((), ())), preferred_element_type=F32)


def _split_bf16(x, parts):
    out = []
    r = x
    for _ in range(parts):
        p = r.astype(BF16)
        out.append(p)
        r = r - p.astype(F32)
    return out


def _sigmoid(x):
    return 0.5 + 0.5 * jnp.tanh(0.5 * x)


def _silu(x):
    return x * _sigmoid(x)


def _const_spec(shape):
    nd = len(shape)
    return pl.BlockSpec(shape, lambda *_: (0,) * nd, pipeline_mode=pl.Buffered(1))


def _layer_spec(shape, layer):
    nd = len(shape)
    return pl.BlockSpec((None,) + tuple(shape[1:]), lambda *_: (layer,) + (0,) * (nd - 1),
                        pipeline_mode=pl.Buffered(1))


def _inproj_kernel(x_ref, g1_ref, wf_ref, qkg_ref, cos_ref, sin_ref, gm_ref,
                   scw_ref, mcw_ref, mcb_ref, dtb_ref, alog_ref, dvec_ref, ng_ref,
                   qt_ref, kb_ref, vt_ref, obd_ref, u5_ref,
                   w_ref, wdt_ref, rest_ref, cbuf, mbuf, hst, ybuf, *, tiles_per_seq, mix_l):
    tm = x_ref.shape[0]

    @pl.when(pl.program_id(0) == 0)
    def _():
        step = 4 * LANES
        for c0 in range(0, W_DT, step):
            w_ref[:, c0:c0 + step] = wf_ref[:, c0:c0 + step].astype(BF16)
        wdt_ref[...] = jnp.zeros(wdt_ref.shape, BF16)
        wdt_ref[:, 0:M2_HEADS] = wf_ref[:, W_DT:D_PROJ].astype(BF16)

    @pl.when(pl.program_id(0) % tiles_per_seq == 0)
    def _():
        _mix_reset(cbuf, mbuf, hst)

    x = x_ref[...]
    ms = jnp.mean(x * x, axis=-1, keepdims=True)
    hn = (x * lax.rsqrt(ms + EPS) * g1_ref[...]).astype(BF16)
    rest_ref[:, R_SC:R_Z] = _dot(hn, w_ref[:, W_SC:W_S5])
    rest_ref[:, R_Z:R_DT] = _dot(hn, w_ref[:, W_Z:W_DT])
    rest_ref[:, R_DT:D_REST] = _dot(hn, wdt_ref[...])
    for r0 in range(0, tm, mix_l):
        _mix_chunk(rest_ref, r0, mix_l, scw_ref, mcw_ref, mcb_ref, dtb_ref, alog_ref, dvec_ref, ng_ref,
                   obd_ref, cbuf, mbuf, hst, ybuf)
    cosv = cos_ref[...]
    sinv = sin_ref[...]
    gm = gm_ref[...]
    lane = lax.broadcasted_iota(jnp.int32, (tm, LANES), 1)
    pair_up = (lane % DA_QK) < (ROPE_DIM // 2)
    heads_per_dot = QK_COLS // DA_V

    def qk_heads(col0, gvec, scale):
        y = _dot(hn, w_ref[:, col0:col0 + QK_COLS])
        hi, lo = _split_bf16(y * y, 2)
        ss = _dot(hi, gm) + _dot(lo, gm)
        yn = y * lax.rsqrt(ss * (1.0 / DA_QK) + EPS)
        out = []
        for j in range(heads_per_dot):
            yh = yn[:, j * DA_V:(j + 1) * DA_V] * gvec
            partner = jnp.where(pair_up,
                                pltpu.roll(yh, LANES - ROPE_DIM // 2, 1),
                                pltpu.roll(yh, ROPE_DIM // 2, 1))
            out.append((yh * cosv + partner * sinv) * scale)
        return out

    for h0 in range(0, DA_HEADS, heads_per_dot):
        qs = qk_heads(W_Q + h0 * DA_V, qkg_ref[0:1, :], DA_QK ** -0.5 * LOG2E)
        ks = qk_heads(W_K + h0 * DA_V, qkg_ref[1:2, :], 1.0)
        v = _dot(hn, w_ref[:, W_V + h0 * DA_V:W_V + h0 * DA_V + QK_COLS])
        for j in range(heads_per_dot):
            qt_ref[h0 + j] = qs[j].T.astype(BF16)
            kj = ks[j].astype(BF16)
            vtj = v[:, j * DA_V:(j + 1) * DA_V].T.astype(BF16)
            blk = kb_ref.shape[2]
            for a in range(tm // blk):
                kb_ref[h0 + j, a] = kj[a * blk:(a + 1) * blk]
                vt_ref[h0 + j, a, 0:DA_V, :] = vtj[:, a * blk:(a + 1) * blk]
                vt_ref[h0 + j, a, DA_V:DA_VX, :] = jnp.ones((DA_VX - DA_V, blk), BF16)
    u5 = _dot(hn, w_ref[:, W_S5:W_Z])
    for hh in range(S5_WIDTH // LANES):
        u5_ref[hh] = u5[:, hh * LANES:(hh + 1) * LANES]


def _inproj(x2, g1, w, layer, qkg, cos_t, sin_t, gm, mix_params, bsz, seq, tm, blk, mix_l):
    t = x2.shape[0]
    nt = seq // tm
    row = lambda i: (i, 0)
    tab = lambda i: (i % nt, 0)
    kern = functools.partial(_inproj_kernel, tiles_per_seq=nt, mix_l=mix_l)
    return pl.pallas_call(
        kern,
        grid=(t // tm,),
        in_specs=[
            pl.BlockSpec((tm, D_MODEL), row),
            _const_spec((1, D_MODEL)),
            _layer_spec(w.shape, layer),
            _const_spec((2, LANES)),
            pl.BlockSpec((tm, LANES), tab),
            pl.BlockSpec((tm, LANES), tab),
            _const_spec((QK_COLS, QK_COLS)),
        ] + [_const_spec(a.shape) for a in mix_params],
        out_specs=[
            pl.BlockSpec((None, DA_HEADS, DA_V, tm), lambda i: (i // nt, 0, 0, i % nt)),
            pl.BlockSpec((None, DA_HEADS, tm // blk, blk, DA_V), lambda i: (i // nt, 0, i % nt, 0, 0)),
            pl.BlockSpec((None, DA_HEADS, tm // blk, DA_VX, blk), lambda i: (i // nt, 0, i % nt, 0, 0)),
            pl.BlockSpec((tm, SC_WIDTH + M2_WIDTH), row),
            pl.BlockSpec((S5_WIDTH // LANES, tm, LANES), lambda i: (0, i, 0)),
        ],
        out_shape=[
            jax.ShapeDtypeStruct((bsz, DA_HEADS, DA_V, seq), BF16),
            jax.ShapeDtypeStruct((bsz, DA_HEADS, seq // blk, blk, DA_V), BF16),
            jax.ShapeDtypeStruct((bsz, DA_HEADS, seq // blk, DA_VX, blk), BF16),
            jax.ShapeDtypeStruct((t, SC_WIDTH + M2_WIDTH), BF16),
            jax.ShapeDtypeStruct((S5_WIDTH // LANES, t, LANES), F32),
        ],
        scratch_shapes=[
            pltpu.VMEM((D_MODEL, W_DT), BF16),
            pltpu.VMEM((D_MODEL, LANES), BF16),
            pltpu.VMEM((tm, D_REST), F32),
            pltpu.VMEM((mix_l + HIST, SC_WIDTH), F32),
            pltpu.VMEM((mix_l + HIST, M2_XBC), F32),
            pltpu.VMEM((M2_GROUPS, M2_STATE, 2 * M2_HEAD_DIM), F32),
            pltpu.VMEM((mix_l, M2_WIDTH), F32),
        ],
        compiler_params=pltpu.CompilerParams(
            dimension_semantics=("arbitrary",), vmem_limit_bytes=VMEM_LIMIT),
        name="inproj_mix",
    )(x2, g1, w, qkg, cos_t, sin_t, gm, *mix_params)


def _attn_kernel(qt_ref, k_ref, vt_ref, lam_ref, sg_ref, o_ref, qz_sc, s_sc, m_sc, acc_sc,
                 *, lambda_init):
    qi = pl.program_id(2)
    heads = qt_ref.shape[0]
    streams = [(hd, c) for hd in range(heads) for c in range(2)]
    for n, (hd, c) in enumerate(streams):
        qt = qt_ref[hd]
        comp = lax.broadcasted_iota(jnp.int32, qt.shape, 0) // DA_QK
        qz_sc[n] = jnp.where(comp == c, qt, jnp.zeros_like(qt))
    m_sc[...] = jnp.full(m_sc.shape, -1e30, F32)
    acc_sc[...] = jnp.zeros(acc_sc.shape, F32)

    def scores(j, buf):
        for n, (hd, c) in enumerate(streams):
            s_sc[buf, n] = _dot(k_ref[hd, j], qz_sc[n])

    def absorb(j, buf, masked):
        for n, (hd, c) in enumerate(streams):
            s = s_sc[buf, n]
            if masked:
                kpos = lax.broadcasted_iota(jnp.int32, s.shape, 0)
                qpos = lax.broadcasted_iota(jnp.int32, s.shape, 1)
                s = jnp.where(kpos <= qpos, s, -jnp.inf)
            m_prev = m_sc[n]
            m_new = jnp.maximum(m_prev, jnp.max(s, axis=0, keepdims=True))
            alpha = jnp.exp2(m_prev - m_new)
            p = jnp.exp2(s - m_new).astype(BF16)
            acc_sc[n] = alpha * acc_sc[n] + _dot(vt_ref[hd, j], p)
            m_sc[n] = m_new

    scores(0, 0)
    pairs = qi // 2

    def body(t, carry):
        j = 2 * t
        scores(j + 1, 1)
        absorb(j, 0, False)
        scores(j + 2, 0)
        absorb(j + 1, 1, False)
        return carry

    def body2(t, carry):
        return body(2 * t + 1, body(2 * t, carry))

    lax.fori_loop(0, pairs // 2, body2, 0)

    @pl.when(pairs % 2 == 1)
    def _():
        body(pairs - 1, 0)

    j0 = 2 * pairs

    @pl.when(j0 == qi)
    def _():
        absorb(qi, 0, True)

    @pl.when(j0 != qi)
    def _():
        scores(qi, 1)
        absorb(j0, 0, False)
        absorb(qi, 1, True)

    lp = lam_ref[...]
    lam = (jnp.exp(jnp.sum(lp[0:1] * lp[1:2], axis=1, keepdims=True))
           - jnp.exp(jnp.sum(lp[2:3] * lp[3:4], axis=1, keepdims=True)) + lambda_init)
    for hd in range(heads):
        a0, a1 = acc_sc[2 * hd], acc_sc[2 * hd + 1]
        ot = (a0[0:DA_V] / a0[DA_V:DA_V + 1] - lam * (a1[0:DA_V] / a1[DA_V:DA_V + 1]))
        ms = jnp.mean(ot * ot, axis=0, keepdims=True)
        o = (ot * lax.rsqrt(ms + EPS)).T * (sg_ref[...] * (1.0 - lambda_init))
        o_ref[:, hd * DA_V:(hd + 1) * DA_V] = o.astype(o_ref.dtype)


def _attention(qt, kb, vtb, lam_p, sub_g, lambda_init):
    bsz, _, _, seq = qt.shape
    nb, blk = kb.shape[2], kb.shape[3]
    hps = ATT_HEADS_PER_STEP
    kern = functools.partial(_attn_kernel, lambda_init=lambda_init)
    return pl.pallas_call(
        kern,
        grid=(bsz, DA_HEADS // hps, nb),
        in_specs=[
            pl.BlockSpec((None, hps, DA_V, blk), lambda b, h, i: (b, h, 0, i)),
            pl.BlockSpec((None, hps, nb, blk, DA_V), lambda b, h, i: (b, h, 0, 0, 0)),
            pl.BlockSpec((None, hps, nb, DA_VX, blk), lambda b, h, i: (b, h, 0, 0, 0)),
            pl.BlockSpec((4, DA_QK), lambda b, h, i: (0, 0)),
            pl.BlockSpec((1, DA_V), lambda b, h, i: (0, 0)),
        ],
        out_specs=pl.BlockSpec((None, blk, hps * DA_V), lambda b, h, i: (b, i, h)),
        out_shape=jax.ShapeDtypeStruct((bsz, seq, DA_WIDTH), BF16),
        scratch_shapes=[
            pltpu.VMEM((2 * hps, DA_V, blk), BF16),
            pltpu.VMEM((2, 2 * hps, blk, blk), F32),
            pltpu.VMEM((2 * hps, 1, blk), F32),
            pltpu.VMEM((2 * hps, DA_VX, blk), F32),
        ],
        compiler_params=pltpu.CompilerParams(
            dimension_semantics=("arbitrary", "arbitrary", "arbitrary"),
            vmem_limit_bytes=VMEM_LIMIT),
        name="diff_attn",
    )(qt, kb, vtb, lam_p, sub_g)


def _expand_blockdiag(src_ref, dst_ref, row_group, col_div, col_inner):
    rows, wide = dst_ref.shape[1], dst_ref.shape[2]
    r_s = lax.broadcasted_iota(jnp.int32, (LANES, wide), 0)
    c_s = lax.broadcasted_iota(jnp.int32, (LANES, wide), 1)
    sel = jnp.where(r_s == (c_s // col_div) * col_inner + c_s % col_inner, 1.0, 0.0).astype(BF16)
    step = 256
    r_i = lax.broadcasted_iota(jnp.int32, (step, wide), 0)
    c_i = lax.broadcasted_iota(jnp.int32, (step, wide), 1)
    col_group = (c_i // (col_div // S5_HALF)) % S5_HALF
    for hh in range(dst_ref.shape[0]):
        for r0 in range(0, rows, step):
            same = ((r_i + r0) // row_group) % S5_HALF == col_group
            wide_blk = _dot(src_ref[hh, r0:r0 + step, :], sel)
            dst_ref[hh, r0:r0 + step, :] = jnp.where(same, wide_blk, 0.0).astype(BF16)


def _s5_kernel(u_ref, tc_ref, bc_ref, cc_ref, lam_ref, s5d_ref, wglu_ref, bglu_ref, o_ref,
               toep_ref, bpow_ref, cpow_ref, st, hc, y_ref, *, ls, tiles_per_seq):
    tr = u_ref.shape[1] // ls
    hs = S5_HSTATE

    @pl.when(pl.program_id(0) == 0)
    def _():
        _expand_blockdiag(tc_ref, toep_ref, S5_GROUP, LANES, S5_GROUP)
        _expand_blockdiag(bc_ref, bpow_ref, S5_GROUP, S5_HSTATE, S5_STATE)
        _expand_blockdiag(cc_ref, cpow_ref, S5_STATE, LANES, S5_GROUP)

    @pl.when(pl.program_id(0) % tiles_per_seq == 0)
    def _():
        hc[...] = jnp.zeros(hc.shape, F32)

    def half_input(hh):
        cols = [u_ref[hh, pl.ds(i, tr, stride=ls), :] for i in range(ls)]
        return jnp.concatenate(cols, axis=1).astype(BF16)

    for hh in range(2):
        st[hh] = _dot(half_input(hh), bpow_ref[hh])

    lam = [(lam_ref[hh, 0:1, :], lam_ref[hh, 1:2, :]) for hh in range(2)]

    def body(c, carry):
        new = []
        for hh in range(2):
            hr, hi = carry[2 * hh], carry[2 * hh + 1]
            lr, li = lam[hh]
            s_r = st[hh, pl.ds(c, 1), 0:hs]
            s_i = st[hh, pl.ds(c, 1), hs:2 * hs]
            st[hh, pl.ds(c, 1), 0:hs] = hr
            st[hh, pl.ds(c, 1), hs:2 * hs] = hi
            new.append(lr * hr - li * hi + s_r)
            new.append(lr * hi + li * hr + s_i)
        return tuple(new)

    init = tuple(hc[hh, :, ri * hs:(ri + 1) * hs] for hh in range(2) for ri in range(2))
    fin = init
    for c in range(tr):
        fin = body(c, fin)
    for hh in range(2):
        for ri in range(2):
            hc[hh, :, ri * hs:(ri + 1) * hs] = fin[2 * hh + ri]

    for hh in range(2):
        y = _dot(half_input(hh), toep_ref[hh]) + _dot(st[hh].astype(BF16), cpow_ref[hh])
        for i in range(ls):
            y_ref[hh, pl.ds(i, tr, stride=ls), :] = y[:, i * LANES:(i + 1) * LANES]

    halves = lambda ref: jnp.concatenate([ref[hh] for hh in range(S5_WIDTH // LANES)], axis=1)
    y5 = halves(y_ref) + s5d_ref[...] * halves(u_ref)
    gl = jax.nn.gelu(y5, approximate=True)
    gate = _dot(gl.astype(BF16), wglu_ref[...]) + bglu_ref[...]
    o_ref[...] = (gl * _sigmoid(gate)).astype(o_ref.dtype)


def _s5_scan(u5, tc, bc, cc, lam_l, layer, s5d, wglu, bglu, ls, nc, tr):
    nh, t, w = u5.shape
    rows = tr * ls
    wide = ls * LANES
    kern = functools.partial(_s5_kernel, ls=ls, tiles_per_seq=nc // tr)
    row = lambda i: (0, i, 0)
    return pl.pallas_call(
        kern,
        grid=(t // rows,),
        in_specs=[
            pl.BlockSpec((nh, rows, w), row),
            _layer_spec(tc.shape, layer),
            _layer_spec(bc.shape, layer),
            _layer_spec(cc.shape, layer),
            _layer_spec(lam_l.shape, layer),
            _const_spec(s5d.shape),
            _const_spec(wglu.shape),
            _const_spec(bglu.shape),
        ],
        out_specs=pl.BlockSpec((rows, S5_WIDTH), lambda i: (i, 0)),
        out_shape=jax.ShapeDtypeStruct((t, S5_WIDTH), BF16),
        scratch_shapes=[
            pltpu.VMEM((nh, wide, wide), BF16),
            pltpu.VMEM((nh, wide, 2 * S5_HSTATE), BF16),
            pltpu.VMEM((nh, 2 * S5_HSTATE, wide), BF16),
            pltpu.VMEM((nh, tr, 2 * S5_HSTATE), F32),
            pltpu.VMEM((nh, 1, 2 * S5_HSTATE), F32),
            pltpu.VMEM((nh, rows, w), F32),
        ],
        compiler_params=pltpu.CompilerParams(
            dimension_semantics=("arbitrary",), vmem_limit_bytes=VMEM_LIMIT),
        name="s5_mixer",
    )(u5, tc, bc, cc, lam_l, s5d, wglu, bglu)


def _s5_operators(lam_re, lam_im, log_dt, b_re, b_im, c_re, c_im, ls):
    hp = lax.Precision.HIGHEST
    dt = jnp.exp(log_dt)[:, None]
    zr, zi = lam_re * dt, lam_im * dt
    er = jnp.exp(zr)
    lbr, lbi = er * jnp.cos(zi), er * jnp.sin(zi)
    den = lam_re * lam_re + lam_im * lam_im
    nr, ni = lbr - 1.0, lbi
    fr = (nr * lam_re + ni * lam_im) / den
    fi = (ni * lam_re - nr * lam_im) / den
    bbr = fr[..., None] * b_re - fi[..., None] * b_im
    bbi = fr[..., None] * b_im + fi[..., None] * b_re
    tau = jnp.arange(ls + 1, dtype=F32)
    pe = jnp.exp(zr[..., None] * tau)
    pr, pi_ = pe * jnp.cos(zi[..., None] * tau), pe * jnp.sin(zi[..., None] * tau)
    nh, gh = S5_GROUPS // S5_HALF, S5_HALF
    w = ls * LANES
    lagw = ls * S5_GROUP
    c_rt, c_it = jnp.swapaxes(c_re, 1, 2)[:, :, None, :], jnp.swapaxes(c_im, 1, 2)[:, :, None, :]
    cpr = (c_rt * pr[:, :, :ls, None] - c_it * pi_[:, :, :ls, None]).reshape(S5_GROUPS, S5_STATE, lagw)
    cpi = (c_rt * pi_[:, :, :ls, None] + c_it * pr[:, :, :ls, None]).reshape(S5_GROUPS, S5_STATE, lagw)
    kf = (jnp.matmul(jnp.swapaxes(bbr, 1, 2), cpr, precision=hp)
          - jnp.matmul(jnp.swapaxes(bbi, 1, 2), cpi, precision=hp))
    kt = jnp.stack([jnp.pad(kf[..., :lagw - s * S5_GROUP], ((0, 0), (0, 0), (s * S5_GROUP, 0)))
                    for s in range(ls)], axis=0)
    tc = kt.reshape(ls, nh, gh, S5_GROUP, lagw).transpose(1, 0, 2, 3, 4).reshape(nh, w, lagw)
    tau_rev = (ls - 1) - jnp.arange(ls, dtype=F32)
    pe_rev = jnp.exp(zr[..., None] * tau_rev)
    rev_r, rev_i = pe_rev * jnp.cos(zi[..., None] * tau_rev), pe_rev * jnp.sin(zi[..., None] * tau_rev)
    bre = rev_r[..., None] * bbr[:, :, None, :] - rev_i[..., None] * bbi[:, :, None, :]
    bim = rev_r[..., None] * bbi[:, :, None, :] + rev_i[..., None] * bbr[:, :, None, :]
    bb = jnp.stack([bre, bim], axis=0).reshape(2, nh, gh, S5_STATE, ls, S5_GROUP)
    bc = bb.transpose(1, 4, 2, 5, 0, 3).reshape(nh, w, 2 * S5_STATE)
    qr = c_re[:, :, :, None] * pr[:, None, :, 1:] - c_im[:, :, :, None] * pi_[:, None, :, 1:]
    qi = c_re[:, :, :, None] * pi_[:, None, :, 1:] + c_im[:, :, :, None] * pr[:, None, :, 1:]
    cc = jnp.stack([qr, -qi], axis=0).reshape(2, nh, gh, S5_GROUP, S5_STATE, ls)
    cc = cc.transpose(1, 0, 2, 4, 5, 3).reshape(nh, 2 * S5_HSTATE, ls * S5_GROUP)
    lam_l = jnp.stack([pr[:, :, ls].reshape(nh, S5_HSTATE), pi_[:, :, ls].reshape(nh, S5_HSTATE)], axis=1)
    return tc.astype(BF16), bc.astype(BF16), cc.astype(BF16), lam_l.astype(F32)


def _mix_reset(cbuf, mbuf, hst):
    cbuf[0:HIST, :] = jnp.zeros((HIST, SC_WIDTH), F32)
    mbuf[0:HIST, :] = jnp.zeros((HIST, M2_XBC), F32)
    hst[...] = jnp.zeros(hst.shape, F32)


def _mix_chunk(rest_ref, r0, L, scw_ref, mcw_ref, mcb_ref, dtb_ref, alog_ref, dvec_ref, ng_ref,
               o_ref, cbuf, mbuf, hst, ybuf):
    rows = slice(r0, r0 + L)
    u = rest_ref[rows, R_SC + SC_WIDTH:R_SC + 2 * SC_WIDTH] * rest_ref[rows, R_SC + 2 * SC_WIDTH:R_SC + 3 * SC_WIDTH]
    cbuf[HIST:HIST + L, :] = u
    conv = (scw_ref[2:3, :] * u + scw_ref[1:2, :] * cbuf[HIST - 1:HIST - 1 + L, :]
            + scw_ref[0:1, :] * cbuf[HIST - 2:HIST - 2 + L, :])
    cbuf[0:HIST, :] = cbuf[L:L + HIST, :]
    o_ref[rows, 0:SC_WIDTH] = (rest_ref[rows, R_SC:R_SC + SC_WIDTH] * conv).astype(o_ref.dtype)

    xr = rest_ref[rows, R_XBC:R_XBC + M2_XBC]
    mbuf[HIST:HIST + L, :] = xr
    conv = (mcw_ref[3:4, :] * xr + mcw_ref[2:3, :] * mbuf[HIST - 1:HIST - 1 + L, :]
            + mcw_ref[1:2, :] * mbuf[HIST - 2:HIST - 2 + L, :]
            + mcw_ref[0:1, :] * mbuf[HIST - 3:HIST - 3 + L, :] + mcb_ref[...])
    mbuf[0:HIST, :] = mbuf[L:L + HIST, :]
    xbc = _silu(conv)
    xs = xbc[:, 0:M2_WIDTH]
    dtr = rest_ref[rows, R_DT:R_DT + LANES] + dtb_ref[...]
    dtv = jnp.maximum(dtr, 0.0) + jnp.log(1.0 + jnp.exp(-jnp.abs(dtr)))
    a = -jnp.exp(alog_ref[...]) * dtv
    r_i = lax.broadcasted_iota(jnp.int32, (L, L), 0)
    c_i = lax.broadcasted_iota(jnp.int32, (L, L), 1)
    tril = c_i <= r_i
    tri = jnp.where(tril, 1.0, 0.0).astype(BF16)
    cs = sum(_dot(tri, part) for part in _split_bf16(a, 3))
    cs_t = cs.T
    cs_last = cs[L - 1:L, :]
    ecs = jnp.exp(cs)
    dec = jnp.exp(cs_last - cs)
    ecl = jnp.exp(cs_last)
    lane = lax.broadcasted_iota(jnp.int32, (1, LANES), 1)
    for g in range(M2_GROUPS):
        bg = xbc[:, M2_WIDTH + g * M2_STATE:M2_WIDTH + (g + 1) * M2_STATE].astype(BF16)
        cg = xbc[:, M2_WIDTH + (M2_GROUPS + g) * M2_STATE:M2_WIDTH + (M2_GROUPS + g + 1) * M2_STATE].astype(BF16)
        gram = _dot_nt(cg, bg)
        hs = hst[g]
        yoff = _dot(cg, hs.astype(BF16))
        xdec = []
        for hh in range(2):
            h = 2 * g + hh
            seg = cs[:, h:h + 1] - cs_t[h:h + 1, :]
            dm = jnp.exp(jnp.where(tril, seg, -jnp.inf))
            xdt = xs[:, h * M2_HEAD_DIM:(h + 1) * M2_HEAD_DIM] * dtv[:, h:h + 1]
            yd = _dot((gram * dm).astype(BF16), xdt.astype(BF16))
            yo = yoff[:, hh * M2_HEAD_DIM:(hh + 1) * M2_HEAD_DIM] * ecs[:, h:h + 1]
            ybuf[:, h * M2_HEAD_DIM:(h + 1) * M2_HEAD_DIM] = yd + yo
            xdec.append(xdt * dec[:, h:h + 1])
        upd = _dot_tn(bg, jnp.concatenate(xdec, axis=1).astype(BF16))
        keep = jnp.where(lane < M2_HEAD_DIM, ecl[:, 2 * g:2 * g + 1], ecl[:, 2 * g + 1:2 * g + 2])
        hst[g] = hs * keep + upd
    y = ybuf[...] + dvec_ref[...] * xs
    yg = y * _silu(rest_ref[rows, R_Z:R_Z + M2_WIDTH])
    ms = jnp.mean(yg * yg, axis=-1, keepdims=True)
    o_ref[rows, SC_WIDTH:SC_WIDTH + M2_WIDTH] = (yg * lax.rsqrt(ms + EPS) * ng_ref[...]).astype(o_ref.dtype)


def _ffn_kernel(x_ref, oa_ref, obd_ref, oc_ref, wo_ref, g2_ref, wg_ref, wu_ref, cw_ref, wd_ref,
                o_ref, hist, cb, *, tiles_per_seq, tf):
    tm = x_ref.shape[0]

    @pl.when(pl.program_id(0) % tiles_per_seq == 0)
    def _():
        hist[...] = jnp.zeros(hist.shape, F32)

    r_b = DA_WIDTH
    r_c = r_b + SC_WIDTH
    r_d = r_c + S5_WIDTH
    x1 = (x_ref[...] + _dot(oa_ref[...], wo_ref[0:r_b, :])
          + _dot(obd_ref[:, 0:SC_WIDTH], wo_ref[r_b:r_c, :])
          + _dot(oc_ref[...], wo_ref[r_c:r_d, :])
          + _dot(obd_ref[:, SC_WIDTH:SC_WIDTH + M2_WIDTH], wo_ref[r_d:D_MIX, :]))
    ms = jnp.mean(x1 * x1, axis=-1, keepdims=True)
    h2 = (x1 * lax.rsqrt(ms + EPS) * g2_ref[...]).astype(BF16)
    o_ref[...] = x1
    for c in range(D_FF // tf):
        sl = slice(c * tf, (c + 1) * tf)
        gpre = _dot(h2, wg_ref[:, sl])
        cb[0:HIST, :] = hist[:, sl]
        cb[HIST:HIST + tm, :] = gpre
        gc = (cw_ref[2:3, sl] * gpre + cw_ref[1:2, sl] * cb[HIST - 1:HIST - 1 + tm, :]
              + cw_ref[0:1, sl] * cb[HIST - 2:HIST - 2 + tm, :])
        hist[:, sl] = cb[tm:tm + HIST, :]
        act = (_silu(gc) * _dot(h2, wu_ref[:, sl])).astype(BF16)
        o_ref[...] += _dot(act, wd_ref[sl, :])


def _outproj_ffn(x2, oa, obd, oc, wo, g2, wg, wu, cw, wd, layer, seq, tm, tf):
    t = x2.shape[0]
    kern = functools.partial(_ffn_kernel, tiles_per_seq=seq // tm, tf=tf)
    row = lambda i: (i, 0)
    return pl.pallas_call(
        kern,
        grid=(t // tm,),
        in_specs=[
            pl.BlockSpec((tm, D_MODEL), row),
            pl.BlockSpec((tm, DA_WIDTH), row),
            pl.BlockSpec((tm, SC_WIDTH + M2_WIDTH), row),
            pl.BlockSpec((tm, S5_WIDTH), row),
            _layer_spec(wo.shape, layer),
            _const_spec((1, D_MODEL)),
            _layer_spec(wg.shape, layer),
            _layer_spec(wu.shape, layer),
            _const_spec((3, D_FF)),
            _layer_spec(wd.shape, layer),
        ],
        out_specs=pl.BlockSpec((tm, D_MODEL), row),
        out_shape=jax.ShapeDtypeStruct((t, D_MODEL), F32),
        scratch_shapes=[
            pltpu.VMEM((HIST, D_FF), F32),
            pltpu.VMEM((tm + HIST, tf), F32),
        ],
        compiler_params=pltpu.CompilerParams(
            dimension_semantics=("arbitrary",), vmem_limit_bytes=VMEM_LIMIT),
        name="outproj_ffn",
    )(x2, oa, obd, oc, wo, g2, wg, wu, cw, wd)


def _rope_lane_tables(seq):
    inv = 1.0 / (ROPE_THETA ** (jnp.arange(0, ROPE_DIM, 2, dtype=F32) / ROPE_DIM))
    ang = jnp.arange(seq, dtype=F32)[:, None] * inv[None, :]
    cos, sin = jnp.cos(ang), jnp.sin(ang)
    ones = jnp.ones((seq, DA_QK - ROPE_DIM), F32)
    cos64 = jnp.concatenate([cos, cos, ones], axis=1)
    sin64 = jnp.concatenate([-sin, sin, 0.0 * ones], axis=1)
    return jnp.tile(cos64, (1, LANES // DA_QK)), jnp.tile(sin64, (1, LANES // DA_QK))


def _forward(x, p, blk, mix_l, s5_tr):
    bsz, seq, _ = x.shape
    depth = p["w_in"].shape[0]
    t = bsz * seq
    ls = S5_CHUNK
    nc = seq // ls
    assert seq % blk == 0 and seq % mix_l == 0 and nc % s5_tr == 0

    cos_t, sin_t = _rope_lane_tables(seq)
    comp = lax.broadcasted_iota(jnp.int32, (QK_COLS, QK_COLS), 0) // DA_QK
    gm = (comp == comp.T).astype(BF16)
    w_out = p["w_out"].astype(BF16)
    w_gate, w_up, w_down = (p[k].astype(BF16) for k in ("ffn_w_gate", "ffn_w_up", "ffn_w_down"))
    s5_ops = jax.vmap(functools.partial(_s5_operators, ls=ls))(
        p["s5_lam_re"], p["s5_lam_im"], p["s5_log_dt"], p["s5_b_re"], p["s5_b_im"],
        p["s5_c_re"], p["s5_c_im"])
    x2 = x.reshape(t, D_MODEL)
    for l in range(depth):
        lambda_init = 0.8 - 0.6 * math.exp(-0.3 * l)
        qkg = jnp.tile(p["qk_norm_g"][l], (1, LANES // DA_QK))
        pad4 = lambda a: jnp.pad(a, (0, LANES - M2_HEADS))[None, :]
        mix_params = (p["sc_conv_w"][l], p["m2_conv_w"][l], p["m2_conv_b"][l][None, :],
                      pad4(p["m2_dt_bias"][l]), pad4(p["m2_a_log"][l]),
                      jnp.repeat(p["m2_d"][l], M2_HEAD_DIM)[None, :], p["m2_norm_g"][l][None, :])
        qt, kb, vtb, obd, u5 = _inproj(x2, p["ln1_g"][l][None, :], p["w_in"], l, qkg, cos_t, sin_t,
                                       gm, mix_params, bsz, seq, min(INPROJ_ROWS, seq), blk, mix_l)
        oa = _attention(qt, kb, vtb, p["da_lambda"][l], p["subln_g"][l][None, :], lambda_init)
        oc = _s5_scan(u5, *s5_ops, l, p["s5_d"][l][None, :], p["s5_w_glu"][l].astype(BF16),
                      p["s5_b_glu"][l][None, :], ls, nc, s5_tr)
        x2 = _outproj_ffn(x2, oa.reshape(t, DA_WIDTH), obd, oc, w_out,
                          p["ln2_g"][l][None, :], w_gate, w_up, p["ffn_conv_w"][l], w_down,
                          l, seq, min(FFN_ROWS, seq), FFN_COLS)
    return x2.reshape(bsz, seq, D_MODEL)


def kernel(x, ln1_g, w_in, qk_norm_g, da_lambda, subln_g, sc_conv_w, s5_lam_re, s5_lam_im, s5_log_dt, s5_b_re, s5_b_im, s5_c_re, s5_c_im, s5_d, s5_w_glu, s5_b_glu, m2_conv_w, m2_conv_b, m2_dt_bias, m2_a_log, m2_d, m2_norm_g, w_out, ln2_g, ffn_w_gate, ffn_w_up, ffn_conv_w, ffn_w_down):
    params = dict(ln1_g=ln1_g, w_in=w_in, qk_norm_g=qk_norm_g, da_lambda=da_lambda, subln_g=subln_g,
                  sc_conv_w=sc_conv_w, s5_lam_re=s5_lam_re, s5_lam_im=s5_lam_im, s5_log_dt=s5_log_dt,
                  s5_b_re=s5_b_re, s5_b_im=s5_b_im, s5_c_re=s5_c_re, s5_c_im=s5_c_im, s5_d=s5_d,
                  s5_w_glu=s5_w_glu, s5_b_glu=s5_b_glu, m2_conv_w=m2_conv_w, m2_conv_b=m2_conv_b,
                  m2_dt_bias=m2_dt_bias, m2_a_log=m2_a_log, m2_d=m2_d, m2_norm_g=m2_norm_g,
                  w_out=w_out, ln2_g=ln2_g, ffn_w_gate=ffn_w_gate, ffn_w_up=ffn_w_up,
                  ffn_conv_w=ffn_conv_w, ffn_w_down=ffn_w_down)
    seq = x.shape[1]
    return _forward(x, params, blk=min(512, seq), mix_l=min(256, seq),
                    s5_tr=min(256, seq // S5_CHUNK))
```

```python
import functools
import math

import jax
import jax.numpy as jnp
from jax import lax
from jax.experimental import pallas as pl
from jax.experimental.pallas import tpu as pltpu

F32 = jnp.float32
BF16 = jnp.bfloat16

D_MODEL = 1024
DA_HEADS = 4
DA_QK = 64
DA_V = 2 * DA_QK
DA_VX = DA_V + 16
DA_WIDTH = DA_HEADS * DA_V
ROPE_DIM = DA_QK // 4
ROPE_THETA = 500000.0
SC_WIDTH = 256
S5_WIDTH = 256
S5_GROUP = 16
S5_GROUPS = S5_WIDTH // S5_GROUP
S5_STATE = 64
M2_HEADS = 4
M2_HEAD_DIM = 64
M2_WIDTH = M2_HEADS * M2_HEAD_DIM
M2_GROUPS = 2
M2_STATE = 128
M2_XBC = M2_WIDTH + 2 * M2_GROUPS * M2_STATE
D_MIX = DA_WIDTH + SC_WIDTH + S5_WIDTH + M2_WIDTH
D_FF = 2816
EPS = 1e-6
LOG2E = 1.4426950408889634

LANES = 128
HIST = 8
VMEM_LIMIT = 56 * 1024 * 1024

W_Q = 0
W_K = W_Q + DA_WIDTH
W_V = W_K + DA_WIDTH
W_SC = W_V + DA_WIDTH
W_S5 = W_SC + 3 * SC_WIDTH
W_Z = W_S5 + S5_WIDTH
W_DT = W_Z + M2_WIDTH + M2_XBC
D_PROJ = W_DT + M2_HEADS
R_SC = 0
R_Z = R_SC + 3 * SC_WIDTH
R_XBC = R_Z + M2_WIDTH
R_DT = R_XBC + M2_XBC
D_REST = R_DT + LANES
QK_COLS = 2 * LANES
ATT_HEADS_PER_STEP = 2
INPROJ_ROWS = 1024
FFN_ROWS = 512
FFN_COLS = 2816

S5_CHUNK = 8
S5_HALF = LANES // S5_GROUP
S5_HSTATE = S5_HALF * S5_STATE


def _dot(a, b):
    return jnp.dot(a, b, preferred_element_type=F32)


def _dot_nt(a, b):
    return lax.dot_general(a, b, (((1,), (1,)), ((), ())), preferred_element_type=F32)


def _dot_tn(a, b):
    return lax.dot_general(a, b, (((0,), (0,)), ((), ())), preferred_element_type=F32)


def _split_bf16(x, parts):
    out = []
    r = x
    for _ in range(parts):
        p = r.astype(BF16)
        out.append(p)
        r = r - p.astype(F32)
    return out


def _sigmoid(x):
    return 0.5 + 0.5 * jnp.tanh(0.5 * x)


def _silu(x):
    return x * _sigmoid(x)


def _const_spec(shape):
    nd = len(shape)
    return pl.BlockSpec(shape, lambda *_: (0,) * nd, pipeline_mode=pl.Buffered(1))


def _layer_spec(shape, layer):
    nd = len(shape)
    return pl.BlockSpec((None,) + tuple(shape[1:]), lambda *_: (layer,) + (0,) * (nd - 1),
                        pipeline_mode=pl.Buffered(1))


def _inproj_kernel(x_ref, g1_ref, wf_ref, qkg_ref, cos_ref, sin_ref, gm_ref,
                   scw_ref, mcw_ref, mcb_ref, dtb_ref, alog_ref, dvec_ref, ng_ref,
                   qt_ref, kb_ref, vt_ref, obd_ref, u5_ref,
                   w_ref, wdt_ref, rest_ref, cbuf, mbuf, hst, ybuf, xbuf, *, tiles_per_seq, mix_l):
    tm = x_ref.shape[0]

    @pl.when(pl.program_id(0) == 0)
    def _():
        step = 4 * LANES
        for c0 in range(0, W_DT, step):
            w_ref[:, c0:c0 + step] = wf_ref[:, c0:c0 + step].astype(BF16)
        wdt_ref[...] = jnp.zeros(wdt_ref.shape, BF16)
        wdt_ref[:, 0:M2_HEADS] = wf_ref[:, W_DT:D_PROJ].astype(BF16)

    @pl.when(pl.program_id(0) % tiles_per_seq == 0)
    def _():
        _mix_reset(cbuf, mbuf, hst)

    x = x_ref[...]
    ms = jnp.mean(x * x, axis=-1, keepdims=True)
    hn = (x * lax.rsqrt(ms + EPS) * g1_ref[...]).astype(BF16)
    rest_ref[:, R_SC:R_Z] = _dot(hn, w_ref[:, W_SC:W_S5])
    rest_ref[:, R_Z:R_DT] = _dot(hn, w_ref[:, W_Z:W_DT])
    rest_ref[:, R_DT:D_REST] = _dot(hn, wdt_ref[...])
    for r0 in range(0, tm, mix_l):
        _mix_chunk(rest_ref, r0, mix_l, scw_ref, mcw_ref, mcb_ref, dtb_ref, alog_ref, dvec_ref, ng_ref,
                   obd_ref, cbuf, mbuf, hst, ybuf, xbuf)
    cosv = cos_ref[...]
    sinv = sin_ref[...]
    gm = gm_ref[...]
    lane = lax.broadcasted_iota(jnp.int32, (tm, LANES), 1)
    pair_up = (lane % DA_QK) < (ROPE_DIM // 2)
    heads_per_dot = QK_COLS // DA_V

    def qk_heads(col0, gvec, scale):
        y = _dot(hn, w_ref[:, col0:col0 + QK_COLS])
        hi, lo = _split_bf16(y * y, 2)
        ss = _dot(hi, gm) + _dot(lo, gm)
        yn = y * lax.rsqrt(ss * (1.0 / DA_QK) + EPS)
        out = []
        for j in range(heads_per_dot):
            yh = yn[:, j * DA_V:(j + 1) * DA_V] * gvec
            partner = jnp.where(pair_up,
                                pltpu.roll(yh, LANES - ROPE_DIM // 2, 1),
                                pltpu.roll(yh, ROPE_DIM // 2, 1))
            out.append((yh * cosv + partner * sinv) * scale)
        return out

    for h0 in range(0, DA_HEADS, heads_per_dot):
        qs = qk_heads(W_Q + h0 * DA_V, qkg_ref[0:1, :], DA_QK ** -0.5 * LOG2E)
        ks = qk_heads(W_K + h0 * DA_V, qkg_ref[1:2, :], 1.0)
        v = _dot(hn, w_ref[:, W_V + h0 * DA_V:W_V + h0 * DA_V + QK_COLS])
        for j in range(heads_per_dot):
            qt_ref[h0 + j] = qs[j].T.astype(BF16)
            kj = ks[j].astype(BF16)
            vtj = v[:, j * DA_V:(j + 1) * DA_V].T.astype(BF16)
            blk = kb_ref.shape[2]
            for a in range(tm // blk):
                kb_ref[h0 + j, a] = kj[a * blk:(a + 1) * blk]
                vt_ref[h0 + j, a, 0:DA_V, :] = vtj[:, a * blk:(a + 1) * blk]
                vt_ref[h0 + j, a, DA_V:DA_VX, :] = jnp.ones((DA_VX - DA_V, blk), BF16)
    u5 = _dot(hn, w_ref[:, W_S5:W_Z])
    for hh in range(S5_WIDTH // LANES):
        u5_ref[hh] = u5[:, hh * LANES:(hh + 1) * LANES]


def _inproj(x2, g1, w, layer, qkg, cos_t, sin_t, gm, mix_params, bsz, seq, tm, blk, mix_l):
    t = x2.shape[0]
    nt = seq // tm
    row = lambda i: (i, 0)
    tab = lambda i: (i % nt, 0)
    kern = functools.partial(_inproj_kernel, tiles_per_seq=nt, mix_l=mix_l)
    return pl.pallas_call(
        kern,
        grid=(t // tm,),
        in_specs=[
            pl.BlockSpec((tm, D_MODEL), row),
            _const_spec((1, D_MODEL)),
            _layer_spec(w.shape, layer),
            _const_spec((2, LANES)),
            pl.BlockSpec((tm, LANES), tab),
            pl.BlockSpec((tm, LANES), tab),
            _const_spec((QK_COLS, QK_COLS)),
        ] + [_const_spec(a.shape) for a in mix_params],
        out_specs=[
            pl.BlockSpec((None, DA_HEADS, DA_V, tm), lambda i: (i // nt, 0, 0, i % nt)),
            pl.BlockSpec((None, DA_HEADS, tm // blk, blk, DA_V), lambda i: (i // nt, 0, i % nt, 0, 0)),
            pl.BlockSpec((None, DA_HEADS, tm // blk, DA_VX, blk), lambda i: (i // nt, 0, i % nt, 0, 0)),
            pl.BlockSpec((tm, SC_WIDTH + M2_WIDTH), row),
            pl.BlockSpec((S5_WIDTH // LANES, tm, LANES), lambda i: (0, i, 0)),
        ],
        out_shape=[
            jax.ShapeDtypeStruct((bsz, DA_HEADS, DA_V, seq), BF16),
            jax.ShapeDtypeStruct((bsz, DA_HEADS, seq // blk, blk, DA_V), BF16),
            jax.ShapeDtypeStruct((bsz, DA_HEADS, seq // blk, DA_VX, blk), BF16),
            jax.ShapeDtypeStruct((t, SC_WIDTH + M2_WIDTH), BF16),
            jax.ShapeDtypeStruct((S5_WIDTH // LANES, t, LANES), F32),
        ],
        scratch_shapes=[
            pltpu.VMEM((D_MODEL, W_DT), BF16),
            pltpu.VMEM((D_MODEL, LANES), BF16),
            pltpu.VMEM((tm, D_REST), F32),
            pltpu.VMEM((mix_l + HIST, SC_WIDTH), F32),
            pltpu.VMEM((mix_l + HIST, M2_XBC), F32),
            pltpu.VMEM((M2_GROUPS, M2_STATE, 2 * M2_HEAD_DIM), F32),
            pltpu.VMEM((mix_l, M2_WIDTH), F32),
            pltpu.VMEM((mix_l, M2_XBC), F32),
        ],
        compiler_params=pltpu.CompilerParams(
            dimension_semantics=("arbitrary",), vmem_limit_bytes=VMEM_LIMIT),
        name="inproj_mix",
    )(x2, g1, w, qkg, cos_t, sin_t, gm, *mix_params)


def _attn_kernel(qt_ref, k_ref, vt_ref, lam_ref, sg_ref, o_ref, qz_sc, s_sc, m_sc, acc_sc,
                 *, lambda_init):
    qi = pl.program_id(2)
    heads = qt_ref.shape[0]
    streams = [(hd, c) for hd in range(heads) for c in range(2)]
    for n, (hd, c) in enumerate(streams):
        qt = qt_ref[hd]
        comp = lax.broadcasted_iota(jnp.int32, qt.shape, 0) // DA_QK
        qz_sc[n] = jnp.where(comp == c, qt, jnp.zeros_like(qt))
    m_sc[...] = jnp.full(m_sc.shape, -1e30, F32)
    acc_sc[...] = jnp.zeros(acc_sc.shape, F32)

    def scores(j, buf):
        for n, (hd, c) in enumerate(streams):
            s_sc[buf, n] = _dot(k_ref[hd, j], qz_sc[n])

    def absorb(j, buf, masked):
        for n, (hd, c) in enumerate(streams):
            s = s_sc[buf, n]
            if masked:
                kpos = lax.broadcasted_iota(jnp.int32, s.shape, 0)
                qpos = lax.broadcasted_iota(jnp.int32, s.shape, 1)
                s = jnp.where(kpos <= qpos, s, -jnp.inf)
            m_prev = m_sc[n]
            m_new = jnp.maximum(m_prev, jnp.max(s, axis=0, keepdims=True))
            alpha = jnp.exp2(m_prev - m_new)
            p = jnp.exp2(s - m_new).astype(BF16)
            acc_sc[n] = alpha * acc_sc[n] + _dot(vt_ref[hd, j], p)
            m_sc[n] = m_new

    scores(0, 0)
    pairs = qi // 2

    def body(t, carry):
        j = 2 * t
        scores(j + 1, 1)
        absorb(j, 0, False)
        scores(j + 2, 0)
        absorb(j + 1, 1, False)
        return carry

    def body2(t, carry):
        return body(2 * t + 1, body(2 * t, carry))

    lax.fori_loop(0, pairs // 2, body2, 0)

    @pl.when(pairs % 2 == 1)
    def _():
        body(pairs - 1, 0)

    j0 = 2 * pairs

    @pl.when(j0 == qi)
    def _():
        absorb(qi, 0, True)

    @pl.when(j0 != qi)
    def _():
        scores(qi, 1)
        absorb(j0, 0, False)
        absorb(qi, 1, True)

    lp = lam_ref[...]
    lam = (jnp.exp(jnp.sum(lp[0:1] * lp[1:2], axis=1, keepdims=True))
           - jnp.exp(jnp.sum(lp[2:3] * lp[3:4], axis=1, keepdims=True)) + lambda_init)
    for hd in range(heads):
        a0, a1 = acc_sc[2 * hd], acc_sc[2 * hd + 1]
        ot = (a0[0:DA_V] / a0[DA_V:DA_V + 1] - lam * (a1[0:DA_V] / a1[DA_V:DA_V + 1]))
        ms = jnp.mean(ot * ot, axis=0, keepdims=True)
        o = (ot * lax.rsqrt(ms + EPS)).T * (sg_ref[...] * (1.0 - lambda_init))
        o_ref[:, hd * DA_V:(hd + 1) * DA_V] = o.astype(o_ref.dtype)


def _attention(qt, kb, vtb, lam_p, sub_g, lambda_init):
    bsz, _, _, seq = qt.shape
    nb, blk = kb.shape[2], kb.shape[3]
    hps = ATT_HEADS_PER_STEP
    kern = functools.partial(_attn_kernel, lambda_init=lambda_init)
    return pl.pallas_call(
        kern,
        grid=(bsz, DA_HEADS // hps, nb),
        in_specs=[
            pl.BlockSpec((None, hps, DA_V, blk), lambda b, h, i: (b, h, 0, i)),
            pl.BlockSpec((None, hps, nb, blk, DA_V), lambda b, h, i: (b, h, 0, 0, 0)),
            pl.BlockSpec((None, hps, nb, DA_VX, blk), lambda b, h, i: (b, h, 0, 0, 0)),
            pl.BlockSpec((4, DA_QK), lambda b, h, i: (0, 0)),
            pl.BlockSpec((1, DA_V), lambda b, h, i: (0, 0)),
        ],
        out_specs=pl.BlockSpec((None, blk, hps * DA_V), lambda b, h, i: (b, i, h)),
        out_shape=jax.ShapeDtypeStruct((bsz, seq, DA_WIDTH), BF16),
        scratch_shapes=[
            pltpu.VMEM((2 * hps, DA_V, blk), BF16),
            pltpu.VMEM((2, 2 * hps, blk, blk), F32),
            pltpu.VMEM((2 * hps, 1, blk), F32),
            pltpu.VMEM((2 * hps, DA_VX, blk), F32),
        ],
        compiler_params=pltpu.CompilerParams(
            dimension_semantics=("arbitrary", "arbitrary", "arbitrary"),
            vmem_limit_bytes=VMEM_LIMIT),
        name="diff_attn",
    )(qt, kb, vtb, lam_p, sub_g)


def _expand_blockdiag(src_ref, dst_ref, row_group, col_div, col_inner):
    rows, wide = dst_ref.shape[1], dst_ref.shape[2]
    r_s = lax.broadcasted_iota(jnp.int32, (LANES, wide), 0)
    c_s = lax.broadcasted_iota(jnp.int32, (LANES, wide), 1)
    sel = jnp.where(r_s == (c_s // col_div) * col_inner + c_s % col_inner, 1.0, 0.0).astype(BF16)
    step = 256
    r_i = lax.broadcasted_iota(jnp.int32, (step, wide), 0)
    c_i = lax.broadcasted_iota(jnp.int32, (step, wide), 1)
    col_group = (c_i // (col_div // S5_HALF)) % S5_HALF
    for hh in range(dst_ref.shape[0]):
        for r0 in range(0, rows, step):
            same = ((r_i + r0) // row_group) % S5_HALF == col_group
            wide_blk = _dot(src_ref[hh, r0:r0 + step, :], sel)
            dst_ref[hh, r0:r0 + step, :] = jnp.where(same, wide_blk, 0.0).astype(BF16)


def _s5_kernel(u_ref, tc_ref, bc_ref, cc_ref, lam_ref, s5d_ref, wglu_ref, bglu_ref, o_ref,
               toep_ref, bpow_ref, cpow_ref, st, hc, y_ref, *, ls, tiles_per_seq):
    tr = u_ref.shape[1] // ls
    hs = S5_HSTATE

    @pl.when(pl.program_id(0) == 0)
    def _():
        _expand_blockdiag(tc_ref, toep_ref, S5_GROUP, LANES, S5_GROUP)
        _expand_blockdiag(bc_ref, bpow_ref, S5_GROUP, S5_HSTATE, S5_STATE)
        _expand_blockdiag(cc_ref, cpow_ref, S5_STATE, LANES, S5_GROUP)

    @pl.when(pl.program_id(0) % tiles_per_seq == 0)
    def _():
        hc[...] = jnp.zeros(hc.shape, F32)

    def half_input(hh):
        cols = [u_ref[hh, pl.ds(i, tr, stride=ls), :] for i in range(ls)]
        return jnp.concatenate(cols, axis=1).astype(BF16)

    for hh in range(2):
        st[hh] = _dot(half_input(hh), bpow_ref[hh])

    lam = [(lam_ref[hh, 0:1, :], lam_ref[hh, 1:2, :]) for hh in range(2)]

    def body(c, carry):
        new = []
        for hh in range(2):
            hr, hi = carry[2 * hh], carry[2 * hh + 1]
            lr, li = lam[hh]
            s_r = st[hh, pl.ds(c, 1), 0:hs]
            s_i = st[hh, pl.ds(c, 1), hs:2 * hs]
            st[hh, pl.ds(c, 1), 0:hs] = hr
            st[hh, pl.ds(c, 1), hs:2 * hs] = hi
            new.append(lr * hr - li * hi + s_r)
            new.append(lr * hi + li * hr + s_i)
        return tuple(new)

    init = tuple(hc[hh, :, ri * hs:(ri + 1) * hs] for hh in range(2) for ri in range(2))
    fin = init
    for c in range(tr):
        fin = body(c, fin)
    for hh in range(2):
        for ri in range(2):
            hc[hh, :, ri * hs:(ri + 1) * hs] = fin[2 * hh + ri]

    for hh in range(2):
        y = _dot(half_input(hh), toep_ref[hh]) + _dot(st[hh].astype(BF16), cpow_ref[hh])
        for i in range(ls):
            y_ref[hh, pl.ds(i, tr, stride=ls), :] = y[:, i * LANES:(i + 1) * LANES]

    halves = lambda ref: jnp.concatenate([ref[hh] for hh in range(S5_WIDTH // LANES)], axis=1)
    y5 = halves(y_ref) + s5d_ref[...] * halves(u_ref)
    gl = jax.nn.gelu(y5, approximate=True)
    gate = _dot(gl.astype(BF16), wglu_ref[...]) + bglu_ref[...]
    o_ref[...] = (gl * _sigmoid(gate)).astype(o_ref.dtype)


def _s5_scan(u5, tc, bc, cc, lam_l, layer, s5d, wglu, bglu, ls, nc, tr):
    nh, t, w = u5.shape
    rows = tr * ls
    wide = ls * LANES
    kern = functools.partial(_s5_kernel, ls=ls, tiles_per_seq=nc // tr)
    row = lambda i: (0, i, 0)
    return pl.pallas_call(
        kern,
        grid=(t // rows,),
        in_specs=[
            pl.BlockSpec((nh, rows, w), row),
            _layer_spec(tc.shape, layer),
            _layer_spec(bc.shape, layer),
            _layer_spec(cc.shape, layer),
            _layer_spec(lam_l.shape, layer),
            _const_spec(s5d.shape),
            _const_spec(wglu.shape),
            _const_spec(bglu.shape),
        ],
        out_specs=pl.BlockSpec((rows, S5_WIDTH), lambda i: (i, 0)),
        out_shape=jax.ShapeDtypeStruct((t, S5_WIDTH), BF16),
        scratch_shapes=[
            pltpu.VMEM((nh, wide, wide), BF16),
            pltpu.VMEM((nh, wide, 2 * S5_HSTATE), BF16),
            pltpu.VMEM((nh, 2 * S5_HSTATE, wide), BF16),
            pltpu.VMEM((nh, tr, 2 * S5_HSTATE), F32),
            pltpu.VMEM((nh, 1, 2 * S5_HSTATE), F32),
            pltpu.VMEM((nh, rows, w), F32),
        ],
        compiler_params=pltpu.CompilerParams(
            dimension_semantics=("arbitrary",), vmem_limit_bytes=VMEM_LIMIT),
        name="s5_mixer",
    )(u5, tc, bc, cc, lam_l, s5d, wglu, bglu)


def _s5_operators(lam_re, lam_im, log_dt, b_re, b_im, c_re, c_im, ls):
    hp = lax.Precision.HIGHEST
    dt = jnp.exp(log_dt)[:, None]
    zr, zi = lam_re * dt, lam_im * dt
    er = jnp.exp(zr)
    lbr, lbi = er * jnp.cos(zi), er * jnp.sin(zi)
    den = lam_re * lam_re + lam_im * lam_im
    nr, ni = lbr - 1.0, lbi
    fr = (nr * lam_re + ni * lam_im) / den
    fi = (ni * lam_re - nr * lam_im) / den
    bbr = fr[..., None] * b_re - fi[..., None] * b_im
    bbi = fr[..., None] * b_im + fi[..., None] * b_re
    tau = jnp.arange(ls + 1, dtype=F32)
    pe = jnp.exp(zr[..., None] * tau)
    pr, pi_ = pe * jnp.cos(zi[..., None] * tau), pe * jnp.sin(zi[..., None] * tau)
    nh, gh = S5_GROUPS // S5_HALF, S5_HALF
    w = ls * LANES
    lagw = ls * S5_GROUP
    c_rt, c_it = jnp.swapaxes(c_re, 1, 2)[:, :, None, :], jnp.swapaxes(c_im, 1, 2)[:, :, None, :]
    cpr = (c_rt * pr[:, :, :ls, None] - c_it * pi_[:, :, :ls, None]).reshape(S5_GROUPS, S5_STATE, lagw)
    cpi = (c_rt * pi_[:, :, :ls, None] + c_it * pr[:, :, :ls, None]).reshape(S5_GROUPS, S5_STATE, lagw)
    kf = (jnp.matmul(jnp.swapaxes(bbr, 1, 2), cpr, precision=hp)
          - jnp.matmul(jnp.swapaxes(bbi, 1, 2), cpi, precision=hp))
    kt = jnp.stack([jnp.pad(kf[..., :lagw - s * S5_GROUP], ((0, 0), (0, 0), (s * S5_GROUP, 0)))
                    for s in range(ls)], axis=0)
    tc = kt.reshape(ls, nh, gh, S5_GROUP, lagw).transpose(1, 0, 2, 3, 4).reshape(nh, w, lagw)
    tau_rev = (ls - 1) - jnp.arange(ls, dtype=F32)
    pe_rev = jnp.exp(zr[..., None] * tau_rev)
    rev_r, rev_i = pe_rev * jnp.cos(zi[..., None] * tau_rev), pe_rev * jnp.sin(zi[..., None] * tau_rev)
    bre = rev_r[..., None] * bbr[:, :, None, :] - rev_i[..., None] * bbi[:, :, None, :]
    bim = rev_r[..., None] * bbi[:, :, None, :] + rev_i[..., None] * bbr[:, :, None, :]
    bb = jnp.stack([bre, bim], axis=0).reshape(2, nh, gh, S5_STATE, ls, S5_GROUP)
    bc = bb.transpose(1, 4, 2, 5, 0, 3).reshape(nh, w, 2 * S5_STATE)
    qr = c_re[:, :, :, None] * pr[:, None, :, 1:] - c_im[:, :, :, None] * pi_[:, None, :, 1:]
    qi = c_re[:, :, :, None] * pi_[:, None, :, 1:] + c_im[:, :, :, None] * pr[:, None, :, 1:]
    cc = jnp.stack([qr, -qi], axis=0).reshape(2, nh, gh, S5_GROUP, S5_STATE, ls)
    cc = cc.transpose(1, 0, 2, 4, 5, 3).reshape(nh, 2 * S5_HSTATE, ls * S5_GROUP)
    lam_l = jnp.stack([pr[:, :, ls].reshape(nh, S5_HSTATE), pi_[:, :, ls].reshape(nh, S5_HSTATE)], axis=1)
    return tc.astype(BF16), bc.astype(BF16), cc.astype(BF16), lam_l.astype(F32)


def _mix_reset(cbuf, mbuf, hst):
    cbuf[0:HIST, :] = jnp.zeros((HIST, SC_WIDTH), F32)
    mbuf[0:HIST, :] = jnp.zeros((HIST, M2_XBC), F32)
    hst[...] = jnp.zeros(hst.shape, F32)


def _mix_chunk(rest_ref, r0, L, scw_ref, mcw_ref, mcb_ref, dtb_ref, alog_ref, dvec_ref, ng_ref,
               o_ref, cbuf, mbuf, hst, ybuf, xbuf):
    rows = slice(r0, r0 + L)
    u = rest_ref[rows, R_SC + SC_WIDTH:R_SC + 2 * SC_WIDTH] * rest_ref[rows, R_SC + 2 * SC_WIDTH:R_SC + 3 * SC_WIDTH]
    cbuf[HIST:HIST + L, :] = u
    conv = (scw_ref[2:3, :] * u + scw_ref[1:2, :] * cbuf[HIST - 1:HIST - 1 + L, :]
            + scw_ref[0:1, :] * cbuf[HIST - 2:HIST - 2 + L, :])
    cbuf[0:HIST, :] = cbuf[L:L + HIST, :]
    o_ref[rows, 0:SC_WIDTH] = (rest_ref[rows, R_SC:R_SC + SC_WIDTH] * conv).astype(o_ref.dtype)

    mbuf[HIST:HIST + L, :] = rest_ref[rows, R_XBC:R_XBC + M2_XBC]
    for c0 in range(0, M2_XBC, M2_WIDTH):
        cs_ = slice(c0, c0 + M2_WIDTH)
        conv = (mcw_ref[3:4, cs_] * mbuf[HIST:HIST + L, cs_] + mcw_ref[2:3, cs_] * mbuf[HIST - 1:HIST - 1 + L, cs_]
                + mcw_ref[1:2, cs_] * mbuf[HIST - 2:HIST - 2 + L, cs_]
                + mcw_ref[0:1, cs_] * mbuf[HIST - 3:HIST - 3 + L, cs_] + mcb_ref[:, cs_])
        xbuf[:, cs_] = _silu(conv)
    mbuf[0:HIST, :] = mbuf[L:L + HIST, :]
    xbc = xbuf
    dtr = rest_ref[rows, R_DT:R_DT + LANES] + dtb_ref[...]
    dtv = jnp.maximum(dtr, 0.0) + jnp.log(1.0 + jnp.exp(-jnp.abs(dtr)))
    a = -jnp.exp(alog_ref[...]) * dtv
    r_i = lax.broadcasted_iota(jnp.int32, (L, L), 0)
    c_i = lax.broadcasted_iota(jnp.int32, (L, L), 1)
    tril = c_i <= r_i
    tri = jnp.where(tril, 1.0, 0.0).astype(BF16)
    cs = sum(_dot(tri, part) for part in _split_bf16(a, 3))
    cs_t = cs.T
    cs_last = cs[L - 1:L, :]
    ecs = jnp.exp(cs)
    dec = jnp.exp(cs_last - cs)
    ecl = jnp.exp(cs_last)
    lane = lax.broadcasted_iota(jnp.int32, (1, LANES), 1)
    for g in range(M2_GROUPS):
        bg = xbc[:, M2_WIDTH + g * M2_STATE:M2_WIDTH + (g + 1) * M2_STATE].astype(BF16)
        cg = xbc[:, M2_WIDTH + (M2_GROUPS + g) * M2_STATE:M2_WIDTH + (M2_GROUPS + g + 1) * M2_STATE].astype(BF16)
        gram = _dot_nt(cg, bg)
        hs = hst[g]
        yoff = _dot(cg, hs.astype(BF16))
        xdec = []
        for hh in range(2):
            h = 2 * g + hh
            seg = cs[:, h:h + 1] - cs_t[h:h + 1, :]
            dm = jnp.exp(jnp.where(tril, seg, -jnp.inf))
            xdt = xbuf[:, h * M2_HEAD_DIM:(h + 1) * M2_HEAD_DIM] * dtv[:, h:h + 1]
            yd = _dot((gram * dm).astype(BF16), xdt.astype(BF16))
            yo = yoff[:, hh * M2_HEAD_DIM:(hh + 1) * M2_HEAD_DIM] * ecs[:, h:h + 1]
            ybuf[:, h * M2_HEAD_DIM:(h + 1) * M2_HEAD_DIM] = yd + yo
            xdec.append(xdt * dec[:, h:h + 1])
        upd = _dot_tn(bg, jnp.concatenate(xdec, axis=1).astype(BF16))
        keep = jnp.where(lane < M2_HEAD_DIM, ecl[:, 2 * g:2 * g + 1], ecl[:, 2 * g + 1:2 * g + 2])
        hst[g] = hs * keep + upd
    y = ybuf[...] + dvec_ref[...] * xbuf[:, 0:M2_WIDTH]
    yg = y * _silu(rest_ref[rows, R_Z:R_Z + M2_WIDTH])
    ms = jnp.mean(yg * yg, axis=-1, keepdims=True)
    o_ref[rows, SC_WIDTH:SC_WIDTH + M2_WIDTH] = (yg * lax.rsqrt(ms + EPS) * ng_ref[...]).astype(o_ref.dtype)


def _ffn_kernel(x_ref, oa_ref, obd_ref, oc_ref, wo_ref, g2_ref, wg_ref, wu_ref, cw_ref, wd_ref,
                o_ref, hist, cb, *, tiles_per_seq, tf):
    tm = x_ref.shape[0]

    @pl.when(pl.program_id(0) % tiles_per_seq == 0)
    def _():
        hist[...] = jnp.zeros(hist.shape, F32)

    r_b = DA_WIDTH
    r_c = r_b + SC_WIDTH
    r_d = r_c + S5_WIDTH
    x1 = (x_ref[...] + _dot(oa_ref[...], wo_ref[0:r_b, :])
          + _dot(obd_ref[:, 0:SC_WIDTH], wo_ref[r_b:r_c, :])
          + _dot(oc_ref[...], wo_ref[r_c:r_d, :])
          + _dot(obd_ref[:, SC_WIDTH:SC_WIDTH + M2_WIDTH], wo_ref[r_d:D_MIX, :]))
    ms = jnp.mean(x1 * x1, axis=-1, keepdims=True)
    h2 = (x1 * lax.rsqrt(ms + EPS) * g2_ref[...]).astype(BF16)
    o_ref[...] = x1
    for c in range(D_FF // tf):
        sl = slice(c * tf, (c + 1) * tf)
        gpre = _dot(h2, wg_ref[:, sl])
        cb[0:HIST, :] = hist[:, sl]
        cb[HIST:HIST + tm, :] = gpre
        gc = (cw_ref[2:3, sl] * gpre + cw_ref[1:2, sl] * cb[HIST - 1:HIST - 1 + tm, :]
              + cw_ref[0:1, sl] * cb[HIST - 2:HIST - 2 + tm, :])
        hist[:, sl] = cb[tm:tm + HIST, :]
        act = (_silu(gc) * _dot(h2, wu_ref[:, sl])).astype(BF16)
        o_ref[...] += _dot(act, wd_ref[sl, :])


def _outproj_ffn(x2, oa, obd, oc, wo, g2, wg, wu, cw, wd, layer, seq, tm, tf):
    t = x2.shape[0]
    kern = functools.partial(_ffn_kernel, tiles_per_seq=seq // tm, tf=tf)
    row = lambda i: (i, 0)
    return pl.pallas_call(
        kern,
        grid=(t // tm,),
        in_specs=[
            pl.BlockSpec((tm, D_MODEL), row),
            pl.BlockSpec((tm, DA_WIDTH), row),
            pl.BlockSpec((tm, SC_WIDTH + M2_WIDTH), row),
            pl.BlockSpec((tm, S5_WIDTH), row),
            _layer_spec(wo.shape, layer),
            _const_spec((1, D_MODEL)),
            _layer_spec(wg.shape, layer),
            _layer_spec(wu.shape, layer),
            _const_spec((3, D_FF)),
            _layer_spec(wd.shape, layer),
        ],
        out_specs=pl.BlockSpec((tm, D_MODEL), row),
        out_shape=jax.ShapeDtypeStruct((t, D_MODEL), F32),
        scratch_shapes=[
            pltpu.VMEM((HIST, D_FF), F32),
            pltpu.VMEM((tm + HIST, tf), F32),
        ],
        compiler_params=pltpu.CompilerParams(
            dimension_semantics=("arbitrary",), vmem_limit_bytes=VMEM_LIMIT),
        name="outproj_ffn",
    )(x2, oa, obd, oc, wo, g2, wg, wu, cw, wd)


def _rope_lane_tables(seq):
    inv = 1.0 / (ROPE_THETA ** (jnp.arange(0, ROPE_DIM, 2, dtype=F32) / ROPE_DIM))
    ang = jnp.arange(seq, dtype=F32)[:, None] * inv[None, :]
    cos, sin = jnp.cos(ang), jnp.sin(ang)
    ones = jnp.ones((seq, DA_QK - ROPE_DIM), F32)
    cos64 = jnp.concatenate([cos, cos, ones], axis=1)
    sin64 = jnp.concatenate([-sin, sin, 0.0 * ones], axis=1)
    return jnp.tile(cos64, (1, LANES // DA_QK)), jnp.tile(sin64, (1, LANES // DA_QK))


def _forward(x, p, blk, mix_l, s5_tr):
    bsz, seq, _ = x.shape
    depth = p["w_in"].shape[0]
    t = bsz * seq
    ls = S5_CHUNK
    nc = seq // ls
    assert seq % blk == 0 and seq % mix_l == 0 and nc % s5_tr == 0

    cos_t, sin_t = _rope_lane_tables(seq)
    comp = lax.broadcasted_iota(jnp.int32, (QK_COLS, QK_COLS), 0) // DA_QK
    gm = (comp == comp.T).astype(BF16)
    w_out = p["w_out"].astype(BF16)
    w_gate, w_up, w_down = (p[k].astype(BF16) for k in ("ffn_w_gate", "ffn_w_up", "ffn_w_down"))
    s5_ops = jax.vmap(functools.partial(_s5_operators, ls=ls))(
        p["s5_lam_re"], p["s5_lam_im"], p["s5_log_dt"], p["s5_b_re"], p["s5_b_im"],
        p["s5_c_re"], p["s5_c_im"])
    x2 = x.reshape(t, D_MODEL)
    for l in range(depth):
        lambda_init = 0.8 - 0.6 * math.exp(-0.3 * l)
        qkg = jnp.tile(p["qk_norm_g"][l], (1, LANES // DA_QK))
        pad4 = lambda a: jnp.pad(a, (0, LANES - M2_HEADS))[None, :]
        mix_params = (p["sc_conv_w"][l], p["m2_conv_w"][l], p["m2_conv_b"][l][None, :],
                      pad4(p["m2_dt_bias"][l]), pad4(p["m2_a_log"][l]),
                      jnp.repeat(p["m2_d"][l], M2_HEAD_DIM)[None, :], p["m2_norm_g"][l][None, :])
        qt, kb, vtb, obd, u5 = _inproj(x2, p["ln1_g"][l][None, :], p["w_in"], l, qkg, cos_t, sin_t,
                                       gm, mix_params, bsz, seq, min(INPROJ_ROWS, seq), blk, mix_l)
        oa = _attention(qt, kb, vtb, p["da_lambda"][l], p["subln_g"][l][None, :], lambda_init)
        oc = _s5_scan(u5, *s5_ops, l, p["s5_d"][l][None, :], p["s5_w_glu"][l].astype(BF16),
                      p["s5_b_glu"][l][None, :], ls, nc, s5_tr)
        x2 = _outproj_ffn(x2, oa.reshape(t, DA_WIDTH), obd, oc, w_out,
                          p["ln2_g"][l][None, :], w_gate, w_up, p["ffn_conv_w"][l], w_down,
                          l, seq, min(FFN_ROWS, seq), FFN_COLS)
    return x2.reshape(bsz, seq, D_MODEL)


def kernel(x, ln1_g, w_in, qk_norm_g, da_lambda, subln_g, sc_conv_w, s5_lam_re, s5_lam_im, s5_log_dt, s5_b_re, s5_b_im, s5_c_re, s5_c_im, s5_d, s5_w_glu, s5_b_glu, m2_conv_w, m2_conv_b, m2_dt_bias, m2_a_log, m2_d, m2_norm_g, w_out, ln2_g, ffn_w_gate, ffn_w_up, ffn_conv_w, ffn_w_down):
    params = dict(ln1_g=ln1_g, w_in=w_in, qk_norm_g=qk_norm_g, da_lambda=da_lambda, subln_g=subln_g,
                  sc_conv_w=sc_conv_w, s5_lam_re=s5_lam_re, s5_lam_im=s5_lam_im, s5_log_dt=s5_log_dt,
                  s5_b_re=s5_b_re, s5_b_im=s5_b_im, s5_c_re=s5_c_re, s5_c_im=s5_c_im, s5_d=s5_d,
                  s5_w_glu=s5_w_glu, s5_b_glu=s5_b_glu, m2_conv_w=m2_conv_w, m2_conv_b=m2_conv_b,
                  m2_dt_bias=m2_dt_bias, m2_a_log=m2_a_log, m2_d=m2_d, m2_norm_g=m2_norm_g,
                  w_out=w_out, ln2_g=ln2_g, ffn_w_gate=ffn_w_gate, ffn_w_up=ffn_w_up,
                  ffn_conv_w=ffn_conv_w, ffn_w_down=ffn_w_down)
    seq = x.shape[1]
    return _forward(x, params, blk=min(512, seq), mix_l=min(256, seq),
                    s5_tr=min(256, seq // S5_CHUNK))
```
